```python
import math
import jax, jax.numpy as jnp
from jax import lax
import numpy as np

D_MODEL = 1024
BATCH = 8
SEQ = 4096
DEPTH = 2

D_SSM = D_MODEL // 2
SSM_GROUP = 16
SSM_GROUPS = D_SSM // SSM_GROUP
SSM_STATE = 64
DT_MIN = 1e-3
DT_MAX = 1e-1
D_ATTN = D_MODEL - D_SSM
HEAD_DIM = 64
N_HEADS = D_ATTN // HEAD_DIM
MOBA_BLOCK = 256
MOBA_TOPK = 3
Q_CHUNK = 32
REL_BUCKETS = 32
REL_MAX_DIST = 128
CONV_WIDTH = 31
D_CONV = D_MODEL
FFN_HIDDEN = 2816
FFN_CONV_WIDTH = 3
N_EVEN = (DEPTH + 1) // 2
N_ODD = DEPTH // 2
EPS = 1e-6

kernel_name = 'hybrid_s5_moba_conformer_convffn'


def rms_norm(x, g):
    xf = x.astype(jnp.float32)
    y = xf * lax.rsqrt(jnp.mean(xf * xf, axis=-1, keepdims=True) + EPS)
    return (y * g.astype(jnp.float32)).astype(x.dtype)


def layer_norm(x, g, b):
    xf = x.astype(jnp.float32)
    mu = jnp.mean(xf, axis=-1, keepdims=True)
    xc = xf - mu
    y = xc * lax.rsqrt(jnp.mean(xc * xc, axis=-1, keepdims=True) + EPS)
    return (y * g.astype(jnp.float32) + b.astype(jnp.float32)).astype(x.dtype)


def causal_dwconv(x, w, b):
    k = w.shape[0]
    xp = jnp.pad(x, ((0, 0), (k - 1, 0), (0, 0)))
    y = lax.conv_general_dilated(xp, w[:, None, :], (1,), 'VALID',
                                 dimension_numbers=('NWC', 'WIO', 'NWC'),
                                 feature_group_count=x.shape[-1])
    return y + b


def modulate(h, shift, scale):
    return h * (1.0 + scale[:, None, :]) + shift[:, None, :]


def s5_mixer(u, a_re, a_im, log_dt, b_re, b_im, c_re, c_im, d_skip, glu_w, glu_b):
    bsz, seq_len, _ = u.shape
    f32 = jnp.float32
    uf = u.astype(f32).reshape(bsz, seq_len, SSM_GROUPS, SSM_GROUP)
    lam_re = a_re.astype(f32)
    lam_im = a_im.astype(f32)
    dt = jnp.exp(log_dt.astype(f32))[:, None]
    mag = jnp.exp(lam_re * dt)
    ab_re = mag * jnp.cos(lam_im * dt)
    ab_im = mag * jnp.sin(lam_im * dt)
    den = lam_re * lam_re + lam_im * lam_im
    nr = ab_re - 1.0
    coef_re = (nr * lam_re + ab_im * lam_im) / den
    coef_im = (ab_im * lam_re - nr * lam_im) / den
    br = b_re.astype(f32)
    bi = b_im.astype(f32)
    bb_re = coef_re[..., None] * br - coef_im[..., None] * bi
    bb_im = coef_re[..., None] * bi + coef_im[..., None] * br
    bu_re = jnp.einsum('blgh,gph->blgp', uf, bb_re)
    bu_im = jnp.einsum('blgh,gph->blgp', uf, bb_im)
    a_full_re = jnp.broadcast_to(ab_re, (1, seq_len) + ab_re.shape)
    a_full_im = jnp.broadcast_to(ab_im, (1, seq_len) + ab_im.shape)

    def combine(e1, e2):
        a1r, a1i, b1r, b1i = e1
        a2r, a2i, b2r, b2i = e2
        return (a1r * a2r - a1i * a2i,
                a1r * a2i + a1i * a2r,
                a2r * b1r - a2i * b1i + b2r,
                a2r * b1i + a2i * b1r + b2i)

    _, _, xr, xi = lax.associative_scan(combine, (a_full_re, a_full_im, bu_re, bu_im), axis=1)
    y = (jnp.einsum('blgp,ghp->blgh', xr, c_re.astype(f32))
         - jnp.einsum('blgp,ghp->blgh', xi, c_im.astype(f32))
         + d_skip.astype(f32) * uf)
    y = jax.nn.gelu(y.reshape(bsz, seq_len, D_SSM)).astype(u.dtype)
    return y * jax.nn.sigmoid(y @ glu_w + glu_b)


def rel_bucket(dist):
    n = jnp.maximum(dist, 0)
    max_exact = REL_BUCKETS // 2
    nf = jnp.maximum(n, 1).astype(jnp.float32)
    large = max_exact + (jnp.log(nf / max_exact) / math.log(REL_MAX_DIST / max_exact)
                         * (REL_BUCKETS - max_exact)).astype(jnp.int32)
    large = jnp.minimum(large, REL_BUCKETS - 1)
    return jnp.where(n < max_exact, n, large)


def moba_attention(q, k, v, rel_bias):
    bsz, seq_len = q.shape[:2]
    f32 = jnp.float32
    n_blocks = -(-seq_len // MOBA_BLOCK)
    l_pad = n_blocks * MOBA_BLOCK
    pad = ((0, 0), (0, l_pad - seq_len), (0, 0), (0, 0))
    q, k, v = [jnp.pad(t, pad).transpose(0, 2, 1, 3) for t in (q, k, v)]
    kb = k.reshape(bsz, N_HEADS, n_blocks, MOBA_BLOCK, HEAD_DIM)
    vb = v.reshape(bsz, N_HEADS, n_blocks, MOBA_BLOCK, HEAD_DIM)
    k_mean = jnp.mean(kb.astype(f32), axis=3).astype(q.dtype)
    top = min(MOBA_TOPK, n_blocks)
    scale = HEAD_DIM ** -0.5
    bias_tab = rel_bias.astype(f32)
    b_idx = jnp.arange(bsz)[:, None, None, None]
    h_idx = jnp.arange(N_HEADS)[None, :, None, None]
    h_bias = jnp.arange(N_HEADS)[None, :, None, None, None]
    offs = jnp.arange(MOBA_BLOCK)

    def chunk(ci):
        start = ci * Q_CHUNK
        blk = start // MOBA_BLOCK
        q_c = lax.dynamic_slice_in_dim(q, start, Q_CHUNK, axis=2)
        q_pos = start + jnp.arange(Q_CHUNK)
        gate = jnp.einsum('bhqd,bhnd->bhqn', q_c, k_mean).astype(f32)
        gate = jnp.where(jnp.arange(n_blocks) < blk, gate, -jnp.inf)
        _, sel = lax.top_k(gate, top)
        sel_valid = jnp.arange(top) < blk
        k_sel = kb[b_idx, h_idx, sel]
        v_sel = vb[b_idx, h_idx, sel]
        k_own = lax.dynamic_index_in_dim(kb, blk, axis=2, keepdims=False)
        v_own = lax.dynamic_index_in_dim(vb, blk, axis=2, keepdims=False)
        s_sel = jnp.einsum('bhqd,bhqrtd->bhqrt', q_c, k_sel).astype(f32) * scale
        k_pos_sel = sel[..., None] * MOBA_BLOCK + offs
        s_sel = s_sel + bias_tab[rel_bucket(q_pos[:, None, None] - k_pos_sel), h_bias]
        s_sel = jnp.where(sel_valid[:, None], s_sel, -jnp.inf)
        dist_own = q_pos[:, None] - (blk * MOBA_BLOCK + offs)[None, :]
        s_own = jnp.einsum('bhqd,bhtd->bhqt', q_c, k_own).astype(f32) * scale
        s_own = s_own + bias_tab[rel_bucket(dist_own)].transpose(2, 0, 1)
        s_own = jnp.where(dist_own >= 0, s_own, -jnp.inf)
        scores = jnp.concatenate(
            [s_sel.reshape(bsz, N_HEADS, Q_CHUNK, top * MOBA_BLOCK), s_own], axis=-1)
        p = jax.nn.softmax(scores, axis=-1).astype(v.dtype)
        p_sel = p[..., :top * MOBA_BLOCK].reshape(bsz, N_HEADS, Q_CHUNK, top, MOBA_BLOCK)
        p_own = p[..., top * MOBA_BLOCK:]
        return (jnp.einsum('bhqrt,bhqrtd->bhqd', p_sel, v_sel)
                + jnp.einsum('bhqt,bhtd->bhqd', p_own, v_own))

    out = lax.map(chunk, jnp.arange(l_pad // Q_CHUNK))
    out = out.transpose(1, 0, 3, 2, 4).reshape(bsz, l_pad, N_HEADS * HEAD_DIM)
    return out[:, :seq_len]


def conformer_conv(h, w_in, b_in, dw_w, dw_b, ln_g, ln_b, w_out, b_out):
    a = h @ w_in + b_in
    a = a[..., :D_CONV] * jax.nn.sigmoid(a[..., D_CONV:])
    a = causal_dwconv(a, dw_w, dw_b)
    a = jax.nn.silu(layer_norm(a, ln_g, ln_b))
    return a @ w_out + b_out


def conv_ffn(h, w_up, w_gate, dw_w, dw_b, w_down):
    g = causal_dwconv(h @ w_gate, dw_w, dw_b)
    return (jax.nn.silu(g) * (h @ w_up)) @ w_down


def setup_inputs(seed: int = 0) -> dict:
    key = jax.random.key(seed)
    ks = iter(jax.random.split(key, 40))

    def nrm(shape, scale):
        return scale * jax.random.normal(next(ks), shape, jnp.float32)

    G, H, P = SSM_GROUPS, SSM_GROUP, SSM_STATE
    n_idx = jnp.arange(P, dtype=jnp.float32)
    return {
        'x': nrm((BATCH, SEQ, D_MODEL), 1.0),
        'c': nrm((BATCH, D_MODEL), 1.0),
        'mod_w': nrm((DEPTH, D_MODEL, 6 * D_MODEL), 0.5 * D_MODEL ** -0.5),
        'mod_b': nrm((DEPTH, 6 * D_MODEL), 0.01),
        'norm_g': 1.0 + nrm((DEPTH, 2, D_MODEL), 0.01),
        'final_g': 1.0 + nrm((D_MODEL,), 0.01),
        'ab_w_in': nrm((N_EVEN, D_MODEL, D_SSM + 3 * D_ATTN), D_MODEL ** -0.5),
        'ssm_a_re': -0.5 + nrm((N_EVEN, G, P), 0.01),
        'ssm_a_im': math.pi * n_idx + nrm((N_EVEN, G, P), 0.01),
        'ssm_log_dt': jax.random.uniform(next(ks), (N_EVEN, G), jnp.float32,
                                         math.log(DT_MIN), math.log(DT_MAX)),
        'ssm_b_re': nrm((N_EVEN, G, P, H), (2 * H) ** -0.5),
        'ssm_b_im': nrm((N_EVEN, G, P, H), (2 * H) ** -0.5),
        'ssm_c_re': nrm((N_EVEN, G, H, P), P ** -0.5),
        'ssm_c_im': nrm((N_EVEN, G, H, P), P ** -0.5),
        'ssm_d': nrm((N_EVEN, G, H), 1.0),
        'ssm_glu_w': nrm((N_EVEN, D_SSM, D_SSM), D_SSM ** -0.5),
        'ssm_glu_b': nrm((N_EVEN, D_SSM), 0.01),
        'ab_w_out': nrm((N_EVEN, D_MODEL, D_MODEL), D_MODEL ** -0.5),
        'rel_bias': nrm((REL_BUCKETS, N_HEADS), 0.5),
        'cm_w_in': nrm((N_ODD, D_MODEL, 2 * D_CONV), D_MODEL ** -0.5),
        'cm_b_in': nrm((N_ODD, 2 * D_CONV), 0.01),
        'cm_dw_w': nrm((N_ODD, CONV_WIDTH, D_CONV), CONV_WIDTH ** -0.5),
        'cm_dw_b': nrm((N_ODD, D_CONV), 0.01),
        'cm_ln_g': 1.0 + nrm((N_ODD, D_CONV), 0.01),
        'cm_ln_b': nrm((N_ODD, D_CONV), 0.01),
        'cm_w_out': nrm((N_ODD, D_CONV, D_MODEL), D_CONV ** -0.5),
        'cm_b_out': nrm((N_ODD, D_MODEL), 0.01),
        'ffn_w_up': nrm((DEPTH, D_MODEL, FFN_HIDDEN), D_MODEL ** -0.5),
        'ffn_w_gate': nrm((DEPTH, D_MODEL, FFN_HIDDEN), D_MODEL ** -0.5),
        'ffn_dw_w': nrm((DEPTH, FFN_CONV_WIDTH, FFN_HIDDEN), FFN_CONV_WIDTH ** -0.5),
        'ffn_dw_b': nrm((DEPTH, FFN_HIDDEN), 0.01),
        'ffn_w_down': nrm((DEPTH, FFN_HIDDEN, D_MODEL), FFN_HIDDEN ** -0.5),
    }


def reference(x, c, mod_w, mod_b, norm_g, final_g, ab_w_in, ssm_a_re, ssm_a_im, ssm_log_dt,
              ssm_b_re, ssm_b_im, ssm_c_re, ssm_c_im, ssm_d, ssm_glu_w, ssm_glu_b, ab_w_out,
              rel_bias, cm_w_in, cm_b_in, cm_dw_w, cm_dw_b, cm_ln_g, cm_ln_b, cm_w_out, cm_b_out,
              ffn_w_up, ffn_w_gate, ffn_dw_w, ffn_dw_b, ffn_w_down):
    bsz, seq_len, _ = x.shape
    cs = jax.nn.silu(c)
    for layer in range(DEPTH):
        mod = cs @ mod_w[layer] + mod_b[layer]
        sh1, sc1, g1, sh2, sc2, g2 = jnp.split(mod, 6, axis=-1)
        h = modulate(rms_norm(x, norm_g[layer, 0]), sh1, sc1)
        i = layer // 2
        if layer % 2 == 0:
            proj = h @ ab_w_in[i]
            u = proj[..., :D_SSM]
            qkv = proj[..., D_SSM:].reshape(bsz, seq_len, 3, N_HEADS, HEAD_DIM)
            y_ssm = s5_mixer(u, ssm_a_re[i], ssm_a_im[i], ssm_log_dt[i], ssm_b_re[i],
                             ssm_b_im[i], ssm_c_re[i], ssm_c_im[i], ssm_d[i],
                             ssm_glu_w[i], ssm_glu_b[i])
            y_att = moba_attention(qkv[:, :, 0], qkv[:, :, 1], qkv[:, :, 2], rel_bias)
            y = jnp.concatenate([y_ssm, y_att], axis=-1) @ ab_w_out[i]
        else:
            y = conformer_conv(h, cm_w_in[i], cm_b_in[i], cm_dw_w[i], cm_dw_b[i],
                               cm_ln_g[i], cm_ln_b[i], cm_w_out[i], cm_b_out[i])
        x = x + g1[:, None, :] * y
        h = modulate(rms_norm(x, norm_g[layer, 1]), sh2, sc2)
        x = x + g2[:, None, :] * conv_ffn(h, ffn_w_up[layer], ffn_w_gate[layer],
                                          ffn_dw_w[layer], ffn_dw_b[layer], ffn_w_down[layer])
    return rms_norm(x, final_g)
```

```python
import functools
import math

import numpy as np
import jax
import jax.numpy as jnp
from jax import lax
from jax.experimental import pallas as pl
from jax.experimental.pallas import tpu as pltpu

F32 = jnp.float32
BF16 = jnp.bfloat16

D_MODEL = 1024
D_SSM = 512
SSM_GROUP = 16
SSM_GROUPS = 32
SSM_STATE = 64
D_ATTN = 512
HEAD_DIM = 64
N_HEADS = 8
MOBA_BLOCK = 256
MOBA_TOPK = 3
REL_BUCKETS = 32
REL_MAX_DIST = 128
CONV_WIDTH = 31
FFN_HIDDEN = 2816
FFN_CONV_WIDTH = 3
EPS = 1e-6

NEG = -1e30

V7X_VMEM_BYTES = 64 * 1024 * 1024
VMEM_LIMIT = V7X_VMEM_BYTES - 8 * 1024 * 1024

S5_T = 4
S5_CB = 32
HALF = 256
GROUPS_PER_HALF = HALF // SSM_GROUP
FFN_CHUNK = 256
N_FFN_CHUNKS = FFN_HIDDEN // FFN_CHUNK
CONV_HALO = 32
FFN_HALO = 8


def _sigmoid(x):
    return 0.5 * jnp.tanh(0.5 * x) + 0.5


def _silu(x):
    return x * _sigmoid(x)


def _gelu_tanh(x):
    c = math.sqrt(2.0 / math.pi)
    return 0.5 * x * (1.0 + jnp.tanh(c * (x + 0.044715 * (x * x * x))))


def _norm_mod(x, g, scale, shift):
    ms = jnp.mean(x * x, axis=-1, keepdims=True)
    y = x * lax.rsqrt(ms + EPS) * g
    return y * (1.0 + scale) + shift


def _params(n_axes, vmem=VMEM_LIMIT):
    return pltpu.CompilerParams(dimension_semantics=("arbitrary",) * n_axes, vmem_limit_bytes=vmem)


def _const_spec(shape):
    nd = len(shape)
    return pl.BlockSpec(shape, lambda *_: (0,) * nd, pipeline_mode=pl.Buffered(1))


def _mod_kernel(c_ref, w_ref, b_ref, o_ref):
    c = c_ref[...]
    cs = _silu(c).astype(BF16)
    o_ref[0] = jnp.dot(cs, w_ref[0].astype(BF16), preferred_element_type=F32) + b_ref[0]


def _modulation(c, mod_w, mod_b):
    depth, d, n = mod_w.shape
    bsz = c.shape[0]
    nt = 1536
    return pl.pallas_call(
        _mod_kernel,
        grid=(depth, n // nt),
        in_specs=[
            pl.BlockSpec((bsz, d), lambda l, j: (0, 0)),
            pl.BlockSpec((1, d, nt), lambda l, j: (l, 0, j)),
            pl.BlockSpec((1, 1, nt), lambda l, j: (l, 0, j)),
        ],
        out_specs=pl.BlockSpec((1, bsz, nt), lambda l, j: (l, 0, j)),
        out_shape=jax.ShapeDtypeStruct((depth, bsz, n), F32),
        compiler_params=_params(2),
        name="modulation",
    )(c, mod_w, mod_b.reshape(depth, 1, n))


def _inproj_kernel(x_ref, g_ref, sc_ref, sh_ref, w_ref, u_ref, q_ref, k_ref, v_ref):
    h = _norm_mod(x_ref[...], g_ref[...], sc_ref[0], sh_ref[0]).astype(BF16)
    p = jnp.dot(h, w_ref[...], preferred_element_type=F32)
    u_ref[...] = p[:, :D_SSM]
    q_ref[...] = (p[:, D_SSM:D_SSM + D_ATTN] * (HEAD_DIM ** -0.5)).astype(BF16)
    k_ref[...] = p[:, D_SSM + D_ATTN:D_SSM + 2 * D_ATTN].astype(BF16)
    v_ref[...] = p[:, D_SSM + 2 * D_ATTN:].astype(BF16)


def _in_projection(x2, g, scale, shift, w, seq_len, tm=512):
    tok, d = x2.shape
    per_seq = seq_len // tm
    n = w.shape[1]
    row = lambda i: (i, 0)
    bat = lambda i: (i // per_seq, 0, 0)
    return pl.pallas_call(
        _inproj_kernel,
        grid=(tok // tm,),
        in_specs=[
            pl.BlockSpec((tm, d), row),
            _const_spec((1, d)),
            pl.BlockSpec((1, 1, d), bat),
            pl.BlockSpec((1, 1, d), bat),
            _const_spec((d, n)),
        ],
        out_specs=[
            pl.BlockSpec((tm, D_SSM), row),
            pl.BlockSpec((tm, D_ATTN), row),
            pl.BlockSpec((tm, D_ATTN), row),
            pl.BlockSpec((tm, D_ATTN), row),
        ],
        out_shape=[
            jax.ShapeDtypeStruct((tok, D_SSM), F32),
            jax.ShapeDtypeStruct((tok, D_ATTN), BF16),
            jax.ShapeDtypeStruct((tok, D_ATTN), BF16),
            jax.ShapeDtypeStruct((tok, D_ATTN), BF16),
        ],
        compiler_params=_params(1),
        name="in_projection",
    )(x2, g, scale, shift, w)


def _s5_prep_kernel(lre_r, lim_r, ldt_r, lre_c, lim_c, ldt_c, btr, bti, cre, cim, ctr, cti,
                    kt_ref, sre_ref, sim_ref, ore_ref, oim_ref, at_ref):
    def discretise(lre, lim, ldt):
        dt = jnp.exp(ldt)
        mag = jnp.exp(lre * dt)
        return mag * jnp.cos(lim * dt), mag * jnp.sin(lim * dt)

    lre, lim = lre_r[...], lim_r[...]
    ar, ai = discretise(lre, lim, ldt_r[...])
    den = lre * lre + lim * lim
    nr = ar - 1.0
    coef_re = (nr * lre + ai * lim) / den
    coef_im = (ai * lre - nr * lim) / den
    br, bi = btr[...], bti[...]
    zr = coef_re * br - coef_im * bi
    zi = coef_re * bi + coef_im * br
    c_re, c_im = cre[...], cim[...]
    for k in range(S5_T):
        sre_ref[S5_T - 1 - k] = zr
        sim_ref[S5_T - 1 - k] = zi
        for h in range(SSM_GROUP):
            kt_ref[k, h] = jnp.sum(c_re[:, h:h + 1, :] * zr - c_im[:, h:h + 1, :] * zi, axis=-1)
        zr, zi = ar * zr - ai * zi, ar * zi + ai * zr

    acr, aci = discretise(lre_c[...], lim_c[...], ldt_c[...])
    pr, pi = acr, aci
    ct_re, ct_im = ctr[...], cti[...]
    for t in range(S5_T):
        ore_ref[t] = ct_re * pr - ct_im * pi
        oim_ref[t] = -ct_re * pi - ct_im * pr
        pr, pi = acr * pr - aci * pi, acr * pi + aci * pr

    qr, qi = ar, ai
    for _ in range(S5_T - 1):
        qr, qi = ar * qr - ai * qi, ar * qi + ai * qr
    at_ref[0] = qr
    at_ref[1] = qi


def _s5_prepare(a_re, a_im, log_dt, b_re, b_im, c_re, c_im):
    g, p, h, t = SSM_GROUPS, SSM_STATE, SSM_GROUP, S5_T
    ins = [
        a_re.reshape(g, 1, p), a_im.reshape(g, 1, p), log_dt.reshape(g, 1, 1),
        a_re.reshape(g, p, 1), a_im.reshape(g, p, 1), log_dt.reshape(g, 1, 1),
        b_re.transpose(0, 2, 1), b_im.transpose(0, 2, 1), c_re, c_im,
        c_re.transpose(0, 2, 1), c_im.transpose(0, 2, 1),
    ]
    full = lambda s: pl.BlockSpec(s, lambda: (0,) * len(s))
    out_shapes = [(t, h, g, h), (t, g, h, p), (t, g, h, p), (t, g, p, h), (t, g, p, h), (2, g, 1, p)]
    kt, sre, sim, ore, oim, at = pl.pallas_call(
        _s5_prep_kernel,
        in_specs=[full(a.shape) for a in ins],
        out_specs=[full(s) for s in out_shapes],
        out_shape=[jax.ShapeDtypeStruct(s, F32) for s in out_shapes],
        name="s5_prepare",
    )(*ins)

    gh = GROUPS_PER_HALF
    eye = jnp.eye(gh, dtype=F32)
    ktg = kt.transpose(0, 2, 3, 1).reshape(t, 2, gh, h, h)
    toe = jnp.einsum("kfgab,gc->kfgacb", ktg, eye).reshape(t, 2, HALF, HALF)

    def s_tiles(s):
        sg = s.reshape(t, 2, gh, h, p)
        return jnp.einsum("sfgap,gc->sfgacp", sg, eye).reshape(t, 2, HALF, gh * p)

    def o_tiles(o):
        og = o.reshape(t, 2, gh, p, h)
        return jnp.einsum("tfgpb,gc->tfgpcb", og, eye).reshape(t, 2, gh * p, HALF)

    smat = jnp.concatenate([s_tiles(sre), s_tiles(sim)], axis=-1)
    omat = jnp.concatenate([o_tiles(ore), o_tiles(oim)], axis=-2)
    atr = at[0].reshape(2, gh * p)
    ati = at[1].reshape(2, gh * p)
    return toe.astype(BF16), smat.astype(BF16), omat.astype(BF16), atr, ati


def _s5_kernel(u_ref, toe_ref, smat_ref, omat_ref, atr_ref, ati_ref, d_ref, gw_ref, gb_ref,
               y_ref, acc_sc, s_sc, xp_sc, cr_sc, ci_sc, *, bsz):
    rows = bsz * S5_CB
    width = S5_T * D_SSM
    hs = GROUPS_PER_HALF * SSM_STATE

    @pl.when(pl.program_id(0) == 0)
    def _():
        cr_sc[...] = jnp.zeros_like(cr_sc)
        ci_sc[...] = jnp.zeros_like(ci_sc)

    u = u_ref[...].reshape(rows, width)
    ub = u.astype(BF16)

    def col(s, f):
        return slice(s * D_SSM + f * HALF, s * D_SSM + (f + 1) * HALF)

    lt = 2 * hs // 128
    for f in range(2):
        st = None
        for s in range(S5_T):
            part = jnp.dot(ub[:, col(s, f)], smat_ref[s, f], preferred_element_type=F32)
            st = part if st is None else st + part
        for l in range(lt):
            s_sc[f * lt + l] = st[:, l * 128:(l + 1) * 128]

    for f in range(2):
        for t in range(S5_T):
            acc = None
            for s in range(t + 1):
                part = jnp.dot(ub[:, col(s, f)], toe_ref[t - s, f], preferred_element_type=F32)
                acc = part if acc is None else acc + part
            acc_sc[:, col(t, f)] = acc

    half_lt = lt // 2
    for f in range(2):
        for l in range(half_lt):
            cols = slice(f * hs + l * 128, f * hs + (l + 1) * 128)
            a_r = atr_ref[f:f + 1, l * 128:(l + 1) * 128]
            a_i = ati_ref[f:f + 1, l * 128:(l + 1) * 128]
            xr = cr_sc[:, cols]
            xi = ci_sc[:, cols]
            t_re = f * lt + l
            t_im = f * lt + half_lt + l
            for c in range(S5_CB):
                idx = pl.ds(c, bsz, stride=S5_CB)
                xp_sc[t_re, idx, :] = xr
                xp_sc[t_im, idx, :] = xi
                sr = s_sc[t_re, idx, :]
                si = s_sc[t_im, idx, :]
                xr, xi = a_r * xr - a_i * xi + sr, a_r * xi + a_i * xr + si
            cr_sc[:, cols] = xr
            ci_sc[:, cols] = xi

    xpb = [jnp.concatenate([xp_sc[f * lt + l] for l in range(lt)], axis=-1).astype(BF16) for f in range(2)]
    gw = gw_ref[...]
    for t in range(S5_T):
        ys = []
        for f in range(2):
            carry = jnp.dot(xpb[f], omat_ref[t, f], preferred_element_type=F32)
            cs = col(t, f)
            ys.append(acc_sc[:, cs] + carry + d_ref[:, f * HALF:(f + 1) * HALF] * u[:, cs])
        y = _gelu_tanh(jnp.concatenate(ys, axis=-1))
        z = jnp.dot(y.astype(BF16), gw, preferred_element_type=F32) + gb_ref[...]
        out = (y * _sigmoid(z)).astype(BF16)
        y_ref[:, :, t * D_SSM:(t + 1) * D_SSM] = out.reshape(bsz, S5_CB, D_SSM)


def _s5_mixer(u, ops, d_skip, glu_w, glu_b):
    bsz, seq_len, _ = u.shape
    toe, smat, omat, atr, ati = ops
    width = S5_T * D_SSM
    nchunk = seq_len // S5_T
    rows = bsz * S5_CB
    state_w = 2 * 2 * GROUPS_PER_HALF * SSM_STATE
    u3 = u.reshape(bsz, nchunk, width)
    blk = pl.BlockSpec((bsz, S5_CB, width), lambda i: (0, i, 0))
    y = pl.pallas_call(
        functools.partial(_s5_kernel, bsz=bsz),
        grid=(nchunk // S5_CB,),
        in_specs=[
            blk,
            _const_spec(toe.shape), _const_spec(smat.shape), _const_spec(omat.shape),
            _const_spec(atr.shape), _const_spec(ati.shape),
            _const_spec((1, D_SSM)), _const_spec((D_SSM, D_SSM)), _const_spec((1, D_SSM)),
        ],
        out_specs=blk,
        out_shape=jax.ShapeDtypeStruct((bsz, nchunk, width), BF16),
        scratch_shapes=[
            pltpu.VMEM((rows, width), F32),
            pltpu.VMEM((state_w // 128, rows, 128), F32),
            pltpu.VMEM((state_w // 128, rows, 128), F32),
            pltpu.VMEM((bsz, state_w // 2), F32),
            pltpu.VMEM((bsz, state_w // 2), F32),
        ],
        compiler_params=_params(1),
        name="s5_mixer",
    )(u3, toe, smat, omat, atr, ati, d_skip.reshape(1, D_SSM), glu_w.astype(BF16), glu_b.reshape(1, D_SSM))
    return y.reshape(bsz, seq_len, D_SSM)


def _rel_bucket_np(dist):
    n = np.maximum(dist, 0)
    max_exact = REL_BUCKETS // 2
    nf = np.maximum(n, 1).astype(np.float64)
    large = max_exact + (np.log(nf / max_exact) / math.log(REL_MAX_DIST / max_exact)
                         * (REL_BUCKETS - max_exact)).astype(np.int64)
    large = np.minimum(large, REL_BUCKETS - 1)
    return np.where(n < max_exact, n, large).astype(np.int32)


def _bias_bucket_tiles():
    ko = np.arange(MOBA_BLOCK)[:, None]
    qo = np.arange(MOBA_BLOCK)[None, :]
    own = np.where(qo >= ko, _rel_bucket_np(qo - ko), -1)
    prev = _rel_bucket_np(qo - ko + MOBA_BLOCK)
    return np.stack([own, prev]).astype(np.int32)


assert int(_rel_bucket_np(np.arange(MOBA_BLOCK + 1, 8 * MOBA_BLOCK)).min()) == REL_BUCKETS - 1


def _bias_kernel(tab_ref, idx_ref, o_ref):
    h = pl.program_id(0)
    for t in range(2):
        idx = idx_ref[t]
        acc = jnp.full(idx.shape, NEG, F32)
        for b in range(REL_BUCKETS):
            acc = jnp.where(idx == b, tab_ref[h, b], acc)
        o_ref[0, t] = acc


def _bias_tiles(rel_bias):
    idx = jnp.asarray(_bias_bucket_tiles())
    blk = MOBA_BLOCK
    return pl.pallas_call(
        _bias_kernel,
        grid=(N_HEADS,),
        in_specs=[
            pl.BlockSpec(memory_space=pltpu.SMEM),
            pl.BlockSpec((2, blk, blk), lambda h: (0, 0, 0)),
        ],
        out_specs=pl.BlockSpec((1, 2, blk, blk), lambda h: (h, 0, 0, 0)),
        out_shape=jax.ShapeDtypeStruct((N_HEADS, 2, blk, blk), F32),
        compiler_params=_params(1),
        name="moba_bias_tiles",
    )(rel_bias.T, idx)


def _attn_kernel(far_ref, q_ref, k_ref, v_ref, bias_ref, o_ref,
                 kmean_sc, vt_sc, mfar_sc, msel_sc, m_sc, l_sc, acc_sc, *, nb):
    hp = pl.program_id(1)
    qt = pl.program_id(2)
    blk = MOBA_BLOCK
    lane = lax.broadcasted_iota(jnp.int32, (1, 2 * HEAD_DIM), 1)
    head_mask = [lane < HEAD_DIM, lane >= HEAD_DIM]
    nt = (((1,), (1,)), ((), ()))

    @pl.when(qt == 0)
    def _():
        for j in range(nb):
            kb = k_ref[0, j * blk:(j + 1) * blk, :].astype(F32)
            km = jnp.mean(kb, axis=0, keepdims=True)
            for hh in range(2):
                kmean_sc[hh, j:j + 1, :] = jnp.where(head_mask[hh], km, 0.0)
            vt_sc[j] = v_ref[0, j * blk:(j + 1) * blk, :].astype(F32).T.astype(BF16)

    q2 = q_ref[0]
    jidx = lax.broadcasted_iota(jnp.int32, (nb, blk), 0)
    qm = []
    for hh in range(2):
        qm.append(jnp.where(head_mask[hh], q2, jnp.zeros_like(q2)))
        gate = lax.dot_general(kmean_sc[hh].astype(BF16), q2, nt, preferred_element_type=F32)
        rank = jnp.zeros((nb, blk), F32)
        for jp in range(nb):
            row = gate[jp:jp + 1, :]
            beats = (row > gate) | ((row == gate) & (jidx > jp))
            rank = rank + jnp.where(beats & (qt > jp), 1.0, 0.0)
        sel = (rank < float(MOBA_TOPK)) & (jidx < qt)
        mfar_sc[hh] = jnp.where(sel, far_ref[2 * hp + hh], NEG)
        msel_sc[hh] = jnp.where(sel, 0.0, NEG)

    def scores(hh, j):
        kb = k_ref[0, pl.ds(pl.multiple_of(j * blk, blk), blk), :]
        return lax.dot_general(kb, qm[hh], nt, preferred_element_type=F32)

    def pv(hh, j, p):
        vt = vt_sc[j, hh * HEAD_DIM:(hh + 1) * HEAD_DIM, :]
        return jnp.dot(vt, p.astype(BF16), preferred_element_type=F32)

    for hh in range(2):
        s = scores(hh, qt) + bias_ref[hh, 0]
        m = jnp.max(s, axis=0, keepdims=True)
        p = jnp.exp(s - m)
        m_sc[hh] = m
        l_sc[hh] = jnp.sum(p, axis=0, keepdims=True)
        acc_sc[hh] = pv(hh, qt, p)

    def update(hh, j, s):
        m_old = m_sc[hh]
        m_new = jnp.maximum(m_old, jnp.max(s, axis=0, keepdims=True))
        alpha = jnp.exp(m_old - m_new)
        p = jnp.exp(s - m_new)
        m_sc[hh] = m_new
        l_sc[hh] = alpha * l_sc[hh] + jnp.sum(p, axis=0, keepdims=True)
        acc_sc[hh] = alpha * acc_sc[hh] + pv(hh, j, p)

    @pl.when(qt >= 1)
    def _():
        j = qt - 1
        for hh in range(2):
            update(hh, j, scores(hh, j) + bias_ref[hh, 1] + msel_sc[hh, pl.ds(j, 1), :])

    def far_body(j, carry):
        for hh in range(2):
            update(hh, j, scores(hh, j) + mfar_sc[hh, pl.ds(j, 1), :])
        return carry

    lax.fori_loop(0, jnp.maximum(qt - 1, 0), far_body, 0)

    out = jnp.concatenate([acc_sc[hh] / l_sc[hh] for hh in range(2)], axis=0)
    o_ref[0] = out.T.astype(BF16)


def _moba_attention(q, k, v, bias_tiles, far_bias):
    bsz, seq_len, _ = q.shape
    blk = MOBA_BLOCK
    nb = seq_len // blk
    pair = 2 * HEAD_DIM
    return pl.pallas_call(
        functools.partial(_attn_kernel, nb=nb),
        grid=(bsz, N_HEADS // 2, nb),
        in_specs=[
            pl.BlockSpec(memory_space=pltpu.SMEM),
            pl.BlockSpec((1, blk, pair), lambda b, h, i: (b, i, h)),
            pl.BlockSpec((1, seq_len, pair), lambda b, h, i: (b, 0, h)),
            pl.BlockSpec((1, seq_len, pair), lambda b, h, i: (b, 0, h)),
            pl.BlockSpec((2, 2, blk, blk), lambda b, h, i: (h, 0, 0, 0)),
        ],
        out_specs=pl.BlockSpec((1, blk, pair), lambda b, h, i: (b, i, h)),
        out_shape=jax.ShapeDtypeStruct((bsz, seq_len, D_ATTN), BF16),
        scratch_shapes=[
            pltpu.VMEM((2, nb, pair), F32),
            pltpu.VMEM((nb, pair, blk), BF16),
            pltpu.VMEM((2, nb, blk), F32),
            pltpu.VMEM((2, nb, blk), F32),
            pltpu.VMEM((2, 1, blk), F32),
            pltpu.VMEM((2, 1, blk), F32),
            pltpu.VMEM((2, HEAD_DIM, blk), F32),
        ],
        compiler_params=_params(3),
        name="moba_attention",
    )(far_bias, q, k, v, bias_tiles)


def _ffn_body(x, g_ref, sc_ref, sh_ref, gate_ref, wg_ref, wu_ref, cw_ref, wd_ref, fin_ref, o_ref,
              gext_sc, carry_sc, acc_sc, h_sc, *, tm, per_seq, final_norm):
    @pl.when(pl.program_id(0) % per_seq == 0)
    def _():
        carry_sc[...] = jnp.zeros_like(carry_sc)

    h_sc[...] = _norm_mod(x, g_ref[...], sc_ref[0], sh_ref[0]).astype(BF16)
    acc_sc[...] = jnp.zeros_like(acc_sc)

    def chunk(j, carry):
        h = h_sc[...]
        gpre = jnp.dot(h, wg_ref[j], preferred_element_type=F32)
        up = jnp.dot(h, wu_ref[j], preferred_element_type=F32)
        gext_sc[0:FFN_HALO, :] = carry_sc[j]
        gext_sc[FFN_HALO:, :] = gpre
        carry_sc[j] = gpre[tm - FFN_HALO:, :]
        cw = cw_ref[j]
        conv = (cw[0:1] * gext_sc[FFN_HALO - 2:FFN_HALO - 2 + tm, :]
                + cw[1:2] * gext_sc[FFN_HALO - 1:FFN_HALO - 1 + tm, :]
                + cw[2:3] * gpre + cw[3:4])
        act = (_silu(conv) * up).astype(BF16)
        acc_sc[...] += jnp.dot(act, wd_ref[j], preferred_element_type=F32)
        return carry

    lax.fori_loop(0, N_FFN_CHUNKS, chunk, 0)
    out = x + gate_ref[0] * acc_sc[...]
    if final_norm:
        ms = jnp.mean(out * out, axis=-1, keepdims=True)
        out = out * lax.rsqrt(ms + EPS) * fin_ref[...]
    o_ref[...] = out


def _ffn0_kernel(x_ref, ys_ref, ya_ref, wo_ref, g1_ref, g_ref, sc_ref, sh_ref, gate_ref,
                 wg_ref, wu_ref, cw_ref, wd_ref, fin_ref, o_ref, gext_sc, carry_sc, acc_sc, h_sc, **kw):
    y = (jnp.dot(ys_ref[...], wo_ref[0:D_SSM, :], preferred_element_type=F32)
         + jnp.dot(ya_ref[...], wo_ref[D_SSM:, :], preferred_element_type=F32))
    x = x_ref[...] + g1_ref[0] * y
    _ffn_body(x, g_ref, sc_ref, sh_ref, gate_ref, wg_ref, wu_ref, cw_ref, wd_ref, fin_ref, o_ref,
              gext_sc, carry_sc, acc_sc, h_sc, **kw)


def _ffn1_kernel(x_ref, g_ref, sc_ref, sh_ref, gate_ref, wg_ref, wu_ref, cw_ref, wd_ref, fin_ref,
                 o_ref, gext_sc, carry_sc, acc_sc, h_sc, **kw):
    _ffn_body(x_ref[...], g_ref, sc_ref, sh_ref, gate_ref, wg_ref, wu_ref, cw_ref, wd_ref, fin_ref, o_ref,
              gext_sc, carry_sc, acc_sc, h_sc, **kw)


def _ffn_weights(w_up, w_gate, dw_w, dw_b, w_down):
    d = w_up.shape[0]
    split = lambda w: w.reshape(d, N_FFN_CHUNKS, FFN_CHUNK).transpose(1, 0, 2).astype(BF16)
    cw = jnp.concatenate([dw_w, dw_b[None, :]], axis=0)
    cw = cw.reshape(FFN_CONV_WIDTH + 1, N_FFN_CHUNKS, FFN_CHUNK).transpose(1, 0, 2)
    return split(w_gate), split(w_up), cw, w_down.reshape(N_FFN_CHUNKS, FFN_CHUNK, d).astype(BF16)


def _conv_ffn(x2, mixer, norm_g, scale, shift, gate, weights, final_g, seq_len, final_norm, tm=512):
    tok, d = x2.shape
    per_seq = seq_len // tm
    wg, wu, cw, wd = weights
    row = lambda i: (i, 0)
    bat = lambda i: (i // per_seq, 0, 0)
    vec = pl.BlockSpec((1, 1, d), bat)
    common_specs = [_const_spec((1, d)), vec, vec, vec,
                    _const_spec(wg.shape), _const_spec(wu.shape), _const_spec(cw.shape), _const_spec(wd.shape),
                    _const_spec((1, d))]
    common_args = [norm_g, scale, shift, gate, wg, wu, cw, wd, final_g]
    kw = dict(tm=tm, per_seq=per_seq, final_norm=final_norm)
    if mixer is None:
        body = functools.partial(_ffn1_kernel, **kw)
        specs = [pl.BlockSpec((tm, d), row)] + common_specs
        args = [x2] + common_args
    else:
        ys, ya, wo, g1 = mixer
        body = functools.partial(_ffn0_kernel, **kw)
        specs = [pl.BlockSpec((tm, d), row), pl.BlockSpec((tm, D_SSM), row), pl.BlockSpec((tm, D_ATTN), row),
                 _const_spec(wo.shape), vec] + common_specs
        args = [x2, ys, ya, wo, g1] + common_args
    return pl.pallas_call(
        body,
        grid=(tok // tm,),
        in_specs=specs,
        out_specs=pl.BlockSpec((tm, d), row),
        out_shape=jax.ShapeDtypeStruct((tok, d), F32),
        scratch_shapes=[
            pltpu.VMEM((tm + FFN_HALO, FFN_CHUNK), F32),
            pltpu.VMEM((N_FFN_CHUNKS, FFN_HALO, FFN_CHUNK), F32),
            pltpu.VMEM((tm, d), F32),
            pltpu.VMEM((tm, d), BF16),
        ],
        compiler_params=_params(1),
        name="conv_ffn_final" if final_norm else "conv_ffn",
    )(*args)


def _conformer_kernel(x_ref, g_ref, sc_ref, sh_ref, gate_ref, win_ref, bin_ref, dw_ref, dwb_ref,
                      lng_ref, lnb_ref, wout_ref, bout_ref, o_ref, aext_sc, conv_sc, *, tm, per_seq):
    x = x_ref[...]
    d = x.shape[-1]
    h = _norm_mod(x, g_ref[...], sc_ref[0], sh_ref[0]).astype(BF16)
    a = jnp.dot(h, win_ref[...], preferred_element_type=F32) + bin_ref[...]
    a = a[:, :d] * _sigmoid(a[:, d:])

    @pl.when(pl.program_id(0) % per_seq == 0)
    def _():
        aext_sc[0:CONV_HALO, :] = jnp.zeros((CONV_HALO, d), F32)

    aext_sc[CONV_HALO:, :] = a

    rc = 32
    off = CONV_HALO - (CONV_WIDTH - 1)

    for base in range(0, tm, rc):
        acc = jnp.zeros((rc, d), F32) + dwb_ref[...]
        for k in range(CONV_WIDTH):
            acc = acc + dw_ref[k:k + 1, :] * aext_sc[base + off + k:base + off + k + rc, :]
        conv_sc[base:base + rc, :] = acc
    aext_sc[0:CONV_HALO, :] = aext_sc[tm:tm + CONV_HALO, :]

    c = conv_sc[...]
    mu = jnp.mean(c, axis=-1, keepdims=True)
    xc = c - mu
    y = xc * lax.rsqrt(jnp.mean(xc * xc, axis=-1, keepdims=True) + EPS)
    y = _silu(y * lng_ref[...] + lnb_ref[...]).astype(BF16)
    out = jnp.dot(y, wout_ref[...], preferred_element_type=F32) + bout_ref[...]
    o_ref[...] = x + gate_ref[0] * out


def _conformer(x2, norm_g, scale, shift, gate, w_in, b_in, dw_w, dw_b, ln_g, ln_b, w_out, b_out, seq_len, tm=256):
    tok, d = x2.shape
    per_seq = seq_len // tm
    row = lambda i: (i, 0)
    bat = lambda i: (i // per_seq, 0, 0)
    vec = pl.BlockSpec((1, 1, d), bat)
    return pl.pallas_call(
        functools.partial(_conformer_kernel, tm=tm, per_seq=per_seq),
        grid=(tok // tm,),
        in_specs=[pl.BlockSpec((tm, d), row), _const_spec((1, d)), vec, vec, vec,
                  _const_spec((d, 2 * d)), _const_spec((1, 2 * d)), _const_spec((CONV_WIDTH, d)), _const_spec((1, d)),
                  _const_spec((1, d)), _const_spec((1, d)), _const_spec((d, d)), _const_spec((1, d))],
        out_specs=pl.BlockSpec((tm, d), row),
        out_shape=jax.ShapeDtypeStruct((tok, d), F32),
        scratch_shapes=[pltpu.VMEM((tm + CONV_HALO, d), F32), pltpu.VMEM((tm, d), F32)],
        compiler_params=_params(1),
        name="conformer_conv",
    )(x2, norm_g, scale, shift, gate, w_in.astype(BF16), b_in.reshape(1, -1), dw_w, dw_b.reshape(1, -1),
      ln_g.reshape(1, -1), ln_b.reshape(1, -1), w_out.astype(BF16), b_out.reshape(1, -1))


def kernel(x, c, mod_w, mod_b, norm_g, final_g, ab_w_in, ssm_a_re, ssm_a_im, ssm_log_dt, ssm_b_re, ssm_b_im, ssm_c_re, ssm_c_im, ssm_d, ssm_glu_w, ssm_glu_b, ab_w_out, rel_bias, cm_w_in, cm_b_in, cm_dw_w, cm_dw_b, cm_ln_g, cm_ln_b, cm_w_out, cm_b_out, ffn_w_up, ffn_w_gate, ffn_dw_w, ffn_dw_b, ffn_w_down):
    bsz, seq_len, d = x.shape
    tok = bsz * seq_len
    x2 = x.reshape(tok, d)
    mod = _modulation(c, mod_w, mod_b)
    vecs = [[mod[l, :, i * d:(i + 1) * d].reshape(bsz, 1, d) for i in range(6)] for l in range(2)]
    fin = final_g.reshape(1, d)

    sh1, sc1, g1, sh2, sc2, g2 = vecs[0]
    u, q, k, v = _in_projection(x2, norm_g[0, 0].reshape(1, d), sc1, sh1, ab_w_in[0].astype(BF16), seq_len)
    ops = _s5_prepare(ssm_a_re[0], ssm_a_im[0], ssm_log_dt[0], ssm_b_re[0], ssm_b_im[0], ssm_c_re[0], ssm_c_im[0])
    y_ssm = _s5_mixer(u.reshape(bsz, seq_len, D_SSM), ops, ssm_d[0], ssm_glu_w[0], ssm_glu_b[0])
    att = lambda a: a.reshape(bsz, seq_len, D_ATTN)
    y_att = _moba_attention(att(q), att(k), att(v), _bias_tiles(rel_bias), rel_bias[REL_BUCKETS - 1])
    w0 = _ffn_weights(ffn_w_up[0], ffn_w_gate[0], ffn_dw_w[0], ffn_dw_b[0], ffn_w_down[0])
    x2 = _conv_ffn(x2, (y_ssm.reshape(tok, D_SSM), y_att.reshape(tok, D_ATTN), ab_w_out[0].astype(BF16), g1),
                   norm_g[0, 1].reshape(1, d), sc2, sh2, g2, w0, fin, seq_len, final_norm=False)

    sh1, sc1, g1, sh2, sc2, g2 = vecs[1]
    x2 = _conformer(x2, norm_g[1, 0].reshape(1, d), sc1, sh1, g1, cm_w_in[0], cm_b_in[0], cm_dw_w[0], cm_dw_b[0],
                    cm_ln_g[0], cm_ln_b[0], cm_w_out[0], cm_b_out[0], seq_len)
    w1 = _ffn_weights(ffn_w_up[1], ffn_w_gate[1], ffn_dw_w[1], ffn_dw_b[1], ffn_w_down[1])
    x2 = _conv_ffn(x2, None, norm_g[1, 1].reshape(1, d), sc2, sh2, g2, w1, fin, seq_len, final_norm=True)
    return x2.reshape(bsz, seq_len, d)
```

```python
import functools
import math

import numpy as np
import jax
import jax.numpy as jnp
from jax import lax
from jax.experimental import pallas as pl
from jax.experimental.pallas import tpu as pltpu

F32 = jnp.float32
BF16 = jnp.bfloat16

D_MODEL = 1024
D_SSM = 512
SSM_GROUP = 16
SSM_GROUPS = 32
SSM_STATE = 64
D_ATTN = 512
HEAD_DIM = 64
N_HEADS = 8
MOBA_BLOCK = 256
MOBA_TOPK = 3
REL_BUCKETS = 32
REL_MAX_DIST = 128
CONV_WIDTH = 31
FFN_HIDDEN = 2816
FFN_CONV_WIDTH = 3
EPS = 1e-6

NEG = -1e30
LOG2E = math.log2(math.e)

V7X_VMEM_BYTES = 64 * 1024 * 1024
VMEM_LIMIT = V7X_VMEM_BYTES - 8 * 1024 * 1024

S5_T = 4
S5_CB = 32
HALF = 256
GROUPS_PER_HALF = HALF // SSM_GROUP
FFN_CHUNK = 256
N_FFN_CHUNKS = FFN_HIDDEN // FFN_CHUNK
CONV_HALO = 32
FFN_HALO = 8
ATTN_GROUP = 4


def _sigmoid(x):
    return 0.5 * jnp.tanh(0.5 * x) + 0.5


def _silu(x):
    return x * _sigmoid(x)


def _gelu_tanh(x):
    c = math.sqrt(2.0 / math.pi)
    return 0.5 * x * (1.0 + jnp.tanh(c * (x + 0.044715 * (x * x * x))))


def _norm_mod(x, g, scale, shift):
    ms = jnp.mean(x * x, axis=-1, keepdims=True)
    y = x * lax.rsqrt(ms + EPS) * g
    return y * (1.0 + scale) + shift


def _params(n_axes, vmem=VMEM_LIMIT):
    return pltpu.CompilerParams(dimension_semantics=("arbitrary",) * n_axes, vmem_limit_bytes=vmem)


def _const_spec(shape):
    nd = len(shape)
    return pl.BlockSpec(shape, lambda *_: (0,) * nd, pipeline_mode=pl.Buffered(1))


def _mod_kernel(c_ref, w_ref, b_ref, o_ref):
    c = c_ref[...]
    cs = _silu(c).astype(BF16)
    o_ref[0] = jnp.dot(cs, w_ref[0].astype(BF16), preferred_element_type=F32) + b_ref[0]


def _modulation(c, mod_w, mod_b):
    depth, d, n = mod_w.shape
    bsz = c.shape[0]
    nt = 1536
    return pl.pallas_call(
        _mod_kernel,
        grid=(depth, n // nt),
        in_specs=[
            pl.BlockSpec((bsz, d), lambda l, j: (0, 0)),
            pl.BlockSpec((1, d, nt), lambda l, j: (l, 0, j)),
            pl.BlockSpec((1, 1, nt), lambda l, j: (l, 0, j)),
        ],
        out_specs=pl.BlockSpec((1, bsz, nt), lambda l, j: (l, 0, j)),
        out_shape=jax.ShapeDtypeStruct((depth, bsz, n), F32),
        compiler_params=_params(2),
        name="modulation",
    )(c, mod_w, mod_b.reshape(depth, 1, n))


def _inproj_kernel(x_ref, g_ref, sc_ref, sh_ref, w_ref, u_ref, q_ref, k_ref, v_ref):
    h = _norm_mod(x_ref[...], g_ref[...], sc_ref[0], sh_ref[0]).astype(BF16)
    p = jnp.dot(h, w_ref[...], preferred_element_type=F32)
    u_ref[...] = p[:, :D_SSM]
    q_ref[...] = (p[:, D_SSM:D_SSM + D_ATTN] * (HEAD_DIM ** -0.5 * LOG2E)).astype(BF16)
    k_ref[...] = p[:, D_SSM + D_ATTN:D_SSM + 2 * D_ATTN].astype(BF16)
    v_ref[...] = p[:, D_SSM + 2 * D_ATTN:].astype(BF16)


def _in_projection(x2, g, scale, shift, w, seq_len, tm=512):
    tok, d = x2.shape
    per_seq = seq_len // tm
    n = w.shape[1]
    row = lambda i: (i, 0)
    bat = lambda i: (i // per_seq, 0, 0)
    return pl.pallas_call(
        _inproj_kernel,
        grid=(tok // tm,),
        in_specs=[
            pl.BlockSpec((tm, d), row),
            _const_spec((1, d)),
            pl.BlockSpec((1, 1, d), bat),
            pl.BlockSpec((1, 1, d), bat),
            _const_spec((d, n)),
        ],
        out_specs=[
            pl.BlockSpec((tm, D_SSM), row),
            pl.BlockSpec((tm, D_ATTN), row),
            pl.BlockSpec((tm, D_ATTN), row),
            pl.BlockSpec((tm, D_ATTN), row),
        ],
        out_shape=[
            jax.ShapeDtypeStruct((tok, D_SSM), F32),
            jax.ShapeDtypeStruct((tok, D_ATTN), BF16),
            jax.ShapeDtypeStruct((tok, D_ATTN), BF16),
            jax.ShapeDtypeStruct((tok, D_ATTN), BF16),
        ],
        compiler_params=_params(1),
        name="in_projection",
    )(x2, g, scale, shift, w)


def _s5_prep_kernel(lre_r, lim_r, ldt_r, lre_c, lim_c, ldt_c, btr, bti, cre, cim, ctr, cti,
                    kt_ref, sre_ref, sim_ref, ore_ref, oim_ref, at_ref):
    def discretise(lre, lim, ldt):
        dt = jnp.exp(ldt)
        mag = jnp.exp(lre * dt)
        return mag * jnp.cos(lim * dt), mag * jnp.sin(lim * dt)

    lre, lim = lre_r[...], lim_r[...]
    ar, ai = discretise(lre, lim, ldt_r[...])
    den = lre * lre + lim * lim
    nr = ar - 1.0
    coef_re = (nr * lre + ai * lim) / den
    coef_im = (ai * lre - nr * lim) / den
    br, bi = btr[...], bti[...]
    zr = coef_re * br - coef_im * bi
    zi = coef_re * bi + coef_im * br
    c_re, c_im = cre[...], cim[...]
    for k in range(S5_T):
        sre_ref[S5_T - 1 - k] = zr
        sim_ref[S5_T - 1 - k] = zi
        for h in range(SSM_GROUP):
            kt_ref[k, h] = jnp.sum(c_re[:, h:h + 1, :] * zr - c_im[:, h:h + 1, :] * zi, axis=-1)
        zr, zi = ar * zr - ai * zi, ar * zi + ai * zr

    acr, aci = discretise(lre_c[...], lim_c[...], ldt_c[...])
    pr, pi = acr, aci
    ct_re, ct_im = ctr[...], cti[...]
    for t in range(S5_T):
        ore_ref[t] = ct_re * pr - ct_im * pi
        oim_ref[t] = -ct_re * pi - ct_im * pr
        pr, pi = acr * pr - aci * pi, acr * pi + aci * pr

    qr, qi = ar, ai
    for _ in range(S5_T - 1):
        qr, qi = ar * qr - ai * qi, ar * qi + ai * qr
    at_ref[0] = qr
    at_ref[1] = qi


def _s5_prepare(a_re, a_im, log_dt, b_re, b_im, c_re, c_im):
    g, p, h, t = SSM_GROUPS, SSM_STATE, SSM_GROUP, S5_T
    ins = [
        a_re.reshape(g, 1, p), a_im.reshape(g, 1, p), log_dt.reshape(g, 1, 1),
        a_re.reshape(g, p, 1), a_im.reshape(g, p, 1), log_dt.reshape(g, 1, 1),
        b_re.transpose(0, 2, 1), b_im.transpose(0, 2, 1), c_re, c_im,
        c_re.transpose(0, 2, 1), c_im.transpose(0, 2, 1),
    ]
    full = lambda s: pl.BlockSpec(s, lambda: (0,) * len(s))
    out_shapes = [(t, h, g, h), (t, g, h, p), (t, g, h, p), (t, g, p, h), (t, g, p, h), (2, g, 1, p)]
    kt, sre, sim, ore, oim, at = pl.pallas_call(
        _s5_prep_kernel,
        in_specs=[full(a.shape) for a in ins],
        out_specs=[full(s) for s in out_shapes],
        out_shape=[jax.ShapeDtypeStruct(s, F32) for s in out_shapes],
        name="s5_prepare",
    )(*ins)

    gh = GROUPS_PER_HALF
    eye = jnp.eye(gh, dtype=F32)
    ktg = kt.transpose(0, 2, 3, 1).reshape(t, 2, gh, h, h)
    toe = jnp.einsum("kfgab,gc->kfgacb", ktg, eye).reshape(t, 2, HALF, HALF)

    def s_tiles(s):
        sg = s.reshape(t, 2, gh, h, p)
        return jnp.einsum("sfgap,gc->sfgacp", sg, eye).reshape(t, 2, HALF, gh * p)

    def o_tiles(o):
        og = o.reshape(t, 2, gh, p, h)
        return jnp.einsum("tfgpb,gc->tfgpcb", og, eye).reshape(t, 2, gh * p, HALF)

    smat = jnp.concatenate([s_tiles(sre), s_tiles(sim)], axis=-1)
    omat = jnp.concatenate([o_tiles(ore), o_tiles(oim)], axis=-2)
    atr = at[0].reshape(2, gh * p)
    ati = at[1].reshape(2, gh * p)
    return toe.astype(BF16), smat.astype(BF16), omat.astype(BF16), atr, ati


def _s5_kernel(u_ref, toe_ref, smat_ref, omat_ref, atr_ref, ati_ref, d_ref, gw_ref, gb_ref,
               y_ref, acc_sc, s_sc, xp_sc, cr_sc, ci_sc, *, bsz):
    rows = bsz * S5_CB
    width = S5_T * D_SSM
    hs = GROUPS_PER_HALF * SSM_STATE

    @pl.when(pl.program_id(0) == 0)
    def _():
        cr_sc[...] = jnp.zeros_like(cr_sc)
        ci_sc[...] = jnp.zeros_like(ci_sc)

    u = u_ref[...].reshape(rows, width)
    ub = u.astype(BF16)

    def col(s, f):
        return slice(s * D_SSM + f * HALF, s * D_SSM + (f + 1) * HALF)

    lt = 2 * hs // 128
    for f in range(2):
        st = None
        for s in range(S5_T):
            part = jnp.dot(ub[:, col(s, f)], smat_ref[s, f], preferred_element_type=F32)
            st = part if st is None else st + part
        for l in range(lt):
            s_sc[f * lt + l] = st[:, l * 128:(l + 1) * 128]

    for f in range(2):
        for t in range(S5_T):
            acc = None
            for s in range(t + 1):
                part = jnp.dot(ub[:, col(s, f)], toe_ref[t - s, f], preferred_element_type=F32)
                acc = part if acc is None else acc + part
            acc_sc[:, col(t, f)] = acc

    half_lt = lt // 2
    for f in range(2):
        for l in range(half_lt):
            cols = slice(f * hs + l * 128, f * hs + (l + 1) * 128)
            a_r = atr_ref[f:f + 1, l * 128:(l + 1) * 128]
            a_i = ati_ref[f:f + 1, l * 128:(l + 1) * 128]
            xr = cr_sc[:, cols]
            xi = ci_sc[:, cols]
            t_re = f * lt + l
            t_im = f * lt + half_lt + l
            for c in range(S5_CB):
                idx = pl.ds(c, bsz, stride=S5_CB)
                xp_sc[t_re, idx, :] = xr
                xp_sc[t_im, idx, :] = xi
                sr = s_sc[t_re, idx, :]
                si = s_sc[t_im, idx, :]
                xr, xi = a_r * xr - a_i * xi + sr, a_r * xi + a_i * xr + si
            cr_sc[:, cols] = xr
            ci_sc[:, cols] = xi

    xpb = [jnp.concatenate([xp_sc[f * lt + l] for l in range(lt)], axis=-1).astype(BF16) for f in range(2)]
    gw = gw_ref[...]
    for t in range(S5_T):
        ys = []
        for f in range(2):
            carry = jnp.dot(xpb[f], omat_ref[t, f], preferred_element_type=F32)
            cs = col(t, f)
            ys.append(acc_sc[:, cs] + carry + d_ref[:, f * HALF:(f + 1) * HALF] * u[:, cs])
        y = _gelu_tanh(jnp.concatenate(ys, axis=-1))
        z = jnp.dot(y.astype(BF16), gw, preferred_element_type=F32) + gb_ref[...]
        out = (y * _sigmoid(z)).astype(BF16)
        y_ref[:, :, t * D_SSM:(t + 1) * D_SSM] = out.reshape(bsz, S5_CB, D_SSM)


def _s5_mixer(u, ops, d_skip, glu_w, glu_b):
    bsz, seq_len, _ = u.shape
    toe, smat, omat, atr, ati = ops
    width = S5_T * D_SSM
    nchunk = seq_len // S5_T
    rows = bsz * S5_CB
    state_w = 2 * 2 * GROUPS_PER_HALF * SSM_STATE
    u3 = u.reshape(bsz, nchunk, width)
    blk = pl.BlockSpec((bsz, S5_CB, width), lambda i: (0, i, 0))
    y = pl.pallas_call(
        functools.partial(_s5_kernel, bsz=bsz),
        grid=(nchunk // S5_CB,),
        in_specs=[
            blk,
            _const_spec(toe.shape), _const_spec(smat.shape), _const_spec(omat.shape),
            _const_spec(atr.shape), _const_spec(ati.shape),
            _const_spec((1, D_SSM)), _const_spec((D_SSM, D_SSM)), _const_spec((1, D_SSM)),
        ],
        out_specs=blk,
        out_shape=jax.ShapeDtypeStruct((bsz, nchunk, width), BF16),
        scratch_shapes=[
            pltpu.VMEM((rows, width), F32),
            pltpu.VMEM((state_w // 128, rows, 128), F32),
            pltpu.VMEM((state_w // 128, rows, 128), F32),
            pltpu.VMEM((bsz, state_w // 2), F32),
            pltpu.VMEM((bsz, state_w // 2), F32),
        ],
        compiler_params=_params(1),
        name="s5_mixer",
    )(u3, toe, smat, omat, atr, ati, d_skip.reshape(1, D_SSM), glu_w.astype(BF16), glu_b.reshape(1, D_SSM))
    return y.reshape(bsz, seq_len, D_SSM)


def _rel_bucket_np(dist):
    n = np.maximum(dist, 0)
    max_exact = REL_BUCKETS // 2
    nf = np.maximum(n, 1).astype(np.float64)
    large = max_exact + (np.log(nf / max_exact) / math.log(REL_MAX_DIST / max_exact)
                         * (REL_BUCKETS - max_exact)).astype(np.int64)
    large = np.minimum(large, REL_BUCKETS - 1)
    return np.where(n < max_exact, n, large).astype(np.int32)


def _bias_bucket_tiles():
    ko = np.arange(MOBA_BLOCK)[:, None]
    qo = np.arange(MOBA_BLOCK)[None, :]
    own = np.where(qo >= ko, _rel_bucket_np(qo - ko), -1)
    prev = _rel_bucket_np(qo - ko + MOBA_BLOCK)
    return np.stack([own, prev]).astype(np.int32)


assert int(_rel_bucket_np(np.arange(MOBA_BLOCK + 1, 8 * MOBA_BLOCK)).min()) == REL_BUCKETS - 1


def _bias_kernel(tab_ref, idx_ref, o_ref):
    h = pl.program_id(0)
    for t in range(2):
        idx = idx_ref[t]
        acc = jnp.full(idx.shape, NEG, F32)
        for b in range(REL_BUCKETS):
            acc = jnp.where(idx == b, tab_ref[h, b] * LOG2E, acc)
        o_ref[0, t] = acc


def _bias_tiles(rel_bias):
    idx = jnp.asarray(_bias_bucket_tiles())
    blk = MOBA_BLOCK
    return pl.pallas_call(
        _bias_kernel,
        grid=(N_HEADS,),
        in_specs=[
            pl.BlockSpec(memory_space=pltpu.SMEM),
            pl.BlockSpec((2, blk, blk), lambda h: (0, 0, 0)),
        ],
        out_specs=pl.BlockSpec((1, 2, blk, blk), lambda h: (h, 0, 0, 0)),
        out_shape=jax.ShapeDtypeStruct((N_HEADS, 2, blk, blk), F32),
        compiler_params=_params(1),
        name="moba_bias_tiles",
    )(rel_bias.T, idx)


def _attn_kernel(far_ref, q_ref, k_ref, v_ref, bias_ref, o_ref,
                 kmean_sc, vt_sc, mfar_sc, msel_sc, s0_sc, s1_sc, gmax0_sc, gmax1_sc, m_sc, l_sc, acc_sc, *, nb):
    s_bufs = (s0_sc, s1_sc)
    gmax_bufs = (gmax0_sc, gmax1_sc)
    hp = pl.program_id(1)
    qt = pl.program_id(2)
    blk = MOBA_BLOCK
    lane = lax.broadcasted_iota(jnp.int32, (1, 2 * HEAD_DIM), 1)
    head_mask = [lane < HEAD_DIM, lane >= HEAD_DIM]
    nt = (((1,), (1,)), ((), ()))

    @pl.when(qt == 0)
    def _():
        for j in range(nb):
            kb = k_ref[0, j * blk:(j + 1) * blk, :].astype(F32)
            km = jnp.mean(kb, axis=0, keepdims=True)
            for hh in range(2):
                kmean_sc[hh, j:j + 1, :] = jnp.where(head_mask[hh], km, 0.0)
            vt_sc[j] = v_ref[0, j * blk:(j + 1) * blk, :].astype(F32).T.astype(BF16)

    q2 = q_ref[0]
    jidx = lax.broadcasted_iota(jnp.int32, (nb, blk), 0)
    qm = []
    for hh in range(2):
        qm.append(jnp.where(head_mask[hh], q2, jnp.zeros_like(q2)))
        gate = lax.dot_general(kmean_sc[hh].astype(BF16), q2, nt, preferred_element_type=F32)
        rank = jnp.zeros((nb, blk), F32)
        for jp in range(nb):
            row = gate[jp:jp + 1, :]
            beats = (row > gate) | ((row == gate) & (jidx > jp))
            rank = rank + jnp.where(beats & (qt > jp), 1.0, 0.0)
        sel = (rank < float(MOBA_TOPK)) & (jidx < qt)
        mfar_sc[hh] = jnp.where(sel, far_ref[2 * hp + hh] * LOG2E, NEG)
        msel_sc[hh] = jnp.where(sel, 0.0, NEG)

    def scores(hh, j):
        kb = k_ref[0, pl.ds(pl.multiple_of(j * blk, blk), blk), :]
        return lax.dot_general(kb, qm[hh], nt, preferred_element_type=F32)

    def pv(hh, j, p):
        vt = vt_sc[j, hh * HEAD_DIM:(hh + 1) * HEAD_DIM, :]
        return jnp.dot(vt, p.astype(BF16), preferred_element_type=F32)

    def mask_row(ref, hh, j):
        return jnp.where(j >= 0, ref[hh, pl.ds(jnp.maximum(j, 0), 1), :], NEG)

    def block_ids(g):
        js = [qt - ATTN_GROUP * g - jj for jj in range(ATTN_GROUP)]
        return js, [jnp.maximum(j, 0) for j in js]

    def row_term(g, jj, hh, j):
        far = mask_row(mfar_sc, hh, j)
        if jj == 0:
            return jnp.where(g == 0, 0.0, far)
        if jj == 1:
            return jnp.where(g == 0, mask_row(msel_sc, hh, j), far)
        return far

    def phase_a(g, buf, first):
        js, jc = block_ids(g)
        s_buf, gmax_buf = s_bufs[buf], gmax_bufs[buf]
        for hh in range(2):
            gmax = None
            for jj in range(ATTN_GROUP):
                s = scores(hh, jc[jj])
                if first and jj < 2:
                    s = s + bias_ref[hh, jj]
                s_buf[hh, jj] = s
                cm = jnp.max(s, axis=0, keepdims=True) + row_term(g, jj, hh, js[jj])
                gmax = cm if gmax is None else jnp.maximum(gmax, cm)
            gmax_buf[hh] = gmax

    def phase_b(g, buf):
        js, jc = block_ids(g)
        s_buf, gmax_buf = s_bufs[buf], gmax_bufs[buf]
        for hh in range(2):
            m_old = m_sc[hh]
            m_new = jnp.maximum(m_old, gmax_buf[hh])
            alpha = jnp.exp2(m_old - m_new)
            lsum = None
            acc = None
            for jj in range(ATTN_GROUP):
                p = jnp.exp2(s_buf[hh, jj] - (m_new - row_term(g, jj, hh, js[jj])))
                ps = jnp.sum(p, axis=0, keepdims=True)
                pa = pv(hh, jc[jj], p)
                lsum = ps if lsum is None else lsum + ps
                acc = pa if acc is None else acc + pa
            m_sc[hh] = m_new
            l_sc[hh] = alpha * l_sc[hh] + lsum
            acc_sc[hh] = alpha * acc_sc[hh] + acc

    phase_a(0, 0, True)
    for hh in range(2):
        m_sc[hh] = gmax0_sc[hh]
        l_sc[hh] = jnp.zeros((1, blk), F32)
        acc_sc[hh] = jnp.zeros((HEAD_DIM, blk), F32)

    n_groups = qt // ATTN_GROUP + 1

    def pipelined(g, carry):
        @pl.when(g % 2 == 0)
        def _():
            phase_a(g + 1, 1, False)
            phase_b(g, 0)

        @pl.when(g % 2 == 1)
        def _():
            phase_a(g + 1, 0, False)
            phase_b(g, 1)

        return carry

    lax.fori_loop(0, n_groups - 1, pipelined, 0)

    @pl.when(n_groups % 2 == 1)
    def _():
        phase_b(n_groups - 1, 0)

    @pl.when(n_groups % 2 == 0)
    def _():
        phase_b(n_groups - 1, 1)

    out = jnp.concatenate([acc_sc[hh] / l_sc[hh] for hh in range(2)], axis=0)
    o_ref[0] = out.T.astype(BF16)


def _moba_attention(q, k, v, bias_tiles, far_bias):
    bsz, seq_len, _ = q.shape
    blk = MOBA_BLOCK
    nb = seq_len // blk
    pair = 2 * HEAD_DIM
    return pl.pallas_call(
        functools.partial(_attn_kernel, nb=nb),
        grid=(bsz, N_HEADS // 2, nb),
        in_specs=[
            pl.BlockSpec(memory_space=pltpu.SMEM),
            pl.BlockSpec((1, blk, pair), lambda b, h, i: (b, i, h)),
            pl.BlockSpec((1, seq_len, pair), lambda b, h, i: (b, 0, h)),
            pl.BlockSpec((1, seq_len, pair), lambda b, h, i: (b, 0, h)),
            pl.BlockSpec((2, 2, blk, blk), lambda b, h, i: (h, 0, 0, 0)),
        ],
        out_specs=pl.BlockSpec((1, blk, pair), lambda b, h, i: (b, i, h)),
        out_shape=jax.ShapeDtypeStruct((bsz, seq_len, D_ATTN), BF16),
        scratch_shapes=[
            pltpu.VMEM((2, nb, pair), F32),
            pltpu.VMEM((nb, pair, blk), BF16),
            pltpu.VMEM((2, nb, blk), F32),
            pltpu.VMEM((2, nb, blk), F32),
            pltpu.VMEM((2, ATTN_GROUP, blk, blk), F32),
            pltpu.VMEM((2, ATTN_GROUP, blk, blk), F32),
            pltpu.VMEM((2, 1, blk), F32),
            pltpu.VMEM((2, 1, blk), F32),
            pltpu.VMEM((2, 1, blk), F32),
            pltpu.VMEM((2, 1, blk), F32),
            pltpu.VMEM((2, HEAD_DIM, blk), F32),
        ],
        compiler_params=_params(3),
        name="moba_attention",
    )(far_bias, q, k, v, bias_tiles)


def _ffn_body(x, g_ref, sc_ref, sh_ref, gate_ref, wg_ref, wu_ref, cw_ref, wd_ref, fin_ref, o_ref,
              gext_sc, carry_sc, acc_sc, h_sc, *, tm, per_seq, final_norm):
    @pl.when(pl.program_id(0) % per_seq == 0)
    def _():
        carry_sc[...] = jnp.zeros_like(carry_sc)

    h_sc[...] = _norm_mod(x, g_ref[...], sc_ref[0], sh_ref[0]).astype(BF16)
    acc_sc[...] = jnp.zeros_like(acc_sc)

    def chunk(j, carry):
        h = h_sc[...]
        gpre = jnp.dot(h, wg_ref[j], preferred_element_type=F32)
        up = jnp.dot(h, wu_ref[j], preferred_element_type=F32)
        gext_sc[0:FFN_HALO, :] = carry_sc[j]
        gext_sc[FFN_HALO:, :] = gpre
        carry_sc[j] = gpre[tm - FFN_HALO:, :]
        cw = cw_ref[j]
        conv = (cw[0:1] * gext_sc[FFN_HALO - 2:FFN_HALO - 2 + tm, :]
                + cw[1:2] * gext_sc[FFN_HALO - 1:FFN_HALO - 1 + tm, :]
                + cw[2:3] * gpre + cw[3:4])
        act = (_silu(conv) * up).astype(BF16)
        acc_sc[...] += jnp.dot(act, wd_ref[j], preferred_element_type=F32)
        return carry

    lax.fori_loop(0, N_FFN_CHUNKS, chunk, 0)
    out = x + gate_ref[0] * acc_sc[...]
    if final_norm:
        ms = jnp.mean(out * out, axis=-1, keepdims=True)
        out = out * lax.rsqrt(ms + EPS) * fin_ref[...]
    o_ref[...] = out


def _ffn0_kernel(x_ref, ys_ref, ya_ref, wo_ref, g1_ref, g_ref, sc_ref, sh_ref, gate_ref,
                 wg_ref, wu_ref, cw_ref, wd_ref, fin_ref, o_ref, gext_sc, carry_sc, acc_sc, h_sc, **kw):
    y = (jnp.dot(ys_ref[...], wo_ref[0:D_SSM, :], preferred_element_type=F32)
         + jnp.dot(ya_ref[...], wo_ref[D_SSM:, :], preferred_element_type=F32))
    x = x_ref[...] + g1_ref[0] * y
    _ffn_body(x, g_ref, sc_ref, sh_ref, gate_ref, wg_ref, wu_ref, cw_ref, wd_ref, fin_ref, o_ref,
              gext_sc, carry_sc, acc_sc, h_sc, **kw)


def _ffn1_kernel(x_ref, g_ref, sc_ref, sh_ref, gate_ref, wg_ref, wu_ref, cw_ref, wd_ref, fin_ref,
                 o_ref, gext_sc, carry_sc, acc_sc, h_sc, **kw):
    _ffn_body(x_ref[...], g_ref, sc_ref, sh_ref, gate_ref, wg_ref, wu_ref, cw_ref, wd_ref, fin_ref, o_ref,
              gext_sc, carry_sc, acc_sc, h_sc, **kw)


def _ffn_weights(w_up, w_gate, dw_w, dw_b, w_down):
    d = w_up.shape[0]
    split = lambda w: w.reshape(d, N_FFN_CHUNKS, FFN_CHUNK).transpose(1, 0, 2).astype(BF16)
    cw = jnp.concatenate([dw_w, dw_b[None, :]], axis=0)
    cw = cw.reshape(FFN_CONV_WIDTH + 1, N_FFN_CHUNKS, FFN_CHUNK).transpose(1, 0, 2)
    return split(w_gate), split(w_up), cw, w_down.reshape(N_FFN_CHUNKS, FFN_CHUNK, d).astype(BF16)


def _conv_ffn(x2, mixer, norm_g, scale, shift, gate, weights, final_g, seq_len, final_norm, tm=512):
    tok, d = x2.shape
    per_seq = seq_len // tm
    wg, wu, cw, wd = weights
    row = lambda i: (i, 0)
    bat = lambda i: (i // per_seq, 0, 0)
    vec = pl.BlockSpec((1, 1, d), bat)
    common_specs = [_const_spec((1, d)), vec, vec, vec,
                    _const_spec(wg.shape), _const_spec(wu.shape), _const_spec(cw.shape), _const_spec(wd.shape),
                    _const_spec((1, d))]
    common_args = [norm_g, scale, shift, gate, wg, wu, cw, wd, final_g]
    kw = dict(tm=tm, per_seq=per_seq, final_norm=final_norm)
    if mixer is None:
        body = functools.partial(_ffn1_kernel, **kw)
        specs = [pl.BlockSpec((tm, d), row)] + common_specs
        args = [x2] + common_args
    else:
        ys, ya, wo, g1 = mixer
        body = functools.partial(_ffn0_kernel, **kw)
        specs = [pl.BlockSpec((tm, d), row), pl.BlockSpec((tm, D_SSM), row), pl.BlockSpec((tm, D_ATTN), row),
                 _const_spec(wo.shape), vec] + common_specs
        args = [x2, ys, ya, wo, g1] + common_args
    return pl.pallas_call(
        body,
        grid=(tok // tm,),
        in_specs=specs,
        out_specs=pl.BlockSpec((tm, d), row),
        out_shape=jax.ShapeDtypeStruct((tok, d), F32),
        scratch_shapes=[
            pltpu.VMEM((tm + FFN_HALO, FFN_CHUNK), F32),
            pltpu.VMEM((N_FFN_CHUNKS, FFN_HALO, FFN_CHUNK), F32),
            pltpu.VMEM((tm, d), F32),
            pltpu.VMEM((tm, d), BF16),
        ],
        compiler_params=_params(1),
        name="conv_ffn_final" if final_norm else "conv_ffn",
    )(*args)


def _conformer_kernel(x_ref, g_ref, sc_ref, sh_ref, gate_ref, win_ref, bin_ref, dw_ref, dwb_ref,
                      lng_ref, lnb_ref, wout_ref, bout_ref, o_ref, aext_sc, conv_sc, *, tm, per_seq):
    x = x_ref[...]
    d = x.shape[-1]
    h = _norm_mod(x, g_ref[...], sc_ref[0], sh_ref[0]).astype(BF16)
    a = jnp.dot(h, win_ref[...], preferred_element_type=F32) + bin_ref[...]
    a = a[:, :d] * _sigmoid(a[:, d:])

    @pl.when(pl.program_id(0) % per_seq == 0)
    def _():
        aext_sc[0:CONV_HALO, :] = jnp.zeros((CONV_HALO, d), F32)

    aext_sc[CONV_HALO:, :] = a

    rc = 32
    off = CONV_HALO - (CONV_WIDTH - 1)

    for base in range(0, tm, rc):
        acc = jnp.zeros((rc, d), F32) + dwb_ref[...]
        for k in range(CONV_WIDTH):
            acc = acc + dw_ref[k:k + 1, :] * aext_sc[base + off + k:base + off + k + rc, :]
        conv_sc[base:base + rc, :] = acc
    aext_sc[0:CONV_HALO, :] = aext_sc[tm:tm + CONV_HALO, :]

    c = conv_sc[...]
    mu = jnp.mean(c, axis=-1, keepdims=True)
    xc = c - mu
    y = xc * lax.rsqrt(jnp.mean(xc * xc, axis=-1, keepdims=True) + EPS)
    y = _silu(y * lng_ref[...] + lnb_ref[...]).astype(BF16)
    out = jnp.dot(y, wout_ref[...], preferred_element_type=F32) + bout_ref[...]
    o_ref[...] = x + gate_ref[0] * out


def _conformer(x2, norm_g, scale, shift, gate, w_in, b_in, dw_w, dw_b, ln_g, ln_b, w_out, b_out, seq_len, tm=256):
    tok, d = x2.shape
    per_seq = seq_len // tm
    row = lambda i: (i, 0)
    bat = lambda i: (i // per_seq, 0, 0)
    vec = pl.BlockSpec((1, 1, d), bat)
    return pl.pallas_call(
        functools.partial(_conformer_kernel, tm=tm, per_seq=per_seq),
        grid=(tok // tm,),
        in_specs=[pl.BlockSpec((tm, d), row), _const_spec((1, d)), vec, vec, vec,
                  _const_spec((d, 2 * d)), _const_spec((1, 2 * d)), _const_spec((CONV_WIDTH, d)), _const_spec((1, d)),
                  _const_spec((1, d)), _const_spec((1, d)), _const_spec((d, d)), _const_spec((1, d))],
        out_specs=pl.BlockSpec((tm, d), row),
        out_shape=jax.ShapeDtypeStruct((tok, d), F32),
        scratch_shapes=[pltpu.VMEM((tm + CONV_HALO, d), F32), pltpu.VMEM((tm, d), F32)],
        compiler_params=_params(1),
        name="conformer_conv",
    )(x2, norm_g, scale, shift, gate, w_in.astype(BF16), b_in.reshape(1, -1), dw_w, dw_b.reshape(1, -1),
      ln_g.reshape(1, -1), ln_b.reshape(1, -1), w_out.astype(BF16), b_out.reshape(1, -1))


def kernel(x, c, mod_w, mod_b, norm_g, final_g, ab_w_in, ssm_a_re, ssm_a_im, ssm_log_dt, ssm_b_re, ssm_b_im, ssm_c_re, ssm_c_im, ssm_d, ssm_glu_w, ssm_glu_b, ab_w_out, rel_bias, cm_w_in, cm_b_in, cm_dw_w, cm_dw_b, cm_ln_g, cm_ln_b, cm_w_out, cm_b_out, ffn_w_up, ffn_w_gate, ffn_dw_w, ffn_dw_b, ffn_w_down):
    bsz, seq_len, d = x.shape
    tok = bsz * seq_len
    x2 = x.reshape(tok, d)
    mod = _modulation(c, mod_w, mod_b)
    vecs = [[mod[l, :, i * d:(i + 1) * d].reshape(bsz, 1, d) for i in range(6)] for l in range(2)]
    fin = final_g.reshape(1, d)

    sh1, sc1, g1, sh2, sc2, g2 = vecs[0]
    u, q, k, v = _in_projection(x2, norm_g[0, 0].reshape(1, d), sc1, sh1, ab_w_in[0].astype(BF16), seq_len)
    ops = _s5_prepare(ssm_a_re[0], ssm_a_im[0], ssm_log_dt[0], ssm_b_re[0], ssm_b_im[0], ssm_c_re[0], ssm_c_im[0])
    y_ssm = _s5_mixer(u.reshape(bsz, seq_len, D_SSM), ops, ssm_d[0], ssm_glu_w[0], ssm_glu_b[0])
    att = lambda a: a.reshape(bsz, seq_len, D_ATTN)
    y_att = _moba_attention(att(q), att(k), att(v), _bias_tiles(rel_bias), rel_bias[REL_BUCKETS - 1])
    w0 = _ffn_weights(ffn_w_up[0], ffn_w_gate[0], ffn_dw_w[0], ffn_dw_b[0], ffn_w_down[0])
    x2 = _conv_ffn(x2, (y_ssm.reshape(tok, D_SSM), y_att.reshape(tok, D_ATTN), ab_w_out[0].astype(BF16), g1),
                   norm_g[0, 1].reshape(1, d), sc2, sh2, g2, w0, fin, seq_len, final_norm=False)

    sh1, sc1, g1, sh2, sc2, g2 = vecs[1]
    x2 = _conformer(x2, norm_g[1, 0].reshape(1, d), sc1, sh1, g1, cm_w_in[0], cm_b_in[0], cm_dw_w[0], cm_dw_b[0],
                    cm_ln_g[0], cm_ln_b[0], cm_w_out[0], cm_b_out[0], seq_len)
    w1 = _ffn_weights(ffn_w_up[1], ffn_w_gate[1], ffn_dw_w[1], ffn_dw_b[1], ffn_w_down[1])
    x2 = _conv_ffn(x2, None, norm_g[1, 1].reshape(1, d), sc2, sh2, g2, w1, fin, seq_len, final_norm=True)
    return x2.reshape(bsz, seq_len, d)
```

```python
import functools
import math

import numpy as np
import jax
import jax.numpy as jnp
from jax import lax
from jax.experimental import pallas as pl
from jax.experimental.pallas import tpu as pltpu

F32 = jnp.float32
BF16 = jnp.bfloat16

D_MODEL = 1024
D_SSM = 512
SSM_GROUP = 16
SSM_GROUPS = 32
SSM_STATE = 64
D_ATTN = 512
HEAD_DIM = 64
N_HEADS = 8
MOBA_BLOCK = 256
MOBA_TOPK = 3
REL_BUCKETS = 32
REL_MAX_DIST = 128
CONV_WIDTH = 31
FFN_HIDDEN = 2816
FFN_CONV_WIDTH = 3
EPS = 1e-6

NEG = -1e30
LOG2E = math.log2(math.e)

V7X_VMEM_BYTES = 64 * 1024 * 1024
VMEM_LIMIT = V7X_VMEM_BYTES - 8 * 1024 * 1024

S5_T = 4
S5_CB = 32
S5_ROW_STRIDE = S5_CB + 8
HALF = 256
GROUPS_PER_HALF = HALF // SSM_GROUP
FFN_CHUNK = 256
N_FFN_CHUNKS = FFN_HIDDEN // FFN_CHUNK
CONV_HALO = 32
FFN_HALO = 8
ATTN_GROUP = 4


def _sigmoid(x):
    return 0.5 * jnp.tanh(0.5 * x) + 0.5


def _silu(x):
    return x * _sigmoid(x)


def _gelu_tanh(x):
    c = math.sqrt(2.0 / math.pi)
    return 0.5 * x * (1.0 + jnp.tanh(c * (x + 0.044715 * (x * x * x))))


def _norm_mod(x, g, scale, shift):
    ms = jnp.mean(x * x, axis=-1, keepdims=True)
    y = x * lax.rsqrt(ms + EPS) * g
    return y * (1.0 + scale) + shift


def _params(n_axes, vmem=VMEM_LIMIT):
    return pltpu.CompilerParams(dimension_semantics=("arbitrary",) * n_axes, vmem_limit_bytes=vmem)


def _const_spec(shape):
    nd = len(shape)
    return pl.BlockSpec(shape, lambda *_: (0,) * nd, pipeline_mode=pl.Buffered(1))


def _mod_kernel(c_ref, w_ref, b_ref, o_ref):
    c = c_ref[...]
    cs = _silu(c).astype(BF16)
    o_ref[0] = jnp.dot(cs, w_ref[0].astype(BF16), preferred_element_type=F32) + b_ref[0]


def _modulation(c, mod_w, mod_b):
    depth, d, n = mod_w.shape
    bsz = c.shape[0]
    nt = 1536
    return pl.pallas_call(
        _mod_kernel,
        grid=(depth, n // nt),
        in_specs=[
            pl.BlockSpec((bsz, d), lambda l, j: (0, 0)),
            pl.BlockSpec((1, d, nt), lambda l, j: (l, 0, j)),
            pl.BlockSpec((1, 1, nt), lambda l, j: (l, 0, j)),
        ],
        out_specs=pl.BlockSpec((1, bsz, nt), lambda l, j: (l, 0, j)),
        out_shape=jax.ShapeDtypeStruct((depth, bsz, n), F32),
        compiler_params=_params(2),
        name="modulation",
    )(c, mod_w, mod_b.reshape(depth, 1, n))


def _inproj_kernel(x_ref, g_ref, sc_ref, sh_ref, w_ref, u_ref, q_ref, k_ref, v_ref, u_sc):
    h = _norm_mod(x_ref[...], g_ref[...], sc_ref[0], sh_ref[0]).astype(BF16)
    p = jnp.dot(h, w_ref[...], preferred_element_type=F32)
    tm = p.shape[0]
    lanes = D_SSM // 128
    for l in range(lanes):
        u_sc[l] = p[:, l * 128:(l + 1) * 128]
    for s in range(S5_T):
        u_ref[s] = jnp.concatenate([u_sc[l, pl.ds(s, tm // S5_T, stride=S5_T), :] for l in range(lanes)], axis=-1)
    q_ref[...] = (p[:, D_SSM:D_SSM + D_ATTN] * (HEAD_DIM ** -0.5 * LOG2E)).astype(BF16)
    k_ref[...] = p[:, D_SSM + D_ATTN:D_SSM + 2 * D_ATTN].astype(BF16)
    v_ref[...] = p[:, D_SSM + 2 * D_ATTN:].astype(BF16)


def _in_projection(x2, g, scale, shift, w, seq_len, tm=512):
    tok, d = x2.shape
    per_seq = seq_len // tm
    n = w.shape[1]
    row = lambda i: (i, 0)
    bat = lambda i: (i // per_seq, 0, 0)
    return pl.pallas_call(
        _inproj_kernel,
        grid=(tok // tm,),
        in_specs=[
            pl.BlockSpec((tm, d), row),
            _const_spec((1, d)),
            pl.BlockSpec((1, 1, d), bat),
            pl.BlockSpec((1, 1, d), bat),
            _const_spec((d, n)),
        ],
        out_specs=[
            pl.BlockSpec((S5_T, tm // S5_T, D_SSM), lambda i: (0, i, 0)),
            pl.BlockSpec((tm, D_ATTN), row),
            pl.BlockSpec((tm, D_ATTN), row),
            pl.BlockSpec((tm, D_ATTN), row),
        ],
        out_shape=[
            jax.ShapeDtypeStruct((S5_T, tok // S5_T, D_SSM), F32),
            jax.ShapeDtypeStruct((tok, D_ATTN), BF16),
            jax.ShapeDtypeStruct((tok, D_ATTN), BF16),
            jax.ShapeDtypeStruct((tok, D_ATTN), BF16),
        ],
        scratch_shapes=[pltpu.VMEM((D_SSM // 128, tm, 128), F32)],
        compiler_params=_params(1),
        name="in_projection",
    )(x2, g, scale, shift, w)


def _s5_prep_kernel(lre_r, lim_r, ldt_r, lre_c, lim_c, ldt_c, btr, bti, cre, cim, ctr, cti,
                    kt_ref, sre_ref, sim_ref, ore_ref, oim_ref, at_ref):
    def discretise(lre, lim, ldt):
        dt = jnp.exp(ldt)
        mag = jnp.exp(lre * dt)
        return mag * jnp.cos(lim * dt), mag * jnp.sin(lim * dt)

    lre, lim = lre_r[...], lim_r[...]
    ar, ai = discretise(lre, lim, ldt_r[...])
    den = lre * lre + lim * lim
    nr = ar - 1.0
    coef_re = (nr * lre + ai * lim) / den
    coef_im = (ai * lre - nr * lim) / den
    br, bi = btr[...], bti[...]
    zr = coef_re * br - coef_im * bi
    zi = coef_re * bi + coef_im * br
    c_re, c_im = cre[...], cim[...]
    for k in range(S5_T):
        sre_ref[S5_T - 1 - k] = zr
        sim_ref[S5_T - 1 - k] = zi
        for h in range(SSM_GROUP):
            kt_ref[k, h] = jnp.sum(c_re[:, h:h + 1, :] * zr - c_im[:, h:h + 1, :] * zi, axis=-1)
        zr, zi = ar * zr - ai * zi, ar * zi + ai * zr

    acr, aci = discretise(lre_c[...], lim_c[...], ldt_c[...])
    pr, pi = acr, aci
    ct_re, ct_im = ctr[...], cti[...]
    for t in range(S5_T):
        ore_ref[t] = ct_re * pr - ct_im * pi
        oim_ref[t] = -ct_re * pi - ct_im * pr
        pr, pi = acr * pr - aci * pi, acr * pi + aci * pr

    qr, qi = ar, ai
    for _ in range(S5_T - 1):
        qr, qi = ar * qr - ai * qi, ar * qi + ai * qr
    at_ref[0] = qr
    at_ref[1] = qi


def _s5_prepare(a_re, a_im, log_dt, b_re, b_im, c_re, c_im):
    g, p, h, t = SSM_GROUPS, SSM_STATE, SSM_GROUP, S5_T
    ins = [
        a_re.reshape(g, 1, p), a_im.reshape(g, 1, p), log_dt.reshape(g, 1, 1),
        a_re.reshape(g, p, 1), a_im.reshape(g, p, 1), log_dt.reshape(g, 1, 1),
        b_re.transpose(0, 2, 1), b_im.transpose(0, 2, 1), c_re, c_im,
        c_re.transpose(0, 2, 1), c_im.transpose(0, 2, 1),
    ]
    full = lambda s: pl.BlockSpec(s, lambda: (0,) * len(s))
    out_shapes = [(t, h, g, h), (t, g, h, p), (t, g, h, p), (t, g, p, h), (t, g, p, h), (2, g, 1, p)]
    kt, sre, sim, ore, oim, at = pl.pallas_call(
        _s5_prep_kernel,
        in_specs=[full(a.shape) for a in ins],
        out_specs=[full(s) for s in out_shapes],
        out_shape=[jax.ShapeDtypeStruct(s, F32) for s in out_shapes],
        name="s5_prepare",
    )(*ins)

    gh = GROUPS_PER_HALF
    eye = jnp.eye(gh, dtype=F32)
    ktg = kt.transpose(0, 2, 3, 1).reshape(t, 2, gh, h, h)
    toe = jnp.einsum("kfgab,gc->kfgacb", ktg, eye).reshape(t, 2, HALF, HALF)

    def s_tiles(s):
        sg = s.reshape(t, 2, gh, h, p)
        return jnp.einsum("sfgap,gc->sfgacp", sg, eye).reshape(t, 2, HALF, gh * p)

    def o_tiles(o):
        og = o.reshape(t, 2, gh, p, h)
        return jnp.einsum("tfgpb,gc->tfgpcb", og, eye).reshape(t, 2, gh * p, HALF)

    smat = jnp.concatenate([s_tiles(sre), s_tiles(sim)], axis=-1)
    omat = jnp.concatenate([o_tiles(ore), o_tiles(oim)], axis=-2)
    atr = at[0].reshape(2, gh * p)
    ati = at[1].reshape(2, gh * p)
    return toe.astype(BF16), smat.astype(BF16), omat.astype(BF16), atr, ati


def _s5_kernel(u_ref, toe_ref, smat_ref, omat_ref, atr_ref, ati_ref, d_ref, gw_ref, gb_ref,
               y_ref, acc_sc, s_sc, xp_sc, cr_sc, ci_sc, y_sc, *, bsz):
    rows = bsz * S5_CB
    hs = GROUPS_PER_HALF * SSM_STATE

    @pl.when(pl.program_id(0) == 0)
    def _():
        cr_sc[...] = jnp.zeros_like(cr_sc)
        ci_sc[...] = jnp.zeros_like(ci_sc)

    u = [u_ref[s].reshape(rows, D_SSM) for s in range(S5_T)]
    ub = [v.astype(BF16) for v in u]

    def half(f):
        return slice(f * HALF, (f + 1) * HALF)

    def col(t, f):
        return slice(t * D_SSM + f * HALF, t * D_SSM + (f + 1) * HALF)

    lt = 2 * hs // 128
    for f in range(2):
        st = None
        for s in range(S5_T):
            part = jnp.dot(ub[s][:, half(f)], smat_ref[s, f], preferred_element_type=F32)
            st = part if st is None else st + part
        for l in range(lt):
            for b in range(bsz):
                s_sc[f * lt + l, b * S5_ROW_STRIDE:b * S5_ROW_STRIDE + S5_CB, :] = (
                    st[b * S5_CB:(b + 1) * S5_CB, l * 128:(l + 1) * 128])

    for f in range(2):
        for t in range(S5_T):
            acc = None
            for s in range(t + 1):
                part = jnp.dot(ub[s][:, half(f)], toe_ref[t - s, f], preferred_element_type=F32)
                acc = part if acc is None else acc + part
            acc_sc[:, col(t, f)] = acc

    half_lt = lt // 2
    for f in range(2):
        for l in range(half_lt):
            cols = slice(f * hs + l * 128, f * hs + (l + 1) * 128)
            a_r = atr_ref[f:f + 1, l * 128:(l + 1) * 128]
            a_i = ati_ref[f:f + 1, l * 128:(l + 1) * 128]
            xr = cr_sc[:, cols]
            xi = ci_sc[:, cols]
            t_re = f * lt + l
            t_im = f * lt + half_lt + l
            for c in range(S5_CB):
                idx = pl.ds(c, bsz, stride=S5_ROW_STRIDE)
                xp_sc[t_re, idx, :] = xr
                xp_sc[t_im, idx, :] = xi
                sr = s_sc[t_re, idx, :]
                si = s_sc[t_im, idx, :]
                xr, xi = a_r * xr - a_i * xi + sr, a_r * xi + a_i * xr + si
            cr_sc[:, cols] = xr
            ci_sc[:, cols] = xi

    def xp_tile(i):
        return jnp.concatenate([xp_sc[i, b * S5_ROW_STRIDE:b * S5_ROW_STRIDE + S5_CB, :] for b in range(bsz)], axis=0)

    xpb = [jnp.concatenate([xp_tile(f * lt + l) for l in range(lt)], axis=-1).astype(BF16) for f in range(2)]
    gw = gw_ref[...]
    for t in range(S5_T):
        ys = []
        for f in range(2):
            carry = jnp.dot(xpb[f], omat_ref[t, f], preferred_element_type=F32)
            ys.append(acc_sc[:, col(t, f)] + carry + d_ref[:, half(f)] * u[t][:, half(f)])
        y = _gelu_tanh(jnp.concatenate(ys, axis=-1))
        z = jnp.dot(y.astype(BF16), gw, preferred_element_type=F32) + gb_ref[...]
        out = y * _sigmoid(z)
        for b in range(bsz):
            for l in range(D_SSM // 128):
                y_sc[l, pl.ds(b * S5_T * S5_CB + t, S5_CB, stride=S5_T), :] = (
                    out[b * S5_CB:(b + 1) * S5_CB, l * 128:(l + 1) * 128])
    span = S5_T * S5_CB
    for b in range(bsz):
        y_ref[b] = jnp.concatenate([y_sc[l, b * span:(b + 1) * span, :] for l in range(D_SSM // 128)],
                                   axis=-1).astype(BF16)


def _s5_mixer(u, ops, d_skip, glu_w, glu_b):
    _, bsz, nchunk, _ = u.shape
    seq_len = nchunk * S5_T
    toe, smat, omat, atr, ati = ops
    width = S5_T * D_SSM
    rows = bsz * S5_CB
    state_w = 2 * 2 * GROUPS_PER_HALF * SSM_STATE
    return pl.pallas_call(
        functools.partial(_s5_kernel, bsz=bsz),
        grid=(nchunk // S5_CB,),
        in_specs=[
            pl.BlockSpec((S5_T, bsz, S5_CB, D_SSM), lambda i: (0, 0, i, 0)),
            _const_spec(toe.shape), _const_spec(smat.shape), _const_spec(omat.shape),
            _const_spec(atr.shape), _const_spec(ati.shape),
            _const_spec((1, D_SSM)), _const_spec((D_SSM, D_SSM)), _const_spec((1, D_SSM)),
        ],
        out_specs=pl.BlockSpec((bsz, S5_T * S5_CB, D_SSM), lambda i: (0, i, 0)),
        out_shape=jax.ShapeDtypeStruct((bsz, seq_len, D_SSM), BF16),
        scratch_shapes=[
            pltpu.VMEM((rows, width), F32),
            pltpu.VMEM((state_w // 128, bsz * S5_ROW_STRIDE, 128), F32),
            pltpu.VMEM((state_w // 128, bsz * S5_ROW_STRIDE, 128), F32),
            pltpu.VMEM((bsz, state_w // 2), F32),
            pltpu.VMEM((bsz, state_w // 2), F32),
            pltpu.VMEM((D_SSM // 128, rows * S5_T, 128), F32),
        ],
        compiler_params=_params(1),
        name="s5_mixer",
    )(u, toe, smat, omat, atr, ati, d_skip.reshape(1, D_SSM), glu_w.astype(BF16), glu_b.reshape(1, D_SSM))


def _rel_bucket_np(dist):
    n = np.maximum(dist, 0)
    max_exact = REL_BUCKETS // 2
    nf = np.maximum(n, 1).astype(np.float64)
    large = max_exact + (np.log(nf / max_exact) / math.log(REL_MAX_DIST / max_exact)
                         * (REL_BUCKETS - max_exact)).astype(np.int64)
    large = np.minimum(large, REL_BUCKETS - 1)
    return np.where(n < max_exact, n, large).astype(np.int32)


def _bias_bucket_tiles():
    ko = np.arange(MOBA_BLOCK)[:, None]
    qo = np.arange(MOBA_BLOCK)[None, :]
    own = np.where(qo >= ko, _rel_bucket_np(qo - ko), -1)
    prev = _rel_bucket_np(qo - ko + MOBA_BLOCK)
    return np.stack([own, prev]).astype(np.int32)


assert int(_rel_bucket_np(np.arange(MOBA_BLOCK + 1, 8 * MOBA_BLOCK)).min()) == REL_BUCKETS - 1


def _bias_kernel(tab_ref, idx_ref, o_ref):
    h = pl.program_id(0)
    for t in range(2):
        idx = idx_ref[t]
        acc = jnp.full(idx.shape, NEG, F32)
        for b in range(REL_BUCKETS):
            acc = jnp.where(idx == b, tab_ref[h, b] * LOG2E, acc)
        o_ref[0, t] = acc


def _bias_tiles(rel_bias):
    idx = jnp.asarray(_bias_bucket_tiles())
    blk = MOBA_BLOCK
    return pl.pallas_call(
        _bias_kernel,
        grid=(N_HEADS,),
        in_specs=[
            pl.BlockSpec(memory_space=pltpu.SMEM),
            pl.BlockSpec((2, blk, blk), lambda h: (0, 0, 0)),
        ],
        out_specs=pl.BlockSpec((1, 2, blk, blk), lambda h: (h, 0, 0, 0)),
        out_shape=jax.ShapeDtypeStruct((N_HEADS, 2, blk, blk), F32),
        compiler_params=_params(1),
        name="moba_bias_tiles",
    )(rel_bias.T, idx)


def _attn_kernel(far_ref, q_ref, k_ref, v_ref, bias_ref, o_ref,
                 kmean_sc, vt_sc, mfar_sc, msel_sc, s0_sc, s1_sc, gmax0_sc, gmax1_sc, m_sc, l_sc, acc_sc, *, nb):
    s_bufs = (s0_sc, s1_sc)
    gmax_bufs = (gmax0_sc, gmax1_sc)
    hp = pl.program_id(1)
    qt = pl.program_id(2)
    blk = MOBA_BLOCK
    lane = lax.broadcasted_iota(jnp.int32, (1, 2 * HEAD_DIM), 1)
    head_mask = [lane < HEAD_DIM, lane >= HEAD_DIM]
    nt = (((1,), (1,)), ((), ()))

    @pl.when(qt == 0)
    def _():
        for j in range(nb):
            kb = k_ref[0, j * blk:(j + 1) * blk, :].astype(F32)
            km = jnp.mean(kb, axis=0, keepdims=True)
            for hh in range(2):
                kmean_sc[hh, j:j + 1, :] = jnp.where(head_mask[hh], km, 0.0)
            vt_sc[j] = v_ref[0, j * blk:(j + 1) * blk, :].astype(F32).T.astype(BF16)

    q2 = q_ref[0]
    jidx = lax.broadcasted_iota(jnp.int32, (nb, blk), 0)
    qm = []
    for hh in range(2):
        qm.append(jnp.where(head_mask[hh], q2, jnp.zeros_like(q2)))
        gate = lax.dot_general(kmean_sc[hh].astype(BF16), q2, nt, preferred_element_type=F32)
        rank = jnp.zeros((nb, blk), F32)
        for jp in range(nb):
            row = gate[jp:jp + 1, :]
            beats = (row > gate) | ((row == gate) & (jidx > jp))
            rank = rank + jnp.where(beats & (qt > jp), 1.0, 0.0)
        sel = (rank < float(MOBA_TOPK)) & (jidx < qt)
        mfar_sc[hh] = jnp.where(sel, far_ref[2 * hp + hh] * LOG2E, NEG)
        msel_sc[hh] = jnp.where(sel, 0.0, NEG)

    def scores(hh, j):
        kb = k_ref[0, pl.ds(pl.multiple_of(j * blk, blk), blk), :]
        return lax.dot_general(kb, qm[hh], nt, preferred_element_type=F32)

    def pv(hh, j, p):
        vt = vt_sc[j, hh * HEAD_DIM:(hh + 1) * HEAD_DIM, :]
        return jnp.dot(vt, p.astype(BF16), preferred_element_type=F32)

    def mask_row(ref, hh, j):
        return jnp.where(j >= 0, ref[hh, pl.ds(jnp.maximum(j, 0), 1), :], NEG)

    def block_ids(g):
        js = [qt - ATTN_GROUP * g - jj for jj in range(ATTN_GROUP)]
        return js, [jnp.maximum(j, 0) for j in js]

    def row_term(g, jj, hh, j):
        far = mask_row(mfar_sc, hh, j)
        if jj == 0:
            return jnp.where(g == 0, 0.0, far)
        if jj == 1:
            return jnp.where(g == 0, mask_row(msel_sc, hh, j), far)
        return far

    def phase_a(g, buf, first):
        js, jc = block_ids(g)
        s_buf, gmax_buf = s_bufs[buf], gmax_bufs[buf]
        for hh in range(2):
            gmax = None
            for jj in range(ATTN_GROUP):
                s = scores(hh, jc[jj])
                if first and jj < 2:
                    s = s + bias_ref[hh, jj]
                s_buf[hh, jj] = s
                cm = jnp.max(s, axis=0, keepdims=True) + row_term(g, jj, hh, js[jj])
                gmax = cm if gmax is None else jnp.maximum(gmax, cm)
            gmax_buf[hh] = gmax

    def phase_b(g, buf):
        js, jc = block_ids(g)
        s_buf, gmax_buf = s_bufs[buf], gmax_bufs[buf]
        for hh in range(2):
            m_old = m_sc[hh]
            m_new = jnp.maximum(m_old, gmax_buf[hh])
            alpha = jnp.exp2(m_old - m_new)
            lsum = None
            acc = None
            for jj in range(ATTN_GROUP):
                p = jnp.exp2(s_buf[hh, jj] - (m_new - row_term(g, jj, hh, js[jj])))
                ps = jnp.sum(p, axis=0, keepdims=True)
                pa = pv(hh, jc[jj], p)
                lsum = ps if lsum is None else lsum + ps
                acc = pa if acc is None else acc + pa
            m_sc[hh] = m_new
            l_sc[hh] = alpha * l_sc[hh] + lsum
            acc_sc[hh] = alpha * acc_sc[hh] + acc

    phase_a(0, 0, True)
    for hh in range(2):
        m_sc[hh] = gmax0_sc[hh]
        l_sc[hh] = jnp.zeros((1, blk), F32)
        acc_sc[hh] = jnp.zeros((HEAD_DIM, blk), F32)

    n_groups = qt // ATTN_GROUP + 1

    def pipelined(g, carry):
        @pl.when(g % 2 == 0)
        def _():
            phase_a(g + 1, 1, False)
            phase_b(g, 0)

        @pl.when(g % 2 == 1)
        def _():
            phase_a(g + 1, 0, False)
            phase_b(g, 1)

        return carry

    lax.fori_loop(0, n_groups - 1, pipelined, 0)

    @pl.when(n_groups % 2 == 1)
    def _():
        phase_b(n_groups - 1, 0)

    @pl.when(n_groups % 2 == 0)
    def _():
        phase_b(n_groups - 1, 1)

    out = jnp.concatenate([acc_sc[hh] / l_sc[hh] for hh in range(2)], axis=0)
    o_ref[0] = out.T.astype(BF16)


def _moba_attention(q, k, v, bias_tiles, far_bias):
    bsz, seq_len, _ = q.shape
    blk = MOBA_BLOCK
    nb = seq_len // blk
    pair = 2 * HEAD_DIM
    return pl.pallas_call(
        functools.partial(_attn_kernel, nb=nb),
        grid=(bsz, N_HEADS // 2, nb),
        in_specs=[
            pl.BlockSpec(memory_space=pltpu.SMEM),
            pl.BlockSpec((1, blk, pair), lambda b, h, i: (b, i, h)),
            pl.BlockSpec((1, seq_len, pair), lambda b, h, i: (b, 0, h)),
            pl.BlockSpec((1, seq_len, pair), lambda b, h, i: (b, 0, h)),
            pl.BlockSpec((2, 2, blk, blk), lambda b, h, i: (h, 0, 0, 0)),
        ],
        out_specs=pl.BlockSpec((1, blk, pair), lambda b, h, i: (b, i, h)),
        out_shape=jax.ShapeDtypeStruct((bsz, seq_len, D_ATTN), BF16),
        scratch_shapes=[
            pltpu.VMEM((2, nb, pair), F32),
            pltpu.VMEM((nb, pair, blk), BF16),
            pltpu.VMEM((2, nb, blk), F32),
            pltpu.VMEM((2, nb, blk), F32),
            pltpu.VMEM((2, ATTN_GROUP, blk, blk), F32),
            pltpu.VMEM((2, ATTN_GROUP, blk, blk), F32),
            pltpu.VMEM((2, 1, blk), F32),
            pltpu.VMEM((2, 1, blk), F32),
            pltpu.VMEM((2, 1, blk), F32),
            pltpu.VMEM((2, 1, blk), F32),
            pltpu.VMEM((2, HEAD_DIM, blk), F32),
        ],
        compiler_params=_params(3),
        name="moba_attention",
    )(far_bias, q, k, v, bias_tiles)


def _ffn_body(x, g_ref, sc_ref, sh_ref, gate_ref, wg_ref, wu_ref, cw_ref, wd_ref, fin_ref, o_ref,
              gext_sc, carry_sc, acc_sc, h_sc, *, tm, per_seq, final_norm):
    @pl.when(pl.program_id(0) % per_seq == 0)
    def _():
        carry_sc[...] = jnp.zeros_like(carry_sc)

    h_sc[...] = _norm_mod(x, g_ref[...], sc_ref[0], sh_ref[0]).astype(BF16)

    for j in range(N_FFN_CHUNKS):
        cols = slice(j * FFN_CHUNK, (j + 1) * FFN_CHUNK)
        h = h_sc[...]
        gpre = jnp.dot(h, wg_ref[:, cols], preferred_element_type=F32)
        up = jnp.dot(h, wu_ref[:, cols], preferred_element_type=F32)
        gext_sc[j, 0:FFN_HALO, :] = carry_sc[j]
        gext_sc[j, FFN_HALO:, :] = gpre
        carry_sc[j] = gpre[tm - FFN_HALO:, :]
        cw = cw_ref[:, cols]
        conv = (cw[0:1] * gext_sc[j, FFN_HALO - 2:FFN_HALO - 2 + tm, :]
                + cw[1:2] * gext_sc[j, FFN_HALO - 1:FFN_HALO - 1 + tm, :]
                + cw[2:3] * gpre + cw[3:4])
        act = (_silu(conv) * up).astype(BF16)
        part = jnp.dot(act, wd_ref[cols, :], preferred_element_type=F32)
        if j == 0:
            acc_sc[...] = part
        else:
            acc_sc[...] += part
    out = x + gate_ref[0] * acc_sc[...]
    if final_norm:
        ms = jnp.mean(out * out, axis=-1, keepdims=True)
        out = out * lax.rsqrt(ms + EPS) * fin_ref[...]
    o_ref[...] = out


def _ffn0_kernel(x_ref, ys_ref, ya_ref, wo_ref, g1_ref, g_ref, sc_ref, sh_ref, gate_ref,
                 wg_ref, wu_ref, cw_ref, wd_ref, fin_ref, o_ref, gext_sc, carry_sc, acc_sc, h_sc, **kw):
    y = (jnp.dot(ys_ref[...], wo_ref[0:D_SSM, :], preferred_element_type=F32)
         + jnp.dot(ya_ref[...], wo_ref[D_SSM:, :], preferred_element_type=F32))
    x = x_ref[...] + g1_ref[0] * y
    _ffn_body(x, g_ref, sc_ref, sh_ref, gate_ref, wg_ref, wu_ref, cw_ref, wd_ref, fin_ref, o_ref,
              gext_sc, carry_sc, acc_sc, h_sc, **kw)


def _ffn1_kernel(x_ref, g_ref, sc_ref, sh_ref, gate_ref, wg_ref, wu_ref, cw_ref, wd_ref, fin_ref,
                 o_ref, gext_sc, carry_sc, acc_sc, h_sc, **kw):
    _ffn_body(x_ref[...], g_ref, sc_ref, sh_ref, gate_ref, wg_ref, wu_ref, cw_ref, wd_ref, fin_ref, o_ref,
              gext_sc, carry_sc, acc_sc, h_sc, **kw)


def _ffn_weights(w_up, w_gate, dw_w, dw_b, w_down):
    cw = jnp.concatenate([dw_w, dw_b[None, :]], axis=0)
    return w_gate.astype(BF16), w_up.astype(BF16), cw, w_down.astype(BF16)


def _conv_ffn(x2, mixer, norm_g, scale, shift, gate, weights, final_g, seq_len, final_norm, tm=512):
    tok, d = x2.shape
    per_seq = seq_len // tm
    wg, wu, cw, wd = weights
    row = lambda i: (i, 0)
    bat = lambda i: (i // per_seq, 0, 0)
    vec = pl.BlockSpec((1, 1, d), bat)
    common_specs = [_const_spec((1, d)), vec, vec, vec,
                    _const_spec(wg.shape), _const_spec(wu.shape), _const_spec(cw.shape), _const_spec(wd.shape),
                    _const_spec((1, d))]
    common_args = [norm_g, scale, shift, gate, wg, wu, cw, wd, final_g]
    kw = dict(tm=tm, per_seq=per_seq, final_norm=final_norm)
    if mixer is None:
        body = functools.partial(_ffn1_kernel, **kw)
        specs = [pl.BlockSpec((tm, d), row)] + common_specs
        args = [x2] + common_args
    else:
        ys, ya, wo, g1 = mixer
        body = functools.partial(_ffn0_kernel, **kw)
        specs = [pl.BlockSpec((tm, d), row), pl.BlockSpec((tm, D_SSM), row), pl.BlockSpec((tm, D_ATTN), row),
                 _const_spec(wo.shape), vec] + common_specs
        args = [x2, ys, ya, wo, g1] + common_args
    return pl.pallas_call(
        body,
        grid=(tok // tm,),
        in_specs=specs,
        out_specs=pl.BlockSpec((tm, d), row),
        out_shape=jax.ShapeDtypeStruct((tok, d), F32),
        scratch_shapes=[
            pltpu.VMEM((N_FFN_CHUNKS, tm + FFN_HALO, FFN_CHUNK), F32),
            pltpu.VMEM((N_FFN_CHUNKS, FFN_HALO, FFN_CHUNK), F32),
            pltpu.VMEM((tm, d), F32),
            pltpu.VMEM((tm, d), BF16),
        ],
        compiler_params=_params(1),
        name="conv_ffn_final" if final_norm else "conv_ffn",
    )(*args)


def _conformer_kernel(x_ref, g_ref, sc_ref, sh_ref, gate_ref, win_ref, bin_ref, dw_ref, dwb_ref,
                      lng_ref, lnb_ref, wout_ref, bout_ref, o_ref, aext_sc, ash_sc, conv_sc, *, tm, per_seq):
    x = x_ref[...]
    d = x.shape[-1]
    h = _norm_mod(x, g_ref[...], sc_ref[0], sh_ref[0]).astype(BF16)
    a = jnp.dot(h, win_ref[...], preferred_element_type=F32) + bin_ref[...]
    a = a[:, :d] * _sigmoid(a[:, d:])

    @pl.when(pl.program_id(0) % per_seq == 0)
    def _():
        aext_sc[0:CONV_HALO, :] = jnp.zeros((CONV_HALO, d), F32)

    aext_sc[CONV_HALO:, :] = a

    span = tm + CONV_HALO - 8
    for r in range(1, 8):
        ash_sc[r - 1, 0:span, :] = aext_sc[r:r + span, :]

    rc = 32
    off = CONV_HALO - (CONV_WIDTH - 1)
    for base in range(0, tm, rc):
        acc = jnp.zeros((rc, d), F32) + dwb_ref[...]
        for k in range(CONV_WIDTH):
            r = (off + k) % 8
            lo = base + off + k - r
            win = aext_sc[lo:lo + rc, :] if r == 0 else ash_sc[r - 1, lo:lo + rc, :]
            acc = acc + dw_ref[k:k + 1, :] * win
        conv_sc[base:base + rc, :] = acc
    aext_sc[0:CONV_HALO, :] = aext_sc[tm:tm + CONV_HALO, :]

    c = conv_sc[...]
    mu = jnp.mean(c, axis=-1, keepdims=True)
    xc = c - mu
    y = xc * lax.rsqrt(jnp.mean(xc * xc, axis=-1, keepdims=True) + EPS)
    y = _silu(y * lng_ref[...] + lnb_ref[...]).astype(BF16)
    out = jnp.dot(y, wout_ref[...], preferred_element_type=F32) + bout_ref[...]
    o_ref[...] = x + gate_ref[0] * out


def _conformer(x2, norm_g, scale, shift, gate, w_in, b_in, dw_w, dw_b, ln_g, ln_b, w_out, b_out, seq_len, tm=256):
    tok, d = x2.shape
    per_seq = seq_len // tm
    row = lambda i: (i, 0)
    bat = lambda i: (i // per_seq, 0, 0)
    vec = pl.BlockSpec((1, 1, d), bat)
    return pl.pallas_call(
        functools.partial(_conformer_kernel, tm=tm, per_seq=per_seq),
        grid=(tok // tm,),
        in_specs=[pl.BlockSpec((tm, d), row), _const_spec((1, d)), vec, vec, vec,
                  _const_spec((d, 2 * d)), _const_spec((1, 2 * d)), _const_spec((CONV_WIDTH, d)), _const_spec((1, d)),
                  _const_spec((1, d)), _const_spec((1, d)), _const_spec((d, d)), _const_spec((1, d))],
        out_specs=pl.BlockSpec((tm, d), row),
        out_shape=jax.ShapeDtypeStruct((tok, d), F32),
        scratch_shapes=[pltpu.VMEM((tm + CONV_HALO, d), F32), pltpu.VMEM((7, tm + CONV_HALO, d), F32),
                        pltpu.VMEM((tm, d), F32)],
        compiler_params=_params(1),
        name="conformer_conv",
    )(x2, norm_g, scale, shift, gate, w_in.astype(BF16), b_in.reshape(1, -1), dw_w, dw_b.reshape(1, -1),
      ln_g.reshape(1, -1), ln_b.reshape(1, -1), w_out.astype(BF16), b_out.reshape(1, -1))


def kernel(x, c, mod_w, mod_b, norm_g, final_g, ab_w_in, ssm_a_re, ssm_a_im, ssm_log_dt, ssm_b_re, ssm_b_im, ssm_c_re, ssm_c_im, ssm_d, ssm_glu_w, ssm_glu_b, ab_w_out, rel_bias, cm_w_in, cm_b_in, cm_dw_w, cm_dw_b, cm_ln_g, cm_ln_b, cm_w_out, cm_b_out, ffn_w_up, ffn_w_gate, ffn_dw_w, ffn_dw_b, ffn_w_down):
    bsz, seq_len, d = x.shape
    tok = bsz * seq_len
    x2 = x.reshape(tok, d)
    mod = _modulation(c, mod_w, mod_b)
    vecs = [[mod[l, :, i * d:(i + 1) * d].reshape(bsz, 1, d) for i in range(6)] for l in range(2)]
    fin = final_g.reshape(1, d)

    sh1, sc1, g1, sh2, sc2, g2 = vecs[0]
    u, q, k, v = _in_projection(x2, norm_g[0, 0].reshape(1, d), sc1, sh1, ab_w_in[0].astype(BF16), seq_len)
    ops = _s5_prepare(ssm_a_re[0], ssm_a_im[0], ssm_log_dt[0], ssm_b_re[0], ssm_b_im[0], ssm_c_re[0], ssm_c_im[0])
    y_ssm = _s5_mixer(u.reshape(S5_T, bsz, seq_len // S5_T, D_SSM), ops, ssm_d[0], ssm_glu_w[0], ssm_glu_b[0])
    att = lambda a: a.reshape(bsz, seq_len, D_ATTN)
    y_att = _moba_attention(att(q), att(k), att(v), _bias_tiles(rel_bias), rel_bias[REL_BUCKETS - 1])
    w0 = _ffn_weights(ffn_w_up[0], ffn_w_gate[0], ffn_dw_w[0], ffn_dw_b[0], ffn_w_down[0])
    x2 = _conv_ffn(x2, (y_ssm.reshape(tok, D_SSM), y_att.reshape(tok, D_ATTN), ab_w_out[0].astype(BF16), g1),
                   norm_g[0, 1].reshape(1, d), sc2, sh2, g2, w0, fin, seq_len, final_norm=False)

    sh1, sc1, g1, sh2, sc2, g2 = vecs[1]
    x2 = _conformer(x2, norm_g[1, 0].reshape(1, d), sc1, sh1, g1, cm_w_in[0], cm_b_in[0], cm_dw_w[0], cm_dw_b[0],
                    cm_ln_g[0], cm_ln_b[0], cm_w_out[0], cm_b_out[0], seq_len)
    w1 = _ffn_weights(ffn_w_up[1], ffn_w_gate[1], ffn_dw_w[1], ffn_dw_b[1], ffn_w_down[1])
    x2 = _conv_ffn(x2, None, norm_g[1, 1].reshape(1, d), sc2, sh2, g2, w1, fin, seq_len, final_norm=True)
    return x2.reshape(bsz, seq_len, d)
```

```python
import functools
import math

import numpy as np
import jax
import jax.numpy as jnp
from jax import lax
from jax.experimental import pallas as pl
from jax.experimental.pallas import tpu as pltpu

F32 = jnp.float32
BF16 = jnp.bfloat16

D_MODEL = 1024
D_SSM = 512
SSM_GROUP = 16
SSM_GROUPS = 32
SSM_STATE = 64
D_ATTN = 512
HEAD_DIM = 64
N_HEADS = 8
MOBA_BLOCK = 256
MOBA_TOPK = 3
REL_BUCKETS = 32
REL_MAX_DIST = 128
CONV_WIDTH = 31
FFN_HIDDEN = 2816
FFN_CONV_WIDTH = 3
EPS = 1e-6

NEG = -1e30
LOG2E = math.log2(math.e)

V7X_VMEM_BYTES = 64 * 1024 * 1024
VMEM_LIMIT = V7X_VMEM_BYTES - 8 * 1024 * 1024

S5_T = 4
S5_CB = 32
S5_ROW_STRIDE = S5_CB + 8
HALF = 256
GROUPS_PER_HALF = HALF // SSM_GROUP
FFN_CHUNK = 256
N_FFN_CHUNKS = FFN_HIDDEN // FFN_CHUNK
CONV_HALO = 32
FFN_HALO = 8
CONV_SUBTILES = 2
ATTN_GROUP = 4


def _sigmoid(x):
    return 0.5 * jnp.tanh(0.5 * x) + 0.5


def _silu(x):
    return x * _sigmoid(x)


def _gelu_tanh(x):
    c = math.sqrt(2.0 / math.pi)
    return 0.5 * x * (1.0 + jnp.tanh(c * (x + 0.044715 * (x * x * x))))


def _norm_mod(x, g, scale, shift):
    ms = jnp.mean(x * x, axis=-1, keepdims=True)
    y = x * lax.rsqrt(ms + EPS) * g
    return y * (1.0 + scale) + shift


def _params(n_axes, vmem=VMEM_LIMIT, flags=None):
    return pltpu.CompilerParams(dimension_semantics=("arbitrary",) * n_axes, vmem_limit_bytes=vmem, flags=flags)


def _const_spec(shape):
    nd = len(shape)
    return pl.BlockSpec(shape, lambda *_: (0,) * nd, pipeline_mode=pl.Buffered(1))


def _mod_kernel(c_ref, w_ref, b_ref, o_ref):
    c = c_ref[...]
    cs = _silu(c).astype(BF16)
    o_ref[0] = jnp.dot(cs, w_ref[0].astype(BF16), preferred_element_type=F32) + b_ref[0]


def _modulation(c, mod_w, mod_b):
    depth, d, n = mod_w.shape
    bsz = c.shape[0]
    nt = 1536
    return pl.pallas_call(
        _mod_kernel,
        grid=(depth, n // nt),
        in_specs=[
            pl.BlockSpec((bsz, d), lambda l, j: (0, 0)),
            pl.BlockSpec((1, d, nt), lambda l, j: (l, 0, j)),
            pl.BlockSpec((1, 1, nt), lambda l, j: (l, 0, j)),
        ],
        out_specs=pl.BlockSpec((1, bsz, nt), lambda l, j: (l, 0, j)),
        out_shape=jax.ShapeDtypeStruct((depth, bsz, n), F32),
        compiler_params=_params(2),
        name="modulation",
    )(c, mod_w, mod_b.reshape(depth, 1, n))


def _inproj_kernel(x_ref, g_ref, sc_ref, sh_ref, w_ref, u_ref, q_ref, k_ref, v_ref, u_sc):
    h = _norm_mod(x_ref[...], g_ref[...], sc_ref[0], sh_ref[0]).astype(BF16)
    p = jnp.dot(h, w_ref[...], preferred_element_type=F32)
    tm = p.shape[0]
    lanes = D_SSM // 128
    for l in range(lanes):
        u_sc[l] = p[:, l * 128:(l + 1) * 128]
    for s in range(S5_T):
        u_ref[s] = jnp.concatenate([u_sc[l, pl.ds(s, tm // S5_T, stride=S5_T), :] for l in range(lanes)], axis=-1)
    q_ref[...] = (p[:, D_SSM:D_SSM + D_ATTN] * (HEAD_DIM ** -0.5 * LOG2E)).astype(BF16)
    k_ref[...] = p[:, D_SSM + D_ATTN:D_SSM + 2 * D_ATTN].astype(BF16)
    v_ref[...] = p[:, D_SSM + 2 * D_ATTN:].astype(BF16)


def _in_projection(x2, g, scale, shift, w, seq_len, tm=512):
    tok, d = x2.shape
    per_seq = seq_len // tm
    n = w.shape[1]
    row = lambda i: (i, 0)
    bat = lambda i: (i // per_seq, 0, 0)
    return pl.pallas_call(
        _inproj_kernel,
        grid=(tok // tm,),
        in_specs=[
            pl.BlockSpec((tm, d), row),
            _const_spec((1, d)),
            pl.BlockSpec((1, 1, d), bat),
            pl.BlockSpec((1, 1, d), bat),
            _const_spec((d, n)),
        ],
        out_specs=[
            pl.BlockSpec((S5_T, tm // S5_T, D_SSM), lambda i: (0, i, 0)),
            pl.BlockSpec((tm, D_ATTN), row),
            pl.BlockSpec((tm, D_ATTN), row),
            pl.BlockSpec((tm, D_ATTN), row),
        ],
        out_shape=[
            jax.ShapeDtypeStruct((S5_T, tok // S5_T, D_SSM), F32),
            jax.ShapeDtypeStruct((tok, D_ATTN), BF16),
            jax.ShapeDtypeStruct((tok, D_ATTN), BF16),
            jax.ShapeDtypeStruct((tok, D_ATTN), BF16),
        ],
        scratch_shapes=[pltpu.VMEM((D_SSM // 128, tm, 128), F32)],
        compiler_params=_params(1),
        name="in_projection",
    )(x2, g, scale, shift, w)


def _s5_prep_kernel(lre_r, lim_r, ldt_r, lre_c, lim_c, ldt_c, btr, bti, cre, cim, ctr, cti,
                    kt_ref, sre_ref, sim_ref, ore_ref, oim_ref, at_ref):
    def discretise(lre, lim, ldt):
        dt = jnp.exp(ldt)
        mag = jnp.exp(lre * dt)
        return mag * jnp.cos(lim * dt), mag * jnp.sin(lim * dt)

    lre, lim = lre_r[...], lim_r[...]
    ar, ai = discretise(lre, lim, ldt_r[...])
    den = lre * lre + lim * lim
    nr = ar - 1.0
    coef_re = (nr * lre + ai * lim) / den
    coef_im = (ai * lre - nr * lim) / den
    br, bi = btr[...], bti[...]
    zr = coef_re * br - coef_im * bi
    zi = coef_re * bi + coef_im * br
    c_re, c_im = cre[...], cim[...]
    for k in range(S5_T):
        sre_ref[S5_T - 1 - k] = zr
        sim_ref[S5_T - 1 - k] = zi
        for h in range(SSM_GROUP):
            kt_ref[k, h] = jnp.sum(c_re[:, h:h + 1, :] * zr - c_im[:, h:h + 1, :] * zi, axis=-1)
        zr, zi = ar * zr - ai * zi, ar * zi + ai * zr

    acr, aci = discretise(lre_c[...], lim_c[...], ldt_c[...])
    pr, pi = acr, aci
    ct_re, ct_im = ctr[...], cti[...]
    for t in range(S5_T):
        ore_ref[t] = ct_re * pr - ct_im * pi
        oim_ref[t] = -ct_re * pi - ct_im * pr
        pr, pi = acr * pr - aci * pi, acr * pi + aci * pr

    qr, qi = ar, ai
    for _ in range(S5_T - 1):
        qr, qi = ar * qr - ai * qi, ar * qi + ai * qr
    at_ref[0] = qr
    at_ref[1] = qi


def _s5_prepare(a_re, a_im, log_dt, b_re, b_im, c_re, c_im):
    g, p, h, t = SSM_GROUPS, SSM_STATE, SSM_GROUP, S5_T
    ins = [
        a_re.reshape(g, 1, p), a_im.reshape(g, 1, p), log_dt.reshape(g, 1, 1),
        a_re.reshape(g, p, 1), a_im.reshape(g, p, 1), log_dt.reshape(g, 1, 1),
        b_re.transpose(0, 2, 1), b_im.transpose(0, 2, 1), c_re, c_im,
        c_re.transpose(0, 2, 1), c_im.transpose(0, 2, 1),
    ]
    full = lambda s: pl.BlockSpec(s, lambda: (0,) * len(s))
    out_shapes = [(t, h, g, h), (t, g, h, p), (t, g, h, p), (t, g, p, h), (t, g, p, h), (2, g, 1, p)]
    kt, sre, sim, ore, oim, at = pl.pallas_call(
        _s5_prep_kernel,
        in_specs=[full(a.shape) for a in ins],
        out_specs=[full(s) for s in out_shapes],
        out_shape=[jax.ShapeDtypeStruct(s, F32) for s in out_shapes],
        name="s5_prepare",
    )(*ins)

    gh = GROUPS_PER_HALF
    eye = jnp.eye(gh, dtype=F32)
    ktg = kt.transpose(0, 2, 3, 1).reshape(t, 2, gh, h, h)
    toe = jnp.einsum("kfgab,gc->kfgacb", ktg, eye).reshape(t, 2, HALF, HALF)

    def s_tiles(s):
        sg = s.reshape(t, 2, gh, h, p)
        return jnp.einsum("sfgap,gc->sfgacp", sg, eye).reshape(t, 2, HALF, gh * p)

    def o_tiles(o):
        og = o.reshape(t, 2, gh, p, h)
        return jnp.einsum("tfgpb,gc->tfgpcb", og, eye).reshape(t, 2, gh * p, HALF)

    smat = jnp.concatenate([s_tiles(sre), s_tiles(sim)], axis=-1)
    omat = jnp.concatenate([o_tiles(ore), o_tiles(oim)], axis=-2)
    atr = at[0].reshape(2, gh * p)
    ati = at[1].reshape(2, gh * p)
    return toe.astype(BF16), smat.astype(BF16), omat.astype(BF16), atr, ati


def _s5_kernel(u_ref, toe_ref, smat_ref, omat_ref, atr_ref, ati_ref, d_ref, gw_ref, gb_ref,
               y_ref, acc_sc, s_sc, xp_sc, cr_sc, ci_sc, y_sc, *, bsz):
    rows = bsz * S5_CB
    hs = GROUPS_PER_HALF * SSM_STATE

    @pl.when(pl.program_id(0) == 0)
    def _():
        cr_sc[...] = jnp.zeros_like(cr_sc)
        ci_sc[...] = jnp.zeros_like(ci_sc)

    u = [u_ref[s].reshape(rows, D_SSM) for s in range(S5_T)]
    ub = [v.astype(BF16) for v in u]

    def half(f):
        return slice(f * HALF, (f + 1) * HALF)

    def col(t, f):
        return slice(t * D_SSM + f * HALF, t * D_SSM + (f + 1) * HALF)

    lt = 2 * hs // 128
    for f in range(2):
        st = None
        for s in range(S5_T):
            part = jnp.dot(ub[s][:, half(f)], smat_ref[s, f], preferred_element_type=F32)
            st = part if st is None else st + part
        for l in range(lt):
            for b in range(bsz):
                s_sc[f * lt + l, b * S5_ROW_STRIDE:b * S5_ROW_STRIDE + S5_CB, :] = (
                    st[b * S5_CB:(b + 1) * S5_CB, l * 128:(l + 1) * 128])

    for f in range(2):
        for t in range(S5_T):
            acc = None
            for s in range(t + 1):
                part = jnp.dot(ub[s][:, half(f)], toe_ref[t - s, f], preferred_element_type=F32)
                acc = part if acc is None else acc + part
            acc_sc[:, col(t, f)] = acc

    half_lt = lt // 2
    for f in range(2):
        for l in range(half_lt):
            cols = slice(f * hs + l * 128, f * hs + (l + 1) * 128)
            a_r = atr_ref[f:f + 1, l * 128:(l + 1) * 128]
            a_i = ati_ref[f:f + 1, l * 128:(l + 1) * 128]
            xr = cr_sc[:, cols]
            xi = ci_sc[:, cols]
            t_re = f * lt + l
            t_im = f * lt + half_lt + l
            for c in range(S5_CB):
                idx = pl.ds(c, bsz, stride=S5_ROW_STRIDE)
                xp_sc[t_re, idx, :] = xr
                xp_sc[t_im, idx, :] = xi
                sr = s_sc[t_re, idx, :]
                si = s_sc[t_im, idx, :]
                xr, xi = a_r * xr - a_i * xi + sr, a_r * xi + a_i * xr + si
            cr_sc[:, cols] = xr
            ci_sc[:, cols] = xi

    def xp_tile(i):
        return jnp.concatenate([xp_sc[i, b * S5_ROW_STRIDE:b * S5_ROW_STRIDE + S5_CB, :] for b in range(bsz)], axis=0)

    xpb = [jnp.concatenate([xp_tile(f * lt + l) for l in range(lt)], axis=-1).astype(BF16) for f in range(2)]
    gw = gw_ref[...]
    for t in range(S5_T):
        ys = []
        for f in range(2):
            carry = jnp.dot(xpb[f], omat_ref[t, f], preferred_element_type=F32)
            ys.append(acc_sc[:, col(t, f)] + carry + d_ref[:, half(f)] * u[t][:, half(f)])
        y = _gelu_tanh(jnp.concatenate(ys, axis=-1))
        z = jnp.dot(y.astype(BF16), gw, preferred_element_type=F32) + gb_ref[...]
        out = y * _sigmoid(z)
        for b in range(bsz):
            for l in range(D_SSM // 128):
                y_sc[l, pl.ds(b * S5_T * S5_CB + t, S5_CB, stride=S5_T), :] = (
                    out[b * S5_CB:(b + 1) * S5_CB, l * 128:(l + 1) * 128])
    span = S5_T * S5_CB
    for b in range(bsz):
        y_ref[b] = jnp.concatenate([y_sc[l, b * span:(b + 1) * span, :] for l in range(D_SSM // 128)],
                                   axis=-1).astype(BF16)


def _s5_mixer(u, ops, d_skip, glu_w, glu_b):
    _, bsz, nchunk, _ = u.shape
    seq_len = nchunk * S5_T
    toe, smat, omat, atr, ati = ops
    width = S5_T * D_SSM
    rows = bsz * S5_CB
    state_w = 2 * 2 * GROUPS_PER_HALF * SSM_STATE
    return pl.pallas_call(
        functools.partial(_s5_kernel, bsz=bsz),
        grid=(nchunk // S5_CB,),
        in_specs=[
            pl.BlockSpec((S5_T, bsz, S5_CB, D_SSM), lambda i: (0, 0, i, 0)),
            _const_spec(toe.shape), _const_spec(smat.shape), _const_spec(omat.shape),
            _const_spec(atr.shape), _const_spec(ati.shape),
            _const_spec((1, D_SSM)), _const_spec((D_SSM, D_SSM)), _const_spec((1, D_SSM)),
        ],
        out_specs=pl.BlockSpec((bsz, S5_T * S5_CB, D_SSM), lambda i: (0, i, 0)),
        out_shape=jax.ShapeDtypeStruct((bsz, seq_len, D_SSM), BF16),
        scratch_shapes=[
            pltpu.VMEM((rows, width), F32),
            pltpu.VMEM((state_w // 128, bsz * S5_ROW_STRIDE, 128), F32),
            pltpu.VMEM((state_w // 128, bsz * S5_ROW_STRIDE, 128), F32),
            pltpu.VMEM((bsz, state_w // 2), F32),
            pltpu.VMEM((bsz, state_w // 2), F32),
            pltpu.VMEM((D_SSM // 128, rows * S5_T, 128), F32),
        ],
        compiler_params=_params(1),
        name="s5_mixer",
    )(u, toe, smat, omat, atr, ati, d_skip.reshape(1, D_SSM), glu_w.astype(BF16), glu_b.reshape(1, D_SSM))


def _rel_bucket_np(dist):
    n = np.maximum(dist, 0)
    max_exact = REL_BUCKETS // 2
    nf = np.maximum(n, 1).astype(np.float64)
    large = max_exact + (np.log(nf / max_exact) / math.log(REL_MAX_DIST / max_exact)
                         * (REL_BUCKETS - max_exact)).astype(np.int64)
    large = np.minimum(large, REL_BUCKETS - 1)
    return np.where(n < max_exact, n, large).astype(np.int32)


def _bias_bucket_tiles():
    ko = np.arange(MOBA_BLOCK)[:, None]
    qo = np.arange(MOBA_BLOCK)[None, :]
    own = np.where(qo >= ko, _rel_bucket_np(qo - ko), -1)
    prev = _rel_bucket_np(qo - ko + MOBA_BLOCK)
    return np.stack([own, prev]).astype(np.int32)


assert int(_rel_bucket_np(np.arange(MOBA_BLOCK + 1, 8 * MOBA_BLOCK)).min()) == REL_BUCKETS - 1


def _bias_kernel(tab_ref, idx_ref, o_ref):
    h = pl.program_id(0)
    for t in range(2):
        idx = idx_ref[t]
        acc = jnp.full(idx.shape, NEG, F32)
        for b in range(REL_BUCKETS):
            acc = jnp.where(idx == b, tab_ref[h, b] * LOG2E, acc)
        o_ref[0, t] = acc


def _bias_tiles(rel_bias):
    idx = jnp.asarray(_bias_bucket_tiles())
    blk = MOBA_BLOCK
    return pl.pallas_call(
        _bias_kernel,
        grid=(N_HEADS,),
        in_specs=[
            pl.BlockSpec(memory_space=pltpu.SMEM),
            pl.BlockSpec((2, blk, blk), lambda h: (0, 0, 0)),
        ],
        out_specs=pl.BlockSpec((1, 2, blk, blk), lambda h: (h, 0, 0, 0)),
        out_shape=jax.ShapeDtypeStruct((N_HEADS, 2, blk, blk), F32),
        compiler_params=_params(1),
        name="moba_bias_tiles",
    )(rel_bias.T, idx)


def _attn_kernel(far_ref, q_ref, k_ref, v_ref, bias_ref, o_ref,
                 kmean_sc, vt_sc, mfar_sc, msel_sc, s0_sc, s1_sc, gmax0_sc, gmax1_sc, m_sc, l_sc, acc_sc, *, nb):
    s_bufs = (s0_sc, s1_sc)
    gmax_bufs = (gmax0_sc, gmax1_sc)
    hp = pl.program_id(1)
    qt = pl.program_id(2)
    blk = MOBA_BLOCK
    lane = lax.broadcasted_iota(jnp.int32, (1, 2 * HEAD_DIM), 1)
    head_mask = [lane < HEAD_DIM, lane >= HEAD_DIM]
    nt = (((1,), (1,)), ((), ()))

    @pl.when(qt == 0)
    def _():
        for j in range(nb):
            kb = k_ref[0, j * blk:(j + 1) * blk, :].astype(F32)
            km = jnp.mean(kb, axis=0, keepdims=True)
            for hh in range(2):
                kmean_sc[hh, j:j + 1, :] = jnp.where(head_mask[hh], km, 0.0)
            vt_sc[j] = v_ref[0, j * blk:(j + 1) * blk, :].astype(F32).T.astype(BF16)

    q2 = q_ref[0]
    jidx = lax.broadcasted_iota(jnp.int32, (nb, blk), 0)
    qm = []
    for hh in range(2):
        qm.append(jnp.where(head_mask[hh], q2, jnp.zeros_like(q2)))
        gate = lax.dot_general(kmean_sc[hh].astype(BF16), q2, nt, preferred_element_type=F32)
        rank = jnp.zeros((nb, blk), F32)
        for jp in range(nb):
            row = gate[jp:jp + 1, :]
            beats = (row > gate) | ((row == gate) & (jidx > jp))
            rank = rank + jnp.where(beats & (qt > jp), 1.0, 0.0)
        sel = (rank < float(MOBA_TOPK)) & (jidx < qt)
        mfar_sc[hh] = jnp.where(sel, far_ref[2 * hp + hh] * LOG2E, NEG)
        msel_sc[hh] = jnp.where(sel, 0.0, NEG)

    def scores(hh, j):
        kb = k_ref[0, pl.ds(pl.multiple_of(j * blk, blk), blk), :]
        return lax.dot_general(kb, qm[hh], nt, preferred_element_type=F32)

    def pv(hh, j, p):
        vt = vt_sc[j, hh * HEAD_DIM:(hh + 1) * HEAD_DIM, :]
        return jnp.dot(vt, p.astype(BF16), preferred_element_type=F32)

    def mask_row(ref, hh, j):
        return jnp.where(j >= 0, ref[hh, pl.ds(jnp.maximum(j, 0), 1), :], NEG)

    def block_ids(g):
        js = [qt - ATTN_GROUP * g - jj for jj in range(ATTN_GROUP)]
        return js, [jnp.maximum(j, 0) for j in js]

    def row_term(g, jj, hh, j):
        far = mask_row(mfar_sc, hh, j)
        if jj == 0:
            return jnp.where(g == 0, 0.0, far)
        if jj == 1:
            return jnp.where(g == 0, mask_row(msel_sc, hh, j), far)
        return far

    def phase_a(g, buf, first):
        js, jc = block_ids(g)
        s_buf, gmax_buf = s_bufs[buf], gmax_bufs[buf]
        for hh in range(2):
            gmax = None
            for jj in range(ATTN_GROUP):
                s = scores(hh, jc[jj])
                if first and jj < 2:
                    s = s + bias_ref[hh, jj]
                s_buf[hh, jj] = s
                cm = jnp.max(s, axis=0, keepdims=True) + row_term(g, jj, hh, js[jj])
                gmax = cm if gmax is None else jnp.maximum(gmax, cm)
            gmax_buf[hh] = gmax

    def phase_b(g, buf):
        js, jc = block_ids(g)
        s_buf, gmax_buf = s_bufs[buf], gmax_bufs[buf]
        for hh in range(2):
            m_old = m_sc[hh]
            m_new = jnp.maximum(m_old, gmax_buf[hh])
            alpha = jnp.exp2(m_old - m_new)
            lsum = None
            acc = None
            for jj in range(ATTN_GROUP):
                p = jnp.exp2(s_buf[hh, jj] - (m_new - row_term(g, jj, hh, js[jj])))
                ps = jnp.sum(p, axis=0, keepdims=True)
                pa = pv(hh, jc[jj], p)
                lsum = ps if lsum is None else lsum + ps
                acc = pa if acc is None else acc + pa
            m_sc[hh] = m_new
            l_sc[hh] = alpha * l_sc[hh] + lsum
            acc_sc[hh] = alpha * acc_sc[hh] + acc

    phase_a(0, 0, True)
    for hh in range(2):
        m_sc[hh] = gmax0_sc[hh]
        l_sc[hh] = jnp.zeros((1, blk), F32)
        acc_sc[hh] = jnp.zeros((HEAD_DIM, blk), F32)

    n_groups = qt // ATTN_GROUP + 1

    def pipelined(g, carry):
        @pl.when(g % 2 == 0)
        def _():
            phase_a(g + 1, 1, False)
            phase_b(g, 0)

        @pl.when(g % 2 == 1)
        def _():
            phase_a(g + 1, 0, False)
            phase_b(g, 1)

        return carry

    lax.fori_loop(0, n_groups - 1, pipelined, 0)

    @pl.when(n_groups % 2 == 1)
    def _():
        phase_b(n_groups - 1, 0)

    @pl.when(n_groups % 2 == 0)
    def _():
        phase_b(n_groups - 1, 1)

    out = jnp.concatenate([acc_sc[hh] / l_sc[hh] for hh in range(2)], axis=0)
    o_ref[0] = out.T.astype(BF16)


def _moba_attention(q, k, v, bias_tiles, far_bias):
    bsz, seq_len, _ = q.shape
    blk = MOBA_BLOCK
    nb = seq_len // blk
    pair = 2 * HEAD_DIM
    return pl.pallas_call(
        functools.partial(_attn_kernel, nb=nb),
        grid=(bsz, N_HEADS // 2, nb),
        in_specs=[
            pl.BlockSpec(memory_space=pltpu.SMEM),
            pl.BlockSpec((1, blk, pair), lambda b, h, i: (b, i, h)),
            pl.BlockSpec((1, seq_len, pair), lambda b, h, i: (b, 0, h)),
            pl.BlockSpec((1, seq_len, pair), lambda b, h, i: (b, 0, h)),
            pl.BlockSpec((2, 2, blk, blk), lambda b, h, i: (h, 0, 0, 0)),
        ],
        out_specs=pl.BlockSpec((1, blk, pair), lambda b, h, i: (b, i, h)),
        out_shape=jax.ShapeDtypeStruct((bsz, seq_len, D_ATTN), BF16),
        scratch_shapes=[
            pltpu.VMEM((2, nb, pair), F32),
            pltpu.VMEM((nb, pair, blk), BF16),
            pltpu.VMEM((2, nb, blk), F32),
            pltpu.VMEM((2, nb, blk), F32),
            pltpu.VMEM((2, ATTN_GROUP, blk, blk), F32),
            pltpu.VMEM((2, ATTN_GROUP, blk, blk), F32),
            pltpu.VMEM((2, 1, blk), F32),
            pltpu.VMEM((2, 1, blk), F32),
            pltpu.VMEM((2, 1, blk), F32),
            pltpu.VMEM((2, 1, blk), F32),
            pltpu.VMEM((2, HEAD_DIM, blk), F32),
        ],
        compiler_params=_params(3),
        name="moba_attention",
    )(far_bias, q, k, v, bias_tiles)


def _ffn_body(x, g_ref, sc_ref, sh_ref, gate_ref, wg_ref, wu_ref, cw_ref, wd_ref, fin_ref, o_ref,
              gext_sc, carry_sc, acc_sc, h_sc, *, tm, per_seq, final_norm):
    @pl.when(pl.program_id(0) % per_seq == 0)
    def _():
        carry_sc[...] = jnp.zeros_like(carry_sc)

    h_sc[...] = _norm_mod(x, g_ref[...], sc_ref[0], sh_ref[0]).astype(BF16)

    def gate_up(j):
        cols = slice(j * FFN_CHUNK, (j + 1) * FFN_CHUNK)
        h = h_sc[...]
        return (jnp.dot(h, wg_ref[:, cols], preferred_element_type=F32),
                jnp.dot(h, wu_ref[:, cols], preferred_element_type=F32))

    nxt = gate_up(0)
    for j in range(N_FFN_CHUNKS):
        cols = slice(j * FFN_CHUNK, (j + 1) * FFN_CHUNK)
        gpre, up = nxt
        if j + 1 < N_FFN_CHUNKS:
            nxt = gate_up(j + 1)
        gext_sc[j, 0:FFN_HALO, :] = carry_sc[j]
        gext_sc[j, FFN_HALO:, :] = gpre
        carry_sc[j] = gpre[tm - FFN_HALO:, :]
        cw = cw_ref[:, cols]
        conv = (cw[0:1] * gext_sc[j, FFN_HALO - 2:FFN_HALO - 2 + tm, :]
                + cw[1:2] * gext_sc[j, FFN_HALO - 1:FFN_HALO - 1 + tm, :]
                + cw[2:3] * gpre + cw[3:4])
        act = (_silu(conv) * up).astype(BF16)
        part = jnp.dot(act, wd_ref[cols, :], preferred_element_type=F32)
        if j == 0:
            acc_sc[...] = part
        else:
            acc_sc[...] += part
    out = x + gate_ref[0] * acc_sc[...]
    if final_norm:
        ms = jnp.mean(out * out, axis=-1, keepdims=True)
        out = out * lax.rsqrt(ms + EPS) * fin_ref[...]
    o_ref[...] = out


def _ffn0_kernel(x_ref, ys_ref, ya_ref, wo_ref, g1_ref, g_ref, sc_ref, sh_ref, gate_ref,
                 wg_ref, wu_ref, cw_ref, wd_ref, fin_ref, o_ref, gext_sc, carry_sc, acc_sc, h_sc, **kw):
    y = (jnp.dot(ys_ref[...], wo_ref[0:D_SSM, :], preferred_element_type=F32)
         + jnp.dot(ya_ref[...], wo_ref[D_SSM:, :], preferred_element_type=F32))
    x = x_ref[...] + g1_ref[0] * y
    _ffn_body(x, g_ref, sc_ref, sh_ref, gate_ref, wg_ref, wu_ref, cw_ref, wd_ref, fin_ref, o_ref,
              gext_sc, carry_sc, acc_sc, h_sc, **kw)


def _ffn1_kernel(x_ref, g_ref, sc_ref, sh_ref, gate_ref, wg_ref, wu_ref, cw_ref, wd_ref, fin_ref,
                 o_ref, gext_sc, carry_sc, acc_sc, h_sc, **kw):
    _ffn_body(x_ref[...], g_ref, sc_ref, sh_ref, gate_ref, wg_ref, wu_ref, cw_ref, wd_ref, fin_ref, o_ref,
              gext_sc, carry_sc, acc_sc, h_sc, **kw)


def _ffn_weights(w_up, w_gate, dw_w, dw_b, w_down):
    cw = jnp.concatenate([dw_w, dw_b[None, :]], axis=0)
    return w_gate.astype(BF16), w_up.astype(BF16), cw, w_down.astype(BF16)


def _conv_ffn(x2, mixer, norm_g, scale, shift, gate, weights, final_g, seq_len, final_norm, tm=256):
    tok, d = x2.shape
    per_seq = seq_len // tm
    wg, wu, cw, wd = weights
    row = lambda i: (i, 0)
    bat = lambda i: (i // per_seq, 0, 0)
    vec = pl.BlockSpec((1, 1, d), bat)
    common_specs = [_const_spec((1, d)), vec, vec, vec,
                    _const_spec(wg.shape), _const_spec(wu.shape), _const_spec(cw.shape), _const_spec(wd.shape),
                    _const_spec((1, d))]
    common_args = [norm_g, scale, shift, gate, wg, wu, cw, wd, final_g]
    kw = dict(tm=tm, per_seq=per_seq, final_norm=final_norm)
    if mixer is None:
        body = functools.partial(_ffn1_kernel, **kw)
        specs = [pl.BlockSpec((tm, d), row)] + common_specs
        args = [x2] + common_args
    else:
        ys, ya, wo, g1 = mixer
        body = functools.partial(_ffn0_kernel, **kw)
        specs = [pl.BlockSpec((tm, d), row), pl.BlockSpec((tm, D_SSM), row), pl.BlockSpec((tm, D_ATTN), row),
                 _const_spec(wo.shape), vec] + common_specs
        args = [x2, ys, ya, wo, g1] + common_args
    return pl.pallas_call(
        body,
        grid=(tok // tm,),
        in_specs=specs,
        out_specs=pl.BlockSpec((tm, d), row),
        out_shape=jax.ShapeDtypeStruct((tok, d), F32),
        scratch_shapes=[
            pltpu.VMEM((N_FFN_CHUNKS, tm + FFN_HALO, FFN_CHUNK), F32),
            pltpu.VMEM((N_FFN_CHUNKS, FFN_HALO, FFN_CHUNK), F32),
            pltpu.VMEM((tm, d), F32),
            pltpu.VMEM((tm, d), BF16),
        ],
        compiler_params=_params(1),
        name="conv_ffn_final" if final_norm else "conv_ffn",
    )(*args)


def _conformer_kernel(x_ref, g_ref, sc_ref, sh_ref, gate_ref, win_ref, bin_ref, dw_ref, dwb_ref,
                      lng_ref, lnb_ref, wout_ref, bout_ref, shift_ref, o_ref,
                      aext_sc, ash_sc, xs_sc, conv_sc, h_sc, *, tm, per_seq):
    i = pl.program_id(0)
    d = x_ref.shape[-1]
    bufs = (aext_sc, ash_sc, xs_sc)
    sub = tm // CONV_SUBTILES
    sub_ext = sub + CONV_HALO
    off = CONV_HALO - (CONV_WIDTH - 1)

    rc = 32
    n_chunks = tm // rc

    def stage1_pieces(cur, prev):
        aext_sc, ash_sc, xs_sc = cur

        def project():
            x = x_ref[...]
            xs_sc[...] = x
            h_sc[...] = _norm_mod(x, g_ref[...], sc_ref[0], sh_ref[0]).astype(BF16)
            a = jnp.dot(h_sc[...], win_ref[...], preferred_element_type=F32) + bin_ref[...]
            a = a[:, :d] * _sigmoid(a[:, d:])
            if prev is None:
                aext_sc[0:CONV_HALO, :] = jnp.zeros((CONV_HALO, d), F32)
            else:
                aext_sc[0:CONV_HALO, :] = jnp.where(i % per_seq == 0, 0.0, prev[0][tm:tm + CONV_HALO, :])
            aext_sc[CONV_HALO:, :] = a

        def shifts(s):
            def run():
                ab = aext_sc[s * sub:s * sub + sub_ext, :].astype(BF16)
                for r in range(7):
                    ash_sc[s, r] = jnp.dot(shift_ref[r], ab, preferred_element_type=F32)
            return run

        return [project] + [shifts(s) for s in range(CONV_SUBTILES)]

    def stage2_pieces(cur):
        aext_sc, ash_sc, xs_sc = cur

        def conv(base):
            def run():
                s = base // sub
                acc = [dwb_ref[...]] * (rc // 8)
                for k in range(CONV_WIDTH):
                    r = (off + k) % 8
                    lo = base + off + k - r
                    w8 = dw_ref[k]
                    for j in range(rc // 8):
                        if r == 0:
                            win = aext_sc[lo + 8 * j:lo + 8 * j + 8, :]
                        else:
                            win = ash_sc[s, r - 1, lo - s * sub + 8 * j:lo - s * sub + 8 * j + 8, :]
                        acc[j] = acc[j] + w8 * win
                for j in range(rc // 8):
                    conv_sc[base + 8 * j:base + 8 * j + 8, :] = acc[j]
            return run

        def finish():
            c = conv_sc[...]
            mu = jnp.mean(c, axis=-1, keepdims=True)
            xc = c - mu
            y = xc * lax.rsqrt(jnp.mean(xc * xc, axis=-1, keepdims=True) + EPS)
            y = _silu(y * lng_ref[...] + lnb_ref[...]).astype(BF16)
            out = jnp.dot(y, wout_ref[...], preferred_element_type=F32) + bout_ref[...]
            o_ref[...] = xs_sc[...] + gate_ref[0] * out

        return [conv(base) for base in range(0, tm, rc)] + [finish]

    def run_all(pieces):
        for piece in pieces:
            piece()

    @pl.when(i == 0)
    def _():
        run_all(stage1_pieces(bufs, None) + stage2_pieces(bufs))

    @pl.when(i > 0)
    def _():
        run_all(stage1_pieces(bufs, bufs) + stage2_pieces(bufs))


def _row_shift_matrix(rows, n_shifts):
    t = np.arange(rows)
    m = np.zeros((n_shifts, rows, rows), np.float32)
    for r in range(1, n_shifts + 1):
        keep = t + r < rows
        m[r - 1, t[keep], t[keep] + r] = 1.0
    return m


def _conformer(x2, norm_g, scale, shift, gate, w_in, b_in, dw_w, dw_b, ln_g, ln_b, w_out, b_out, seq_len, tm=256):
    tok, d = x2.shape
    per_seq = seq_len // tm
    sub_ext = tm // CONV_SUBTILES + CONV_HALO
    shift_mat = jnp.asarray(_row_shift_matrix(sub_ext, 7), dtype=BF16)
    row = lambda i: (i, 0)
    vec = pl.BlockSpec((1, 1, d), lambda i: (i // per_seq, 0, 0))
    return pl.pallas_call(
        functools.partial(_conformer_kernel, tm=tm, per_seq=per_seq),
        grid=(tok // tm,),
        in_specs=[pl.BlockSpec((tm, d), row), _const_spec((1, d)), vec, vec, vec,
                  _const_spec((d, 2 * d)), _const_spec((1, 2 * d)), _const_spec((CONV_WIDTH, 8, d)), _const_spec((8, d)),
                  _const_spec((1, d)), _const_spec((1, d)), _const_spec((d, d)), _const_spec((1, d)),
                  _const_spec(shift_mat.shape)],
        out_specs=pl.BlockSpec((tm, d), row),
        out_shape=jax.ShapeDtypeStruct((tok, d), F32),
        scratch_shapes=[pltpu.VMEM((tm + CONV_HALO, d), F32), pltpu.VMEM((CONV_SUBTILES, 7, sub_ext, d), F32),
                        pltpu.VMEM((tm, d), F32), pltpu.VMEM((tm, d), F32), pltpu.VMEM((tm, d), BF16)],
        compiler_params=_params(1),
        name="conformer_conv",
    )(x2, norm_g, scale, shift, gate, w_in.astype(BF16), b_in.reshape(1, -1),
      jnp.broadcast_to(dw_w[:, None, :], (CONV_WIDTH, 8, d)), jnp.broadcast_to(dw_b[None, :], (8, d)),
      ln_g.reshape(1, -1), ln_b.reshape(1, -1), w_out.astype(BF16), b_out.reshape(1, -1), shift_mat)


def kernel(x, c, mod_w, mod_b, norm_g, final_g, ab_w_in, ssm_a_re, ssm_a_im, ssm_log_dt, ssm_b_re, ssm_b_im, ssm_c_re, ssm_c_im, ssm_d, ssm_glu_w, ssm_glu_b, ab_w_out, rel_bias, cm_w_in, cm_b_in, cm_dw_w, cm_dw_b, cm_ln_g, cm_ln_b, cm_w_out, cm_b_out, ffn_w_up, ffn_w_gate, ffn_dw_w, ffn_dw_b, ffn_w_down):
    bsz, seq_len, d = x.shape
    tok = bsz * seq_len
    x2 = x.reshape(tok, d)
    mod = _modulation(c, mod_w, mod_b)
    vecs = [[mod[l, :, i * d:(i + 1) * d].reshape(bsz, 1, d) for i in range(6)] for l in range(2)]
    fin = final_g.reshape(1, d)

    sh1, sc1, g1, sh2, sc2, g2 = vecs[0]
    u, q, k, v = _in_projection(x2, norm_g[0, 0].reshape(1, d), sc1, sh1, ab_w_in[0].astype(BF16), seq_len)
    ops = _s5_prepare(ssm_a_re[0], ssm_a_im[0], ssm_log_dt[0], ssm_b_re[0], ssm_b_im[0], ssm_c_re[0], ssm_c_im[0])
    y_ssm = _s5_mixer(u.reshape(S5_T, bsz, seq_len // S5_T, D_SSM), ops, ssm_d[0], ssm_glu_w[0], ssm_glu_b[0])
    att = lambda a: a.reshape(bsz, seq_len, D_ATTN)
    y_att = _moba_attention(att(q), att(k), att(v), _bias_tiles(rel_bias), rel_bias[REL_BUCKETS - 1])
    w0 = _ffn_weights(ffn_w_up[0], ffn_w_gate[0], ffn_dw_w[0], ffn_dw_b[0], ffn_w_down[0])
    x2 = _conv_ffn(x2, (y_ssm.reshape(tok, D_SSM), y_att.reshape(tok, D_ATTN), ab_w_out[0].astype(BF16), g1),
                   norm_g[0, 1].reshape(1, d), sc2, sh2, g2, w0, fin, seq_len, final_norm=False)

    sh1, sc1, g1, sh2, sc2, g2 = vecs[1]
    x2 = _conformer(x2, norm_g[1, 0].reshape(1, d), sc1, sh1, g1, cm_w_in[0], cm_b_in[0], cm_dw_w[0], cm_dw_b[0],
                    cm_ln_g[0], cm_ln_b[0], cm_w_out[0], cm_b_out[0], seq_len)
    w1 = _ffn_weights(ffn_w_up[1], ffn_w_gate[1], ffn_dw_w[1], ffn_dw_b[1], ffn_w_down[1])
    x2 = _conv_ffn(x2, None, norm_g[1, 1].reshape(1, d), sc2, sh2, g2, w1, fin, seq_len, final_norm=True)
    return x2.reshape(bsz, seq_len, d)
```

```python
import functools
import math

import numpy as np
import jax
import jax.numpy as jnp
from jax import lax
from jax.experimental import pallas as pl
from jax.experimental.pallas import tpu as pltpu

F32 = jnp.float32
BF16 = jnp.bfloat16

D_MODEL = 1024
D_SSM = 512
SSM_GROUP = 16
SSM_GROUPS = 32
SSM_STATE = 64
D_ATTN = 512
HEAD_DIM = 64
N_HEADS = 8
MOBA_BLOCK = 256
MOBA_TOPK = 3
REL_BUCKETS = 32
REL_MAX_DIST = 128
CONV_WIDTH = 31
FFN_HIDDEN = 2816
FFN_CONV_WIDTH = 3
EPS = 1e-6

NEG = -1e30
LOG2E = math.log2(math.e)

V7X_VMEM_BYTES = 64 * 1024 * 1024
VMEM_LIMIT = V7X_VMEM_BYTES - 8 * 1024 * 1024

S5_T = 4
S5_CB = 32
S5_ROW_STRIDE = S5_CB + 8
HALF = 256
GROUPS_PER_HALF = HALF // SSM_GROUP
FFN_CHUNK = 256
N_FFN_CHUNKS = FFN_HIDDEN // FFN_CHUNK
CONV_HALO = 32
FFN_HALO = 8
CONV_SUBTILES = 2
ATTN_GROUP = 4
PV_ROWS = HEAD_DIM + 16


def _sigmoid(x):
    return 0.5 * jnp.tanh(0.5 * x) + 0.5


def _silu(x):
    return x * _sigmoid(x)


def _gelu_tanh(x):
    c = math.sqrt(2.0 / math.pi)
    return 0.5 * x * (1.0 + jnp.tanh(c * (x + 0.044715 * (x * x * x))))


def _norm_mod(x, g, scale, shift):
    ms = jnp.mean(x * x, axis=-1, keepdims=True)
    y = x * lax.rsqrt(ms + EPS) * g
    return y * (1.0 + scale) + shift


def _params(n_axes, vmem=VMEM_LIMIT, flags=None):
    return pltpu.CompilerParams(dimension_semantics=("arbitrary",) * n_axes, vmem_limit_bytes=vmem, flags=flags)


def _const_spec(shape):
    nd = len(shape)
    return pl.BlockSpec(shape, lambda *_: (0,) * nd, pipeline_mode=pl.Buffered(1))


def _mod_kernel(c_ref, w_ref, b_ref, o_ref):
    c = c_ref[...]
    cs = _silu(c).astype(BF16)
    o_ref[0] = jnp.dot(cs, w_ref[0].astype(BF16), preferred_element_type=F32) + b_ref[0]


def _modulation(c, mod_w, mod_b):
    depth, d, n = mod_w.shape
    bsz = c.shape[0]
    nt = 1536
    return pl.pallas_call(
        _mod_kernel,
        grid=(depth, n // nt),
        in_specs=[
            pl.BlockSpec((bsz, d), lambda l, j: (0, 0)),
            pl.BlockSpec((1, d, nt), lambda l, j: (l, 0, j)),
            pl.BlockSpec((1, 1, nt), lambda l, j: (l, 0, j)),
        ],
        out_specs=pl.BlockSpec((1, bsz, nt), lambda l, j: (l, 0, j)),
        out_shape=jax.ShapeDtypeStruct((depth, bsz, n), F32),
        compiler_params=_params(2),
        name="modulation",
    )(c, mod_w, mod_b.reshape(depth, 1, n))


def _inproj_kernel(x_ref, g_ref, sc_ref, sh_ref, w_ref, u_ref, q_ref, k_ref, v_ref, u_sc):
    h = _norm_mod(x_ref[...], g_ref[...], sc_ref[0], sh_ref[0]).astype(BF16)
    p = jnp.dot(h, w_ref[...], preferred_element_type=F32)
    tm = p.shape[0]
    lanes = D_SSM // 128
    for l in range(lanes):
        u_sc[l] = p[:, l * 128:(l + 1) * 128]
    for s in range(S5_T):
        u_ref[s] = jnp.concatenate([u_sc[l, pl.ds(s, tm // S5_T, stride=S5_T), :] for l in range(lanes)], axis=-1)
    q_ref[...] = (p[:, D_SSM:D_SSM + D_ATTN] * (HEAD_DIM ** -0.5 * LOG2E)).astype(BF16)
    k_ref[...] = p[:, D_SSM + D_ATTN:D_SSM + 2 * D_ATTN].astype(BF16)
    v_ref[...] = p[:, D_SSM + 2 * D_ATTN:].astype(BF16)


def _in_projection(x2, g, scale, shift, w, seq_len, tm=512):
    tok, d = x2.shape
    per_seq = seq_len // tm
    n = w.shape[1]
    row = lambda i: (i, 0)
    bat = lambda i: (i // per_seq, 0, 0)
    return pl.pallas_call(
        _inproj_kernel,
        grid=(tok // tm,),
        in_specs=[
            pl.BlockSpec((tm, d), row),
            _const_spec((1, d)),
            pl.BlockSpec((1, 1, d), bat),
            pl.BlockSpec((1, 1, d), bat),
            _const_spec((d, n)),
        ],
        out_specs=[
            pl.BlockSpec((S5_T, tm // S5_T, D_SSM), lambda i: (0, i, 0)),
            pl.BlockSpec((tm, D_ATTN), row),
            pl.BlockSpec((tm, D_ATTN), row),
            pl.BlockSpec((tm, D_ATTN), row),
        ],
        out_shape=[
            jax.ShapeDtypeStruct((S5_T, tok // S5_T, D_SSM), F32),
            jax.ShapeDtypeStruct((tok, D_ATTN), BF16),
            jax.ShapeDtypeStruct((tok, D_ATTN), BF16),
            jax.ShapeDtypeStruct((tok, D_ATTN), BF16),
        ],
        scratch_shapes=[pltpu.VMEM((D_SSM // 128, tm, 128), F32)],
        compiler_params=_params(1),
        name="in_projection",
    )(x2, g, scale, shift, w)


def _s5_prep_kernel(lre_r, lim_r, ldt_r, lre_c, lim_c, ldt_c, btr, bti, cre, cim, ctr, cti,
                    kt_ref, sre_ref, sim_ref, ore_ref, oim_ref, at_ref):
    def discretise(lre, lim, ldt):
        dt = jnp.exp(ldt)
        mag = jnp.exp(lre * dt)
        return mag * jnp.cos(lim * dt), mag * jnp.sin(lim * dt)

    lre, lim = lre_r[...], lim_r[...]
    ar, ai = discretise(lre, lim, ldt_r[...])
    den = lre * lre + lim * lim
    nr = ar - 1.0
    coef_re = (nr * lre + ai * lim) / den
    coef_im = (ai * lre - nr * lim) / den
    br, bi = btr[...], bti[...]
    zr = coef_re * br - coef_im * bi
    zi = coef_re * bi + coef_im * br
    c_re, c_im = cre[...], cim[...]
    for k in range(S5_T):
        sre_ref[S5_T - 1 - k] = zr
        sim_ref[S5_T - 1 - k] = zi
        for h in range(SSM_GROUP):
            kt_ref[k, h] = jnp.sum(c_re[:, h:h + 1, :] * zr - c_im[:, h:h + 1, :] * zi, axis=-1)
        zr, zi = ar * zr - ai * zi, ar * zi + ai * zr

    acr, aci = discretise(lre_c[...], lim_c[...], ldt_c[...])
    pr, pi = acr, aci
    ct_re, ct_im = ctr[...], cti[...]
    for t in range(S5_T):
        ore_ref[t] = ct_re * pr - ct_im * pi
        oim_ref[t] = -ct_re * pi - ct_im * pr
        pr, pi = acr * pr - aci * pi, acr * pi + aci * pr

    qr, qi = ar, ai
    for _ in range(S5_T - 1):
        qr, qi = ar * qr - ai * qi, ar * qi + ai * qr
    at_ref[0] = qr
    at_ref[1] = qi


def _s5_prepare(a_re, a_im, log_dt, b_re, b_im, c_re, c_im):
    g, p, h, t = SSM_GROUPS, SSM_STATE, SSM_GROUP, S5_T
    ins = [
        a_re.reshape(g, 1, p), a_im.reshape(g, 1, p), log_dt.reshape(g, 1, 1),
        a_re.reshape(g, p, 1), a_im.reshape(g, p, 1), log_dt.reshape(g, 1, 1),
        b_re.transpose(0, 2, 1), b_im.transpose(0, 2, 1), c_re, c_im,
        c_re.transpose(0, 2, 1), c_im.transpose(0, 2, 1),
    ]
    full = lambda s: pl.BlockSpec(s, lambda: (0,) * len(s))
    out_shapes = [(t, h, g, h), (t, g, h, p), (t, g, h, p), (t, g, p, h), (t, g, p, h), (2, g, 1, p)]
    kt, sre, sim, ore, oim, at = pl.pallas_call(
        _s5_prep_kernel,
        in_specs=[full(a.shape) for a in ins],
        out_specs=[full(s) for s in out_shapes],
        out_shape=[jax.ShapeDtypeStruct(s, F32) for s in out_shapes],
        name="s5_prepare",
    )(*ins)

    gh = GROUPS_PER_HALF
    eye = jnp.eye(gh, dtype=F32)
    ktg = kt.transpose(0, 2, 3, 1).reshape(t, 2, gh, h, h)
    toe = jnp.einsum("kfgab,gc->kfgacb", ktg, eye).reshape(t, 2, HALF, HALF)

    def s_tiles(s):
        sg = s.reshape(t, 2, gh, h, p)
        return jnp.einsum("sfgap,gc->sfgacp", sg, eye).reshape(t, 2, HALF, gh * p)

    def o_tiles(o):
        og = o.reshape(t, 2, gh, p, h)
        return jnp.einsum("tfgpb,gc->tfgpcb", og, eye).reshape(t, 2, gh * p, HALF)

    smat = jnp.concatenate([s_tiles(sre), s_tiles(sim)], axis=-1)
    omat = jnp.concatenate([o_tiles(ore), o_tiles(oim)], axis=-2)
    atr = at[0].reshape(2, gh * p)
    ati = at[1].reshape(2, gh * p)
    return toe.astype(BF16), smat.astype(BF16), omat.astype(BF16), atr, ati


def _s5_kernel(u_ref, toe_ref, smat_ref, omat_ref, atr_ref, ati_ref, d_ref, gw_ref, gb_ref,
               y_ref, acc_sc, s_sc, xp_sc, cr_sc, ci_sc, y_sc, *, bsz):
    rows = bsz * S5_CB
    hs = GROUPS_PER_HALF * SSM_STATE

    @pl.when(pl.program_id(0) == 0)
    def _():
        cr_sc[...] = jnp.zeros_like(cr_sc)
        ci_sc[...] = jnp.zeros_like(ci_sc)

    u = [u_ref[s].reshape(rows, D_SSM) for s in range(S5_T)]
    ub = [v.astype(BF16) for v in u]

    def half(f):
        return slice(f * HALF, (f + 1) * HALF)

    def col(t, f):
        return slice(t * D_SSM + f * HALF, t * D_SSM + (f + 1) * HALF)

    lt = 2 * hs // 128
    for f in range(2):
        st = None
        for s in range(S5_T):
            part = jnp.dot(ub[s][:, half(f)], smat_ref[s, f], preferred_element_type=F32)
            st = part if st is None else st + part
        for l in range(lt):
            for b in range(bsz):
                s_sc[f * lt + l, b * S5_ROW_STRIDE:b * S5_ROW_STRIDE + S5_CB, :] = (
                    st[b * S5_CB:(b + 1) * S5_CB, l * 128:(l + 1) * 128])

    for f in range(2):
        for t in range(S5_T):
            acc = None
            for s in range(t + 1):
                part = jnp.dot(ub[s][:, half(f)], toe_ref[t - s, f], preferred_element_type=F32)
                acc = part if acc is None else acc + part
            acc_sc[:, col(t, f)] = acc

    half_lt = lt // 2
    for f in range(2):
        for l in range(half_lt):
            cols = slice(f * hs + l * 128, f * hs + (l + 1) * 128)
            a_r = atr_ref[f:f + 1, l * 128:(l + 1) * 128]
            a_i = ati_ref[f:f + 1, l * 128:(l + 1) * 128]
            xr = cr_sc[:, cols]
            xi = ci_sc[:, cols]
            t_re = f * lt + l
            t_im = f * lt + half_lt + l
            for c in range(S5_CB):
                idx = pl.ds(c, bsz, stride=S5_ROW_STRIDE)
                xp_sc[t_re, idx, :] = xr
                xp_sc[t_im, idx, :] = xi
                sr = s_sc[t_re, idx, :]
                si = s_sc[t_im, idx, :]
                xr, xi = a_r * xr - a_i * xi + sr, a_r * xi + a_i * xr + si
            cr_sc[:, cols] = xr
            ci_sc[:, cols] = xi

    def xp_tile(i):
        return jnp.concatenate([xp_sc[i, b * S5_ROW_STRIDE:b * S5_ROW_STRIDE + S5_CB, :] for b in range(bsz)], axis=0)

    xpb = [jnp.concatenate([xp_tile(f * lt + l) for l in range(lt)], axis=-1).astype(BF16) for f in range(2)]
    gw = gw_ref[...]
    for t in range(S5_T):
        ys = []
        for f in range(2):
            carry = jnp.dot(xpb[f], omat_ref[t, f], preferred_element_type=F32)
            ys.append(acc_sc[:, col(t, f)] + carry + d_ref[:, half(f)] * u[t][:, half(f)])
        y = _gelu_tanh(jnp.concatenate(ys, axis=-1))
        z = jnp.dot(y.astype(BF16), gw, preferred_element_type=F32) + gb_ref[...]
        out = y * _sigmoid(z)
        for b in range(bsz):
            for l in range(D_SSM // 128):
                y_sc[l, pl.ds(b * S5_T * S5_CB + t, S5_CB, stride=S5_T), :] = (
                    out[b * S5_CB:(b + 1) * S5_CB, l * 128:(l + 1) * 128])
    span = S5_T * S5_CB
    for b in range(bsz):
        y_ref[b] = jnp.concatenate([y_sc[l, b * span:(b + 1) * span, :] for l in range(D_SSM // 128)],
                                   axis=-1).astype(BF16)


def _s5_mixer(u, ops, d_skip, glu_w, glu_b):
    _, bsz, nchunk, _ = u.shape
    seq_len = nchunk * S5_T
    toe, smat, omat, atr, ati = ops
    width = S5_T * D_SSM
    rows = bsz * S5_CB
    state_w = 2 * 2 * GROUPS_PER_HALF * SSM_STATE
    return pl.pallas_call(
        functools.partial(_s5_kernel, bsz=bsz),
        grid=(nchunk // S5_CB,),
        in_specs=[
            pl.BlockSpec((S5_T, bsz, S5_CB, D_SSM), lambda i: (0, 0, i, 0)),
            _const_spec(toe.shape), _const_spec(smat.shape), _const_spec(omat.shape),
            _const_spec(atr.shape), _const_spec(ati.shape),
            _const_spec((1, D_SSM)), _const_spec((D_SSM, D_SSM)), _const_spec((1, D_SSM)),
        ],
        out_specs=pl.BlockSpec((bsz, S5_T * S5_CB, D_SSM), lambda i: (0, i, 0)),
        out_shape=jax.ShapeDtypeStruct((bsz, seq_len, D_SSM), BF16),
        scratch_shapes=[
            pltpu.VMEM((rows, width), F32),
            pltpu.VMEM((state_w // 128, bsz * S5_ROW_STRIDE, 128), F32),
            pltpu.VMEM((state_w // 128, bsz * S5_ROW_STRIDE, 128), F32),
            pltpu.VMEM((bsz, state_w // 2), F32),
            pltpu.VMEM((bsz, state_w // 2), F32),
            pltpu.VMEM((D_SSM // 128, rows * S5_T, 128), F32),
        ],
        compiler_params=_params(1),
        name="s5_mixer",
    )(u, toe, smat, omat, atr, ati, d_skip.reshape(1, D_SSM), glu_w.astype(BF16), glu_b.reshape(1, D_SSM))


def _rel_bucket_np(dist):
    n = np.maximum(dist, 0)
    max_exact = REL_BUCKETS // 2
    nf = np.maximum(n, 1).astype(np.float64)
    large = max_exact + (np.log(nf / max_exact) / math.log(REL_MAX_DIST / max_exact)
                         * (REL_BUCKETS - max_exact)).astype(np.int64)
    large = np.minimum(large, REL_BUCKETS - 1)
    return np.where(n < max_exact, n, large).astype(np.int32)


def _bias_bucket_tiles():
    ko = np.arange(MOBA_BLOCK)[:, None]
    qo = np.arange(MOBA_BLOCK)[None, :]
    own = np.where(qo >= ko, _rel_bucket_np(qo - ko), -1)
    prev = _rel_bucket_np(qo - ko + MOBA_BLOCK)
    return np.stack([own, prev]).astype(np.int32)


assert int(_rel_bucket_np(np.arange(MOBA_BLOCK + 1, 8 * MOBA_BLOCK)).min()) == REL_BUCKETS - 1


def _bias_kernel(tab_ref, idx_ref, o_ref):
    h = pl.program_id(0)
    for t in range(2):
        idx = idx_ref[t]
        acc = jnp.full(idx.shape, NEG, F32)
        for b in range(REL_BUCKETS):
            acc = jnp.where(idx == b, tab_ref[h, b] * LOG2E, acc)
        o_ref[0, t] = acc


def _bias_tiles(rel_bias):
    idx = jnp.asarray(_bias_bucket_tiles())
    blk = MOBA_BLOCK
    return pl.pallas_call(
        _bias_kernel,
        grid=(N_HEADS,),
        in_specs=[
            pl.BlockSpec(memory_space=pltpu.SMEM),
            pl.BlockSpec((2, blk, blk), lambda h: (0, 0, 0)),
        ],
        out_specs=pl.BlockSpec((1, 2, blk, blk), lambda h: (h, 0, 0, 0)),
        out_shape=jax.ShapeDtypeStruct((N_HEADS, 2, blk, blk), F32),
        compiler_params=_params(1),
        name="moba_bias_tiles",
    )(rel_bias.T, idx)


def _attn_kernel(far_ref, q_ref, k_ref, v_ref, bias_ref, o_ref,
                 kmean_sc, vt_sc, mfar_sc, msel_sc, s0_sc, s1_sc, gmax0_sc, gmax1_sc, m_sc, acc_sc, *, nb):
    s_bufs = (s0_sc, s1_sc)
    gmax_bufs = (gmax0_sc, gmax1_sc)
    hp = pl.program_id(1)
    qt = pl.program_id(2)
    blk = MOBA_BLOCK
    qw = 2 * blk
    top = 2 * qt + 1
    qlane = lax.broadcasted_iota(jnp.int32, (1, qw), 1)
    own = 2 * qt + jnp.where(qlane >= blk, 1, 0)
    lane = lax.broadcasted_iota(jnp.int32, (1, 2 * HEAD_DIM), 1)
    head_mask = [lane < HEAD_DIM, lane >= HEAD_DIM]
    nt = (((1,), (1,)), ((), ()))

    @pl.when(qt == 0)
    def _():
        for j in range(nb):
            kb = k_ref[0, j * blk:(j + 1) * blk, :].astype(F32)
            km = jnp.mean(kb, axis=0, keepdims=True)
            for hh in range(2):
                kmean_sc[hh, j:j + 1, :] = jnp.where(head_mask[hh], km, 0.0)
            vt = v_ref[0, j * blk:(j + 1) * blk, :].astype(F32).T.astype(BF16)
            ones_row = jnp.where(lax.broadcasted_iota(jnp.int32, (PV_ROWS - HEAD_DIM, blk), 0) == 0, 1.0, 0.0)
            for hh in range(2):
                vt_sc[hh, j, 0:HEAD_DIM, :] = vt[hh * HEAD_DIM:(hh + 1) * HEAD_DIM, :]
                vt_sc[hh, j, HEAD_DIM:, :] = ones_row.astype(BF16)

    q2 = q_ref[0]
    jidx = lax.broadcasted_iota(jnp.int32, (nb, qw), 0)
    qm = []
    for hh in range(2):
        qm.append(jnp.where(head_mask[hh], q2, jnp.zeros_like(q2)))
        gate = lax.dot_general(kmean_sc[hh].astype(BF16), q2, nt, preferred_element_type=F32)
        rank = jnp.zeros((nb, qw), F32)
        for jp in range(nb):
            row = gate[jp:jp + 1, :]
            beats = (row > gate) | ((row == gate) & (jidx > jp))
            rank = rank + jnp.where(beats & (own > jp), 1.0, 0.0)
        sel = (rank < float(MOBA_TOPK)) & (jidx < own)
        mfar_sc[hh] = jnp.where(sel, far_ref[2 * hp + hh] * LOG2E, NEG)
        msel_sc[hh] = jnp.where(sel, 0.0, NEG)

    def scores(hh, j):
        kb = k_ref[0, pl.ds(pl.multiple_of(j * blk, blk), blk), :]
        return lax.dot_general(kb, qm[hh], nt, preferred_element_type=F32)

    def pv(hh, j, p):
        return jnp.dot(vt_sc[hh, j], p.astype(BF16), preferred_element_type=F32)

    def mask_row(ref, hh, j):
        return jnp.where(j >= 0, ref[hh, pl.ds(jnp.maximum(j, 0), 1), :], NEG)

    def block_ids(g):
        js = [top - ATTN_GROUP * g - jj for jj in range(ATTN_GROUP)]
        return js, [jnp.maximum(j, 0) for j in js]

    def row_term(g, jj, hh, j):
        far = mask_row(mfar_sc, hh, j)
        if jj > 2:
            return far
        second = qlane >= blk
        if jj == 0:
            special = jnp.where(second, 0.0, NEG)
        elif jj == 1:
            special = jnp.where(second, mask_row(msel_sc, hh, j), 0.0)
        else:
            special = jnp.where(second, far, mask_row(msel_sc, hh, j))
        return jnp.where(g == 0, special, far)

    class PhaseA:
        def __init__(self, g, buf, first):
            self.g, self.first = g, first
            self.js, self.jc = block_ids(g)
            self.s_buf, self.gmax_buf = s_bufs[buf], gmax_bufs[buf]
            self.gmax = [None, None]

        def block(self, hh, jj):
            s = scores(hh, self.jc[jj])
            if self.first and jj < 3:
                tiles = [(None, 0), (0, 1), (1, None)][jj]
                parts = [s[:, h * blk:(h + 1) * blk] if t is None else s[:, h * blk:(h + 1) * blk] + bias_ref[hh, t]
                         for h, t in enumerate(tiles)]
                s = jnp.concatenate(parts, axis=1)
            self.s_buf[hh, jj] = s
            cm = jnp.max(s, axis=0, keepdims=True) + row_term(self.g, jj, hh, self.js[jj])
            self.gmax[hh] = cm if self.gmax[hh] is None else jnp.maximum(self.gmax[hh], cm)

        def finish(self):
            for hh in range(2):
                self.gmax_buf[hh] = self.gmax[hh]

    class PhaseB:
        def __init__(self, g, buf):
            self.g = g
            self.js, self.jc = block_ids(g)
            self.s_buf = s_bufs[buf]
            self.m_old = [m_sc[hh] for hh in range(2)]
            self.m_new = [jnp.maximum(self.m_old[hh], gmax_bufs[buf][hh]) for hh in range(2)]
            self.acc = [None, None]

        def block(self, hh, jj):
            p = jnp.exp2(self.s_buf[hh, jj] - (self.m_new[hh] - row_term(self.g, jj, hh, self.js[jj])))
            pa = pv(hh, self.jc[jj], p)
            self.acc[hh] = pa if self.acc[hh] is None else self.acc[hh] + pa

        def finish(self):
            for hh in range(2):
                alpha = jnp.exp2(self.m_old[hh] - self.m_new[hh])
                m_sc[hh] = self.m_new[hh]
                acc_sc[hh] = alpha * acc_sc[hh] + self.acc[hh]

    def run(*phases):
        for hh in range(2):
            for jj in range(ATTN_GROUP):
                for ph in phases:
                    ph.block(hh, jj)
        for ph in phases:
            ph.finish()

    run(PhaseA(0, 0, True))
    for hh in range(2):
        m_sc[hh] = gmax0_sc[hh]
        acc_sc[hh] = jnp.zeros((PV_ROWS, qw), F32)

    n_groups = top // ATTN_GROUP + 1

    def pipelined(g, carry):
        @pl.when(g % 2 == 0)
        def _():
            run(PhaseB(g, 0), PhaseA(g + 1, 1, False))

        @pl.when(g % 2 == 1)
        def _():
            run(PhaseB(g, 1), PhaseA(g + 1, 0, False))

        return carry

    lax.fori_loop(0, n_groups - 1, pipelined, 0)

    @pl.when(n_groups % 2 == 1)
    def _():
        run(PhaseB(n_groups - 1, 0))

    @pl.when(n_groups % 2 == 0)
    def _():
        run(PhaseB(n_groups - 1, 1))

    out = jnp.concatenate([acc_sc[hh, 0:HEAD_DIM, :] / acc_sc[hh, HEAD_DIM:HEAD_DIM + 1, :] for hh in range(2)],
                          axis=0)
    o_ref[0] = out.T.astype(BF16)


def _moba_attention(q, k, v, bias_tiles, far_bias):
    bsz, seq_len, _ = q.shape
    blk = MOBA_BLOCK
    nb = seq_len // blk
    pair = 2 * HEAD_DIM
    qw = 2 * blk
    assert nb % 2 == 0
    return pl.pallas_call(
        functools.partial(_attn_kernel, nb=nb),
        grid=(bsz, N_HEADS // 2, nb // 2),
        in_specs=[
            pl.BlockSpec(memory_space=pltpu.SMEM),
            pl.BlockSpec((1, qw, pair), lambda b, h, i: (b, i, h)),
            pl.BlockSpec((1, seq_len, pair), lambda b, h, i: (b, 0, h)),
            pl.BlockSpec((1, seq_len, pair), lambda b, h, i: (b, 0, h)),
            pl.BlockSpec((2, 2, blk, blk), lambda b, h, i: (h, 0, 0, 0)),
        ],
        out_specs=pl.BlockSpec((1, qw, pair), lambda b, h, i: (b, i, h)),
        out_shape=jax.ShapeDtypeStruct((bsz, seq_len, D_ATTN), BF16),
        scratch_shapes=[
            pltpu.VMEM((2, nb, pair), F32),
            pltpu.VMEM((2, nb, PV_ROWS, blk), BF16),
            pltpu.VMEM((2, nb, qw), F32),
            pltpu.VMEM((2, nb, qw), F32),
            pltpu.VMEM((2, ATTN_GROUP, blk, qw), F32),
            pltpu.VMEM((2, ATTN_GROUP, blk, qw), F32),
            pltpu.VMEM((2, 1, qw), F32),
            pltpu.VMEM((2, 1, qw), F32),
            pltpu.VMEM((2, 1, qw), F32),
            pltpu.VMEM((2, PV_ROWS, qw), F32),
        ],
        compiler_params=_params(3),
        name="moba_attention",
    )(far_bias, q, k, v, bias_tiles)


def _ffn_body(x, g_ref, sc_ref, sh_ref, gate_ref, wg_ref, wu_ref, cw_ref, wd_ref, fin_ref, o_ref,
              gext_sc, carry_sc, acc_sc, h_sc, *, tm, per_seq, final_norm):
    @pl.when(pl.program_id(0) % per_seq == 0)
    def _():
        carry_sc[...] = jnp.zeros_like(carry_sc)

    h_sc[...] = _norm_mod(x, g_ref[...], sc_ref[0], sh_ref[0]).astype(BF16)

    def gate_up(j):
        cols = slice(j * FFN_CHUNK, (j + 1) * FFN_CHUNK)
        h = h_sc[...]
        return (jnp.dot(h, wg_ref[:, cols], preferred_element_type=F32),
                jnp.dot(h, wu_ref[:, cols], preferred_element_type=F32))

    nxt = gate_up(0)
    for j in range(N_FFN_CHUNKS):
        cols = slice(j * FFN_CHUNK, (j + 1) * FFN_CHUNK)
        gpre, up = nxt
        if j + 1 < N_FFN_CHUNKS:
            nxt = gate_up(j + 1)
        gext_sc[j, 0:FFN_HALO, :] = carry_sc[j]
        gext_sc[j, FFN_HALO:, :] = gpre
        carry_sc[j] = gpre[tm - FFN_HALO:, :]
        cw = cw_ref[:, cols]
        conv = (cw[0:1] * gext_sc[j, FFN_HALO - 2:FFN_HALO - 2 + tm, :]
                + cw[1:2] * gext_sc[j, FFN_HALO - 1:FFN_HALO - 1 + tm, :]
                + cw[2:3] * gpre + cw[3:4])
        act = (_silu(conv) * up).astype(BF16)
        part = jnp.dot(act, wd_ref[cols, :], preferred_element_type=F32)
        if j == 0:
            acc_sc[...] = part
        else:
            acc_sc[...] += part
    out = x + gate_ref[0] * acc_sc[...]
    if final_norm:
        ms = jnp.mean(out * out, axis=-1, keepdims=True)
        out = out * lax.rsqrt(ms + EPS) * fin_ref[...]
    o_ref[...] = out


def _ffn0_kernel(x_ref, ys_ref, ya_ref, wo_ref, g1_ref, g_ref, sc_ref, sh_ref, gate_ref,
                 wg_ref, wu_ref, cw_ref, wd_ref, fin_ref, o_ref, gext_sc, carry_sc, acc_sc, h_sc, **kw):
    y = (jnp.dot(ys_ref[...], wo_ref[0:D_SSM, :], preferred_element_type=F32)
         + jnp.dot(ya_ref[...], wo_ref[D_SSM:, :], preferred_element_type=F32))
    x = x_ref[...] + g1_ref[0] * y
    _ffn_body(x, g_ref, sc_ref, sh_ref, gate_ref, wg_ref, wu_ref, cw_ref, wd_ref, fin_ref, o_ref,
              gext_sc, carry_sc, acc_sc, h_sc, **kw)


def _ffn1_kernel(x_ref, g_ref, sc_ref, sh_ref, gate_ref, wg_ref, wu_ref, cw_ref, wd_ref, fin_ref,
                 o_ref, gext_sc, carry_sc, acc_sc, h_sc, **kw):
    _ffn_body(x_ref[...], g_ref, sc_ref, sh_ref, gate_ref, wg_ref, wu_ref, cw_ref, wd_ref, fin_ref, o_ref,
              gext_sc, carry_sc, acc_sc, h_sc, **kw)


def _ffn_weights(w_up, w_gate, dw_w, dw_b, w_down):
    cw = jnp.concatenate([dw_w, dw_b[None, :]], axis=0)
    return w_gate.astype(BF16), w_up.astype(BF16), cw, w_down.astype(BF16)


def _conv_ffn(x2, mixer, norm_g, scale, shift, gate, weights, final_g, seq_len, final_norm, tm=256):
    tok, d = x2.shape
    per_seq = seq_len // tm
    wg, wu, cw, wd = weights
    row = lambda i: (i, 0)
    bat = lambda i: (i // per_seq, 0, 0)
    vec = pl.BlockSpec((1, 1, d), bat)
    common_specs = [_const_spec((1, d)), vec, vec, vec,
                    _const_spec(wg.shape), _const_spec(wu.shape), _const_spec(cw.shape), _const_spec(wd.shape),
                    _const_spec((1, d))]
    common_args = [norm_g, scale, shift, gate, wg, wu, cw, wd, final_g]
    kw = dict(tm=tm, per_seq=per_seq, final_norm=final_norm)
    if mixer is None:
        body = functools.partial(_ffn1_kernel, **kw)
        specs = [pl.BlockSpec((tm, d), row)] + common_specs
        args = [x2] + common_args
    else:
        ys, ya, wo, g1 = mixer
        body = functools.partial(_ffn0_kernel, **kw)
        specs = [pl.BlockSpec((tm, d), row), pl.BlockSpec((tm, D_SSM), row), pl.BlockSpec((tm, D_ATTN), row),
                 _const_spec(wo.shape), vec] + common_specs
        args = [x2, ys, ya, wo, g1] + common_args
    return pl.pallas_call(
        body,
        grid=(tok // tm,),
        in_specs=specs,
        out_specs=pl.BlockSpec((tm, d), row),
        out_shape=jax.ShapeDtypeStruct((tok, d), F32),
        scratch_shapes=[
            pltpu.VMEM((N_FFN_CHUNKS, tm + FFN_HALO, FFN_CHUNK), F32),
            pltpu.VMEM((N_FFN_CHUNKS, FFN_HALO, FFN_CHUNK), F32),
            pltpu.VMEM((tm, d), F32),
            pltpu.VMEM((tm, d), BF16),
        ],
        compiler_params=_params(1),
        name="conv_ffn_final" if final_norm else "conv_ffn",
    )(*args)


def _conformer_kernel(x_ref, g_ref, sc_ref, sh_ref, gate_ref, win_ref, bin_ref, dw_ref, dwb_ref,
                      lng_ref, lnb_ref, wout_ref, bout_ref, shift_ref, o_ref,
                      aext_sc, ash_sc, xs_sc, conv_sc, h_sc, *, tm, per_seq):
    i = pl.program_id(0)
    d = x_ref.shape[-1]
    bufs = (aext_sc, ash_sc, xs_sc)
    sub = tm // CONV_SUBTILES
    sub_ext = sub + CONV_HALO
    off = CONV_HALO - (CONV_WIDTH - 1)

    rc = 32
    n_chunks = tm // rc

    def stage1_pieces(cur, prev):
        aext_sc, ash_sc, xs_sc = cur

        def project():
            x = x_ref[...]
            xs_sc[...] = x
            h_sc[...] = _norm_mod(x, g_ref[...], sc_ref[0], sh_ref[0]).astype(BF16)
            a = jnp.dot(h_sc[...], win_ref[...], preferred_element_type=F32) + bin_ref[...]
            a = a[:, :d] * _sigmoid(a[:, d:])
            if prev is None:
                aext_sc[0:CONV_HALO, :] = jnp.zeros((CONV_HALO, d), F32)
            else:
                aext_sc[0:CONV_HALO, :] = jnp.where(i % per_seq == 0, 0.0, prev[0][tm:tm + CONV_HALO, :])
            aext_sc[CONV_HALO:, :] = a

        def shifts(s):
            def run():
                ab = aext_sc[s * sub:s * sub + sub_ext, :].astype(BF16)
                for r in range(7):
                    ash_sc[s, r] = jnp.dot(shift_ref[r], ab, preferred_element_type=F32)
            return run

        return [project] + [shifts(s) for s in range(CONV_SUBTILES)]

    def stage2_pieces(cur):
        aext_sc, ash_sc, xs_sc = cur

        def conv(base):
            def run():
                s = base // sub
                acc = [dwb_ref[...]] * (rc // 8)
                for k in range(CONV_WIDTH):
                    r = (off + k) % 8
                    lo = base + off + k - r
                    w8 = dw_ref[k]
                    for j in range(rc // 8):
                        if r == 0:
                            win = aext_sc[lo + 8 * j:lo + 8 * j + 8, :]
                        else:
                            win = ash_sc[s, r - 1, lo - s * sub + 8 * j:lo - s * sub + 8 * j + 8, :]
                        acc[j] = acc[j] + w8 * win
                for j in range(rc // 8):
                    conv_sc[base + 8 * j:base + 8 * j + 8, :] = acc[j]
            return run

        def finish():
            c = conv_sc[...]
            mu = jnp.mean(c, axis=-1, keepdims=True)
            xc = c - mu
            y = xc * lax.rsqrt(jnp.mean(xc * xc, axis=-1, keepdims=True) + EPS)
            y = _silu(y * lng_ref[...] + lnb_ref[...]).astype(BF16)
            out = jnp.dot(y, wout_ref[...], preferred_element_type=F32) + bout_ref[...]
            o_ref[...] = xs_sc[...] + gate_ref[0] * out

        return [conv(base) for base in range(0, tm, rc)] + [finish]

    def run_all(pieces):
        for piece in pieces:
            piece()

    @pl.when(i == 0)
    def _():
        run_all(stage1_pieces(bufs, None) + stage2_pieces(bufs))

    @pl.when(i > 0)
    def _():
        run_all(stage1_pieces(bufs, bufs) + stage2_pieces(bufs))


def _row_shift_matrix(rows, n_shifts):
    t = np.arange(rows)
    m = np.zeros((n_shifts, rows, rows), np.float32)
    for r in range(1, n_shifts + 1):
        keep = t + r < rows
        m[r - 1, t[keep], t[keep] + r] = 1.0
    return m


def _conformer(x2, norm_g, scale, shift, gate, w_in, b_in, dw_w, dw_b, ln_g, ln_b, w_out, b_out, seq_len, tm=256):
    tok, d = x2.shape
    per_seq = seq_len // tm
    sub_ext = tm // CONV_SUBTILES + CONV_HALO
    shift_mat = jnp.asarray(_row_shift_matrix(sub_ext, 7), dtype=BF16)
    row = lambda i: (i, 0)
    vec = pl.BlockSpec((1, 1, d), lambda i: (i // per_seq, 0, 0))
    return pl.pallas_call(
        functools.partial(_conformer_kernel, tm=tm, per_seq=per_seq),
        grid=(tok // tm,),
        in_specs=[pl.BlockSpec((tm, d), row), _const_spec((1, d)), vec, vec, vec,
                  _const_spec((d, 2 * d)), _const_spec((1, 2 * d)), _const_spec((CONV_WIDTH, 8, d)), _const_spec((8, d)),
                  _const_spec((1, d)), _const_spec((1, d)), _const_spec((d, d)), _const_spec((1, d)),
                  _const_spec(shift_mat.shape)],
        out_specs=pl.BlockSpec((tm, d), row),
        out_shape=jax.ShapeDtypeStruct((tok, d), F32),
        scratch_shapes=[pltpu.VMEM((tm + CONV_HALO, d), F32), pltpu.VMEM((CONV_SUBTILES, 7, sub_ext, d), F32),
                        pltpu.VMEM((tm, d), F32), pltpu.VMEM((tm, d), F32), pltpu.VMEM((tm, d), BF16)],
        compiler_params=_params(1),
        name="conformer_conv",
    )(x2, norm_g, scale, shift, gate, w_in.astype(BF16), b_in.reshape(1, -1),
      jnp.broadcast_to(dw_w[:, None, :], (CONV_WIDTH, 8, d)), jnp.broadcast_to(dw_b[None, :], (8, d)),
      ln_g.reshape(1, -1), ln_b.reshape(1, -1), w_out.astype(BF16), b_out.reshape(1, -1), shift_mat)


def kernel(x, c, mod_w, mod_b, norm_g, final_g, ab_w_in, ssm_a_re, ssm_a_im, ssm_log_dt, ssm_b_re, ssm_b_im, ssm_c_re, ssm_c_im, ssm_d, ssm_glu_w, ssm_glu_b, ab_w_out, rel_bias, cm_w_in, cm_b_in, cm_dw_w, cm_dw_b, cm_ln_g, cm_ln_b, cm_w_out, cm_b_out, ffn_w_up, ffn_w_gate, ffn_dw_w, ffn_dw_b, ffn_w_down):
    bsz, seq_len, d = x.shape
    tok = bsz * seq_len
    x2 = x.reshape(tok, d)
    mod = _modulation(c, mod_w, mod_b)
    vecs = [[mod[l, :, i * d:(i + 1) * d].reshape(bsz, 1, d) for i in range(6)] for l in range(2)]
    fin = final_g.reshape(1, d)

    sh1, sc1, g1, sh2, sc2, g2 = vecs[0]
    u, q, k, v = _in_projection(x2, norm_g[0, 0].reshape(1, d), sc1, sh1, ab_w_in[0].astype(BF16), seq_len)
    ops = _s5_prepare(ssm_a_re[0], ssm_a_im[0], ssm_log_dt[0], ssm_b_re[0], ssm_b_im[0], ssm_c_re[0], ssm_c_im[0])
    y_ssm = _s5_mixer(u.reshape(S5_T, bsz, seq_len // S5_T, D_SSM), ops, ssm_d[0], ssm_glu_w[0], ssm_glu_b[0])
    att = lambda a: a.reshape(bsz, seq_len, D_ATTN)
    y_att = _moba_attention(att(q), att(k), att(v), _bias_tiles(rel_bias), rel_bias[REL_BUCKETS - 1])
    w0 = _ffn_weights(ffn_w_up[0], ffn_w_gate[0], ffn_dw_w[0], ffn_dw_b[0], ffn_w_down[0])
    x2 = _conv_ffn(x2, (y_ssm.reshape(tok, D_SSM), y_att.reshape(tok, D_ATTN), ab_w_out[0].astype(BF16), g1),
                   norm_g[0, 1].reshape(1, d), sc2, sh2, g2, w0, fin, seq_len, final_norm=False)

    sh1, sc1, g1, sh2, sc2, g2 = vecs[1]
    x2 = _conformer(x2, norm_g[1, 0].reshape(1, d), sc1, sh1, g1, cm_w_in[0], cm_b_in[0], cm_dw_w[0], cm_dw_b[0],
                    cm_ln_g[0], cm_ln_b[0], cm_w_out[0], cm_b_out[0], seq_len)
    w1 = _ffn_weights(ffn_w_up[1], ffn_w_gate[1], ffn_dw_w[1], ffn_dw_b[1], ffn_w_down[1])
    x2 = _conv_ffn(x2, None, norm_g[1, 1].reshape(1, d), sc2, sh2, g2, w1, fin, seq_len, final_norm=True)
    return x2.reshape(bsz, seq_len, d)
```

```python
import functools
import math

import numpy as np
import jax
import jax.numpy as jnp
from jax import lax
from jax.experimental import pallas as pl
from jax.experimental.pallas import tpu as pltpu

F32 = jnp.float32
BF16 = jnp.bfloat16

D_MODEL = 1024
D_SSM = 512
SSM_GROUP = 16
SSM_GROUPS = 32
SSM_STATE = 64
D_ATTN = 512
HEAD_DIM = 64
N_HEADS = 8
MOBA_BLOCK = 256
MOBA_TOPK = 3
REL_BUCKETS = 32
REL_MAX_DIST = 128
CONV_WIDTH = 31
FFN_HIDDEN = 2816
FFN_CONV_WIDTH = 3
EPS = 1e-6

NEG = -1e30
LOG2E = math.log2(math.e)

V7X_VMEM_BYTES = 64 * 1024 * 1024
VMEM_LIMIT = V7X_VMEM_BYTES - 8 * 1024 * 1024

S5_T = 4
S5_CB = 32
S5_ROW_STRIDE = S5_CB + 8
HALF = 256
GROUPS_PER_HALF = HALF // SSM_GROUP
FFN_CHUNK = 256
N_FFN_CHUNKS = FFN_HIDDEN // FFN_CHUNK
CONV_HALO = 32
FFN_HALO = 8
ATTN_GROUP = 4
PV_ROWS = HEAD_DIM + 16


def _sigmoid(x):
    return 0.5 * jnp.tanh(0.5 * x) + 0.5


def _silu(x):
    return x * _sigmoid(x)


def _gelu_tanh(x):
    c = math.sqrt(2.0 / math.pi)
    return 0.5 * x * (1.0 + jnp.tanh(c * (x + 0.044715 * (x * x * x))))


def _norm_mod(x, g, scale, shift):
    ms = jnp.mean(x * x, axis=-1, keepdims=True)
    y = x * lax.rsqrt(ms + EPS) * g
    return y * (1.0 + scale) + shift


def _params(n_axes, vmem=VMEM_LIMIT, flags=None):
    return pltpu.CompilerParams(dimension_semantics=("arbitrary",) * n_axes, vmem_limit_bytes=vmem, flags=flags)


def _const_spec(shape):
    nd = len(shape)
    return pl.BlockSpec(shape, lambda *_: (0,) * nd, pipeline_mode=pl.Buffered(1))


def _mod_kernel(c_ref, w_ref, b_ref, o_ref):
    c = c_ref[...]
    cs = _silu(c).astype(BF16)
    o_ref[0] = jnp.dot(cs, w_ref[0].astype(BF16), preferred_element_type=F32) + b_ref[0]


def _modulation(c, mod_w, mod_b):
    depth, d, n = mod_w.shape
    bsz = c.shape[0]
    nt = 1536
    return pl.pallas_call(
        _mod_kernel,
        grid=(depth, n // nt),
        in_specs=[
            pl.BlockSpec((bsz, d), lambda l, j: (0, 0)),
            pl.BlockSpec((1, d, nt), lambda l, j: (l, 0, j)),
            pl.BlockSpec((1, 1, nt), lambda l, j: (l, 0, j)),
        ],
        out_specs=pl.BlockSpec((1, bsz, nt), lambda l, j: (l, 0, j)),
        out_shape=jax.ShapeDtypeStruct((depth, bsz, n), F32),
        compiler_params=_params(2),
        name="modulation",
    )(c, mod_w, mod_b.reshape(depth, 1, n))


def _inproj_kernel(x_ref, g_ref, sc_ref, sh_ref, w_ref, u_ref, q_ref, k_ref, v_ref, u_sc):
    h = _norm_mod(x_ref[...], g_ref[...], sc_ref[0], sh_ref[0]).astype(BF16)
    p = jnp.dot(h, w_ref[...], preferred_element_type=F32)
    tm = p.shape[0]
    lanes = D_SSM // 128
    for l in range(lanes):
        u_sc[l] = p[:, l * 128:(l + 1) * 128]
    for s in range(S5_T):
        u_ref[s] = jnp.concatenate([u_sc[l, pl.ds(s, tm // S5_T, stride=S5_T), :] for l in range(lanes)], axis=-1)
    q_ref[...] = (p[:, D_SSM:D_SSM + D_ATTN] * (HEAD_DIM ** -0.5 * LOG2E)).astype(BF16)
    k_ref[...] = p[:, D_SSM + D_ATTN:D_SSM + 2 * D_ATTN].astype(BF16)
    v_ref[...] = p[:, D_SSM + 2 * D_ATTN:].astype(BF16)


def _in_projection(x2, g, scale, shift, w, seq_len, tm=512):
    tok, d = x2.shape
    per_seq = seq_len // tm
    n = w.shape[1]
    row = lambda i: (i, 0)
    bat = lambda i: (i // per_seq, 0, 0)
    return pl.pallas_call(
        _inproj_kernel,
        grid=(tok // tm,),
        in_specs=[
            pl.BlockSpec((tm, d), row),
            _const_spec((1, d)),
            pl.BlockSpec((1, 1, d), bat),
            pl.BlockSpec((1, 1, d), bat),
            _const_spec((d, n)),
        ],
        out_specs=[
            pl.BlockSpec((S5_T, tm // S5_T, D_SSM), lambda i: (0, i, 0)),
            pl.BlockSpec((tm, D_ATTN), row),
            pl.BlockSpec((tm, D_ATTN), row),
            pl.BlockSpec((tm, D_ATTN), row),
        ],
        out_shape=[
            jax.ShapeDtypeStruct((S5_T, tok // S5_T, D_SSM), F32),
            jax.ShapeDtypeStruct((tok, D_ATTN), BF16),
            jax.ShapeDtypeStruct((tok, D_ATTN), BF16),
            jax.ShapeDtypeStruct((tok, D_ATTN), BF16),
        ],
        scratch_shapes=[pltpu.VMEM((D_SSM // 128, tm, 128), F32)],
        compiler_params=_params(1),
        name="in_projection",
    )(x2, g, scale, shift, w)


def _s5_prep_kernel(lre_r, lim_r, ldt_r, lre_c, lim_c, ldt_c, btr, bti, cre, cim, ctr, cti,
                    kt_ref, sre_ref, sim_ref, ore_ref, oim_ref, at_ref):
    def discretise(lre, lim, ldt):
        dt = jnp.exp(ldt)
        mag = jnp.exp(lre * dt)
        return mag * jnp.cos(lim * dt), mag * jnp.sin(lim * dt)

    lre, lim = lre_r[...], lim_r[...]
    ar, ai = discretise(lre, lim, ldt_r[...])
    den = lre * lre + lim * lim
    nr = ar - 1.0
    coef_re = (nr * lre + ai * lim) / den
    coef_im = (ai * lre - nr * lim) / den
    br, bi = btr[...], bti[...]
    zr = coef_re * br - coef_im * bi
    zi = coef_re * bi + coef_im * br
    c_re, c_im = cre[...], cim[...]
    for k in range(S5_T):
        sre_ref[S5_T - 1 - k] = zr
        sim_ref[S5_T - 1 - k] = zi
        for h in range(SSM_GROUP):
            kt_ref[k, h] = jnp.sum(c_re[:, h:h + 1, :] * zr - c_im[:, h:h + 1, :] * zi, axis=-1)
        zr, zi = ar * zr - ai * zi, ar * zi + ai * zr

    acr, aci = discretise(lre_c[...], lim_c[...], ldt_c[...])
    pr, pi = acr, aci
    ct_re, ct_im = ctr[...], cti[...]
    for t in range(S5_T):
        ore_ref[t] = ct_re * pr - ct_im * pi
        oim_ref[t] = -ct_re * pi - ct_im * pr
        pr, pi = acr * pr - aci * pi, acr * pi + aci * pr

    qr, qi = ar, ai
    for _ in range(S5_T - 1):
        qr, qi = ar * qr - ai * qi, ar * qi + ai * qr
    at_ref[0] = qr
    at_ref[1] = qi


def _s5_prepare(a_re, a_im, log_dt, b_re, b_im, c_re, c_im):
    g, p, h, t = SSM_GROUPS, SSM_STATE, SSM_GROUP, S5_T
    ins = [
        a_re.reshape(g, 1, p), a_im.reshape(g, 1, p), log_dt.reshape(g, 1, 1),
        a_re.reshape(g, p, 1), a_im.reshape(g, p, 1), log_dt.reshape(g, 1, 1),
        b_re.transpose(0, 2, 1), b_im.transpose(0, 2, 1), c_re, c_im,
        c_re.transpose(0, 2, 1), c_im.transpose(0, 2, 1),
    ]
    full = lambda s: pl.BlockSpec(s, lambda: (0,) * len(s))
    out_shapes = [(t, h, g, h), (t, g, h, p), (t, g, h, p), (t, g, p, h), (t, g, p, h), (2, g, 1, p)]
    kt, sre, sim, ore, oim, at = pl.pallas_call(
        _s5_prep_kernel,
        in_specs=[full(a.shape) for a in ins],
        out_specs=[full(s) for s in out_shapes],
        out_shape=[jax.ShapeDtypeStruct(s, F32) for s in out_shapes],
        name="s5_prepare",
    )(*ins)

    gh = GROUPS_PER_HALF
    eye = jnp.eye(gh, dtype=F32)
    ktg = kt.transpose(0, 2, 3, 1).reshape(t, 2, gh, h, h)
    toe = jnp.einsum("kfgab,gc->kfgacb", ktg, eye).reshape(t, 2, HALF, HALF)

    def s_tiles(s):
        sg = s.reshape(t, 2, gh, h, p)
        return jnp.einsum("sfgap,gc->sfgacp", sg, eye).reshape(t, 2, HALF, gh * p)

    def o_tiles(o):
        og = o.reshape(t, 2, gh, p, h)
        return jnp.einsum("tfgpb,gc->tfgpcb", og, eye).reshape(t, 2, gh * p, HALF)

    smat = jnp.concatenate([s_tiles(sre), s_tiles(sim)], axis=-1)
    omat = jnp.concatenate([o_tiles(ore), o_tiles(oim)], axis=-2)
    atr = at[0].reshape(2, gh * p)
    ati = at[1].reshape(2, gh * p)
    return toe.astype(BF16), smat.astype(BF16), omat.astype(BF16), atr, ati


def _s5_kernel(u_ref, toe_ref, smat_ref, omat_ref, atr_ref, ati_ref, d_ref, gw_ref, gb_ref,
               y_ref, acc_sc, s_sc, xp_sc, cr_sc, ci_sc, y_sc, *, bsz):
    rows = bsz * S5_CB
    hs = GROUPS_PER_HALF * SSM_STATE

    @pl.when(pl.program_id(0) == 0)
    def _():
        cr_sc[...] = jnp.zeros_like(cr_sc)
        ci_sc[...] = jnp.zeros_like(ci_sc)

    u = [u_ref[s].reshape(rows, D_SSM) for s in range(S5_T)]
    ub = [v.astype(BF16) for v in u]

    def half(f):
        return slice(f * HALF, (f + 1) * HALF)

    def col(t, f):
        return slice(t * D_SSM + f * HALF, t * D_SSM + (f + 1) * HALF)

    lt = 2 * hs // 128
    for f in range(2):
        st = None
        for s in range(S5_T):
            part = jnp.dot(ub[s][:, half(f)], smat_ref[s, f], preferred_element_type=F32)
            st = part if st is None else st + part
        for l in range(lt):
            for b in range(bsz):
                s_sc[f * lt + l, b * S5_ROW_STRIDE:b * S5_ROW_STRIDE + S5_CB, :] = (
                    st[b * S5_CB:(b + 1) * S5_CB, l * 128:(l + 1) * 128])

    for f in range(2):
        for t in range(S5_T):
            acc = None
            for s in range(t + 1):
                part = jnp.dot(ub[s][:, half(f)], toe_ref[t - s, f], preferred_element_type=F32)
                acc = part if acc is None else acc + part
            acc_sc[:, col(t, f)] = acc

    half_lt = lt // 2
    for f in range(2):
        for l in range(half_lt):
            cols = slice(f * hs + l * 128, f * hs + (l + 1) * 128)
            a_r = atr_ref[f:f + 1, l * 128:(l + 1) * 128]
            a_i = ati_ref[f:f + 1, l * 128:(l + 1) * 128]
            xr = cr_sc[:, cols]
            xi = ci_sc[:, cols]
            t_re = f * lt + l
            t_im = f * lt + half_lt + l
            for c in range(S5_CB):
                idx = pl.ds(c, bsz, stride=S5_ROW_STRIDE)
                xp_sc[t_re, idx, :] = xr
                xp_sc[t_im, idx, :] = xi
                sr = s_sc[t_re, idx, :]
                si = s_sc[t_im, idx, :]
                xr, xi = a_r * xr - a_i * xi + sr, a_r * xi + a_i * xr + si
            cr_sc[:, cols] = xr
            ci_sc[:, cols] = xi

    def xp_tile(i):
        return jnp.concatenate([xp_sc[i, b * S5_ROW_STRIDE:b * S5_ROW_STRIDE + S5_CB, :] for b in range(bsz)], axis=0)

    xpb = [jnp.concatenate([xp_tile(f * lt + l) for l in range(lt)], axis=-1).astype(BF16) for f in range(2)]
    gw = gw_ref[...]
    for t in range(S5_T):
        ys = []
        for f in range(2):
            carry = jnp.dot(xpb[f], omat_ref[t, f], preferred_element_type=F32)
            ys.append(acc_sc[:, col(t, f)] + carry + d_ref[:, half(f)] * u[t][:, half(f)])
        y = _gelu_tanh(jnp.concatenate(ys, axis=-1))
        z = jnp.dot(y.astype(BF16), gw, preferred_element_type=F32) + gb_ref[...]
        out = y * _sigmoid(z)
        for b in range(bsz):
            for l in range(D_SSM // 128):
                y_sc[l, pl.ds(b * S5_T * S5_CB + t, S5_CB, stride=S5_T), :] = (
                    out[b * S5_CB:(b + 1) * S5_CB, l * 128:(l + 1) * 128])
    span = S5_T * S5_CB
    for b in range(bsz):
        y_ref[b] = jnp.concatenate([y_sc[l, b * span:(b + 1) * span, :] for l in range(D_SSM // 128)],
                                   axis=-1).astype(BF16)


def _s5_mixer(u, ops, d_skip, glu_w, glu_b):
    _, bsz, nchunk, _ = u.shape
    seq_len = nchunk * S5_T
    toe, smat, omat, atr, ati = ops
    width = S5_T * D_SSM
    rows = bsz * S5_CB
    state_w = 2 * 2 * GROUPS_PER_HALF * SSM_STATE
    return pl.pallas_call(
        functools.partial(_s5_kernel, bsz=bsz),
        grid=(nchunk // S5_CB,),
        in_specs=[
            pl.BlockSpec((S5_T, bsz, S5_CB, D_SSM), lambda i: (0, 0, i, 0)),
            _const_spec(toe.shape), _const_spec(smat.shape), _const_spec(omat.shape),
            _const_spec(atr.shape), _const_spec(ati.shape),
            _const_spec((1, D_SSM)), _const_spec((D_SSM, D_SSM)), _const_spec((1, D_SSM)),
        ],
        out_specs=pl.BlockSpec((bsz, S5_T * S5_CB, D_SSM), lambda i: (0, i, 0)),
        out_shape=jax.ShapeDtypeStruct((bsz, seq_len, D_SSM), BF16),
        scratch_shapes=[
            pltpu.VMEM((rows, width), F32),
            pltpu.VMEM((state_w // 128, bsz * S5_ROW_STRIDE, 128), F32),
            pltpu.VMEM((state_w // 128, bsz * S5_ROW_STRIDE, 128), F32),
            pltpu.VMEM((bsz, state_w // 2), F32),
            pltpu.VMEM((bsz, state_w // 2), F32),
            pltpu.VMEM((D_SSM // 128, rows * S5_T, 128), F32),
        ],
        compiler_params=_params(1),
        name="s5_mixer",
    )(u, toe, smat, omat, atr, ati, d_skip.reshape(1, D_SSM), glu_w.astype(BF16), glu_b.reshape(1, D_SSM))


def _rel_bucket_np(dist):
    n = np.maximum(dist, 0)
    max_exact = REL_BUCKETS // 2
    nf = np.maximum(n, 1).astype(np.float64)
    large = max_exact + (np.log(nf / max_exact) / math.log(REL_MAX_DIST / max_exact)
                         * (REL_BUCKETS - max_exact)).astype(np.int64)
    large = np.minimum(large, REL_BUCKETS - 1)
    return np.where(n < max_exact, n, large).astype(np.int32)


def _bias_bucket_tiles():
    ko = np.arange(MOBA_BLOCK)[:, None]
    qo = np.arange(MOBA_BLOCK)[None, :]
    own = np.where(qo >= ko, _rel_bucket_np(qo - ko), -1)
    prev = _rel_bucket_np(qo - ko + MOBA_BLOCK)
    return np.stack([own, prev]).astype(np.int32)


assert int(_rel_bucket_np(np.arange(MOBA_BLOCK + 1, 8 * MOBA_BLOCK)).min()) == REL_BUCKETS - 1


def _bias_kernel(tab_ref, idx_ref, o_ref):
    h = pl.program_id(0)
    for t in range(2):
        idx = idx_ref[t]
        acc = jnp.full(idx.shape, NEG, F32)
        for b in range(REL_BUCKETS):
            acc = jnp.where(idx == b, tab_ref[h, b] * LOG2E, acc)
        o_ref[0, t] = acc


def _bias_tiles(rel_bias):
    idx = jnp.asarray(_bias_bucket_tiles())
    blk = MOBA_BLOCK
    return pl.pallas_call(
        _bias_kernel,
        grid=(N_HEADS,),
        in_specs=[
            pl.BlockSpec(memory_space=pltpu.SMEM),
            pl.BlockSpec((2, blk, blk), lambda h: (0, 0, 0)),
        ],
        out_specs=pl.BlockSpec((1, 2, blk, blk), lambda h: (h, 0, 0, 0)),
        out_shape=jax.ShapeDtypeStruct((N_HEADS, 2, blk, blk), F32),
        compiler_params=_params(1),
        name="moba_bias_tiles",
    )(rel_bias.T, idx)


def _attn_kernel(far_ref, q_ref, qn_ref, k_ref, v_ref, bias_ref, o_ref,
                 kmean_sc, vt_sc, mfar_sc, msel_sc, rows_sc, s0_sc, s1_sc, gmax0_sc, gmax1_sc, m_sc, acc_sc, *, nb):
    s_bufs = (s0_sc, s1_sc)
    gmax_bufs = (gmax0_sc, gmax1_sc)
    hp = pl.program_id(1)
    qt = pl.program_id(2)
    n_tiles = nb // 2
    blk = MOBA_BLOCK
    qw = 2 * blk
    qlane = lax.broadcasted_iota(jnp.int32, (1, qw), 1)
    lane = lax.broadcasted_iota(jnp.int32, (1, 2 * HEAD_DIM), 1)
    head_mask = [lane < HEAD_DIM, lane >= HEAD_DIM]
    nt = (((1,), (1,)), ((), ()))

    @pl.when(qt == 0)
    def _():
        for j in range(nb):
            kb = k_ref[0, j * blk:(j + 1) * blk, :].astype(F32)
            km = jnp.mean(kb, axis=0, keepdims=True)
            for hh in range(2):
                kmean_sc[hh, j:j + 1, :] = jnp.where(head_mask[hh], km, 0.0)
            vt = v_ref[0, j * blk:(j + 1) * blk, :].astype(F32).T.astype(BF16)
            ones_row = jnp.where(lax.broadcasted_iota(jnp.int32, (PV_ROWS - HEAD_DIM, blk), 0) == 0, 1.0, 0.0)
            for hh in range(2):
                vt_sc[hh, j, 0:HEAD_DIM, :] = vt[hh * HEAD_DIM:(hh + 1) * HEAD_DIM, :]
                vt_sc[hh, j, HEAD_DIM:, :] = ones_row.astype(BF16)

    jidx = lax.broadcasted_iota(jnp.int32, (nb, qw), 0)

    def pv(hh, j, p):
        return jnp.dot(vt_sc[hh, j], p.astype(BF16), preferred_element_type=F32)

    def mask_row(ref, hh, j):
        return jnp.where(j >= 0, ref[hh, pl.ds(jnp.maximum(j, 0), 1), :], NEG)

    class Tile:
        def __init__(self, t, q2):
            self.q2 = q2
            self.top = 2 * t + 1
            self.own = 2 * t + jnp.where(qlane >= blk, 1, 0)
            self.n_groups = self.top // ATTN_GROUP + 1
            self.qm = [jnp.where(head_mask[hh], q2, jnp.zeros_like(q2)) for hh in range(2)]

        def select(self):
            for hh in range(2):
                gate = lax.dot_general(kmean_sc[hh].astype(BF16), self.q2, nt, preferred_element_type=F32)
                rank = jnp.zeros((nb, qw), F32)
                for jp in range(nb):
                    row = gate[jp:jp + 1, :]
                    beats = (row > gate) | ((row == gate) & (jidx > jp))
                    rank = rank + jnp.where(beats & (self.own > jp), 1.0, 0.0)
                sel = (rank < float(MOBA_TOPK)) & (jidx < self.own)
                mfar_sc[hh] = jnp.where(sel, far_ref[2 * hp + hh] * LOG2E, NEG)
                msel_sc[hh] = jnp.where(sel, 0.0, NEG)

        def scores(self, hh, j):
            kb = k_ref[0, pl.ds(pl.multiple_of(j * blk, blk), blk), :]
            return lax.dot_general(kb, self.qm[hh], nt, preferred_element_type=F32)

        def block_ids(self, g):
            js = [self.top - ATTN_GROUP * g - jj for jj in range(ATTN_GROUP)]
            return js, [jnp.maximum(j, 0) for j in js]

        def row_term(self, g, jj, hh, j):
            far = mask_row(mfar_sc, hh, j)
            if jj > 2:
                return far
            second = qlane >= blk
            if jj == 0:
                special = jnp.where(second, 0.0, NEG)
            elif jj == 1:
                special = jnp.where(second, mask_row(msel_sc, hh, j), 0.0)
            else:
                special = jnp.where(second, far, mask_row(msel_sc, hh, j))
            return jnp.where(g == 0, special, far)

    class PhaseA:
        def __init__(self, tile, g, buf, first):
            self.tile, self.g, self.first = tile, g, first
            self.js, self.jc = tile.block_ids(g)
            self.s_buf, self.gmax_buf = s_bufs[buf], gmax_bufs[buf]
            self.gmax = [None, None]

        def block(self, hh, jj):
            s = self.tile.scores(hh, self.jc[jj])
            if self.first and jj < 3:
                tiles = [(None, 0), (0, 1), (1, None)][jj]
                parts = [s[:, h * blk:(h + 1) * blk] if t is None else s[:, h * blk:(h + 1) * blk] + bias_ref[hh, t]
                         for h, t in enumerate(tiles)]
                s = jnp.concatenate(parts, axis=1)
            self.s_buf[hh, jj] = s
            cm = jnp.max(s, axis=0, keepdims=True) + self.tile.row_term(self.g, jj, hh, self.js[jj])
            self.gmax[hh] = cm if self.gmax[hh] is None else jnp.maximum(self.gmax[hh], cm)

        def finish(self):
            for hh in range(2):
                self.gmax_buf[hh] = self.gmax[hh]

    class PhaseB:
        def __init__(self, tile, g, buf, stash_rows=False):
            self.tile, self.g, self.stash_rows = tile, g, stash_rows
            self.js, self.jc = tile.block_ids(g)
            self.s_buf = s_bufs[buf]
            self.m_old = [m_sc[hh] for hh in range(2)]
            self.m_new = [jnp.maximum(self.m_old[hh], gmax_bufs[buf][hh]) for hh in range(2)]
            self.acc = [None, None]
            if stash_rows:
                for hh in range(2):
                    for jj in range(ATTN_GROUP):
                        rows_sc[hh, jj] = tile.row_term(g, jj, hh, self.js[jj])

        def block(self, hh, jj):
            row = rows_sc[hh, jj] if self.stash_rows else self.tile.row_term(self.g, jj, hh, self.js[jj])
            p = jnp.exp2(self.s_buf[hh, jj] - (self.m_new[hh] - row))
            pa = pv(hh, self.jc[jj], p)
            self.acc[hh] = pa if self.acc[hh] is None else self.acc[hh] + pa

        def finish(self):
            for hh in range(2):
                alpha = jnp.exp2(self.m_old[hh] - self.m_new[hh])
                m_sc[hh] = self.m_new[hh]
                acc_sc[hh] = alpha * acc_sc[hh] + self.acc[hh]

    def run(*phases):
        for hh in range(2):
            for jj in range(ATTN_GROUP):
                for ph in phases:
                    ph.block(hh, jj)
        for ph in phases:
            ph.finish()

    cur = Tile(qt, q_ref[0])

    @pl.when(qt == 0)
    def _():
        cur.select()
        run(PhaseA(cur, 0, 0, True))

    for hh in range(2):
        m_sc[hh] = gmax0_sc[hh]
        acc_sc[hh] = jnp.zeros((PV_ROWS, qw), F32)

    def pipelined(g, carry):
        for p in range(2):
            @pl.when(g % 2 == p)
            def _():
                run(PhaseB(cur, g, p), PhaseA(cur, g + 1, 1 - p, False))

        return carry

    lax.fori_loop(0, cur.n_groups - 1, pipelined, 0)

    def write_out():
        out = jnp.concatenate([acc_sc[hh, 0:HEAD_DIM, :] / acc_sc[hh, HEAD_DIM:HEAD_DIM + 1, :] for hh in range(2)],
                              axis=0)
        o_ref[0] = out.T.astype(BF16)

    last = cur.n_groups - 1
    for p in range(2):
        @pl.when((last % 2 == p) & (qt < n_tiles - 1))
        def _():
            nxt = Tile(qt + 1, qn_ref[0])
            tail = PhaseB(cur, last, p, stash_rows=True)
            nxt.select()
            run(tail, PhaseA(nxt, 0, 0, True))
            write_out()

        @pl.when((last % 2 == p) & (qt == n_tiles - 1))
        def _():
            run(PhaseB(cur, last, p))
            write_out()


def _moba_attention(q, k, v, bias_tiles, far_bias):
    bsz, seq_len, _ = q.shape
    blk = MOBA_BLOCK
    nb = seq_len // blk
    pair = 2 * HEAD_DIM
    qw = 2 * blk
    assert nb % 2 == 0
    return pl.pallas_call(
        functools.partial(_attn_kernel, nb=nb),
        grid=(bsz, N_HEADS // 2, nb // 2),
        in_specs=[
            pl.BlockSpec(memory_space=pltpu.SMEM),
            pl.BlockSpec((1, qw, pair), lambda b, h, i: (b, i, h)),
            pl.BlockSpec((1, qw, pair), lambda b, h, i: (b, jnp.minimum(i + 1, nb // 2 - 1), h)),
            pl.BlockSpec((1, seq_len, pair), lambda b, h, i: (b, 0, h)),
            pl.BlockSpec((1, seq_len, pair), lambda b, h, i: (b, 0, h)),
            pl.BlockSpec((2, 2, blk, blk), lambda b, h, i: (h, 0, 0, 0)),
        ],
        out_specs=pl.BlockSpec((1, qw, pair), lambda b, h, i: (b, i, h)),
        out_shape=jax.ShapeDtypeStruct((bsz, seq_len, D_ATTN), BF16),
        scratch_shapes=[
            pltpu.VMEM((2, nb, pair), F32),
            pltpu.VMEM((2, nb, PV_ROWS, blk), BF16),
            pltpu.VMEM((2, nb, qw), F32),
            pltpu.VMEM((2, nb, qw), F32),
            pltpu.VMEM((2, ATTN_GROUP, 1, qw), F32),
            pltpu.VMEM((2, ATTN_GROUP, blk, qw), F32),
            pltpu.VMEM((2, ATTN_GROUP, blk, qw), F32),
            pltpu.VMEM((2, 1, qw), F32),
            pltpu.VMEM((2, 1, qw), F32),
            pltpu.VMEM((2, 1, qw), F32),
            pltpu.VMEM((2, PV_ROWS, qw), F32),
        ],
        compiler_params=_params(3),
        name="moba_attention",
    )(far_bias, q, q, k, v, bias_tiles)


def _ffn_body(x, g_ref, sc_ref, sh_ref, gate_ref, wg_ref, wu_ref, cw_ref, wd_ref, fin_ref, o_ref,
              gext_sc, carry_sc, acc_sc, h_sc, *, tm, per_seq, final_norm):
    @pl.when(pl.program_id(0) % per_seq == 0)
    def _():
        carry_sc[...] = jnp.zeros_like(carry_sc)

    h_sc[...] = _norm_mod(x, g_ref[...], sc_ref[0], sh_ref[0]).astype(BF16)

    def gate_up(j):
        cols = slice(j * FFN_CHUNK, (j + 1) * FFN_CHUNK)
        h = h_sc[...]
        return (jnp.dot(h, wg_ref[:, cols], preferred_element_type=F32),
                jnp.dot(h, wu_ref[:, cols], preferred_element_type=F32))

    nxt = gate_up(0)
    for j in range(N_FFN_CHUNKS):
        cols = slice(j * FFN_CHUNK, (j + 1) * FFN_CHUNK)
        gpre, up = nxt
        if j + 1 < N_FFN_CHUNKS:
            nxt = gate_up(j + 1)
        lane_tiles = FFN_CHUNK // 128
        for l in range(lane_tiles):
            gext_sc[j, l, 0:FFN_HALO, :] = carry_sc[j, :, l * 128:(l + 1) * 128]
            gext_sc[j, l, FFN_HALO:, :] = gpre[:, l * 128:(l + 1) * 128]
        carry_sc[j] = gpre[tm - FFN_HALO:, :]

        def back(n):
            return jnp.concatenate([gext_sc[j, l, FFN_HALO - n:FFN_HALO - n + tm, :] for l in range(lane_tiles)],
                                   axis=-1)

        cw = cw_ref[:, cols]
        conv = cw[0:1] * back(2) + cw[1:2] * back(1) + cw[2:3] * gpre + cw[3:4]
        act = (_silu(conv) * up).astype(BF16)
        part = jnp.dot(act, wd_ref[cols, :], preferred_element_type=F32)
        if j == 0:
            acc_sc[...] = part
        else:
            acc_sc[...] += part
    out = x + gate_ref[0] * acc_sc[...]
    if final_norm:
        ms = jnp.mean(out * out, axis=-1, keepdims=True)
        out = out * lax.rsqrt(ms + EPS) * fin_ref[...]
    o_ref[...] = out


def _ffn0_kernel(x_ref, ys_ref, ya_ref, wo_ref, g1_ref, g_ref, sc_ref, sh_ref, gate_ref,
                 wg_ref, wu_ref, cw_ref, wd_ref, fin_ref, o_ref, gext_sc, carry_sc, acc_sc, h_sc, **kw):
    y = (jnp.dot(ys_ref[...], wo_ref[0:D_SSM, :], preferred_element_type=F32)
         + jnp.dot(ya_ref[...], wo_ref[D_SSM:, :], preferred_element_type=F32))
    x = x_ref[...] + g1_ref[0] * y
    _ffn_body(x, g_ref, sc_ref, sh_ref, gate_ref, wg_ref, wu_ref, cw_ref, wd_ref, fin_ref, o_ref,
              gext_sc, carry_sc, acc_sc, h_sc, **kw)


def _ffn1_kernel(x_ref, g_ref, sc_ref, sh_ref, gate_ref, wg_ref, wu_ref, cw_ref, wd_ref, fin_ref,
                 o_ref, gext_sc, carry_sc, acc_sc, h_sc, **kw):
    _ffn_body(x_ref[...], g_ref, sc_ref, sh_ref, gate_ref, wg_ref, wu_ref, cw_ref, wd_ref, fin_ref, o_ref,
              gext_sc, carry_sc, acc_sc, h_sc, **kw)


def _ffn_weights(w_up, w_gate, dw_w, dw_b, w_down):
    cw = jnp.concatenate([dw_w, dw_b[None, :]], axis=0)
    return w_gate.astype(BF16), w_up.astype(BF16), cw, w_down.astype(BF16)


def _conv_ffn(x2, mixer, norm_g, scale, shift, gate, weights, final_g, seq_len, final_norm, tm=256):
    tok, d = x2.shape
    per_seq = seq_len // tm
    wg, wu, cw, wd = weights
    row = lambda i: (i, 0)
    bat = lambda i: (i // per_seq, 0, 0)
    vec = pl.BlockSpec((1, 1, d), bat)
    common_specs = [_const_spec((1, d)), vec, vec, vec,
                    _const_spec(wg.shape), _const_spec(wu.shape), _const_spec(cw.shape), _const_spec(wd.shape),
                    _const_spec((1, d))]
    common_args = [norm_g, scale, shift, gate, wg, wu, cw, wd, final_g]
    kw = dict(tm=tm, per_seq=per_seq, final_norm=final_norm)
    if mixer is None:
        body = functools.partial(_ffn1_kernel, **kw)
        specs = [pl.BlockSpec((tm, d), row)] + common_specs
        args = [x2] + common_args
    else:
        ys, ya, wo, g1 = mixer
        body = functools.partial(_ffn0_kernel, **kw)
        specs = [pl.BlockSpec((tm, d), row), pl.BlockSpec((tm, D_SSM), row), pl.BlockSpec((tm, D_ATTN), row),
                 _const_spec(wo.shape), vec] + common_specs
        args = [x2, ys, ya, wo, g1] + common_args
    return pl.pallas_call(
        body,
        grid=(tok // tm,),
        in_specs=specs,
        out_specs=pl.BlockSpec((tm, d), row),
        out_shape=jax.ShapeDtypeStruct((tok, d), F32),
        scratch_shapes=[
            pltpu.VMEM((N_FFN_CHUNKS, FFN_CHUNK // 128, tm + FFN_HALO, 128), F32),
            pltpu.VMEM((N_FFN_CHUNKS, FFN_HALO, FFN_CHUNK), F32),
            pltpu.VMEM((tm, d), F32),
            pltpu.VMEM((tm, d), BF16),
        ],
        compiler_params=_params(1),
        name="conv_ffn_final" if final_norm else "conv_ffn",
    )(*args)


def _conformer_kernel(x_ref, g_ref, sc_ref, sh_ref, gate_ref, win_ref, bin_ref, dw_ref, dwb_ref,
                      lng_ref, lnb_ref, wout_ref, bout_ref, o_ref, aext_sc, conv_sc, h_sc, *, tm, per_seq):
    d = x_ref.shape[-1]
    lanes = d // 128
    x = x_ref[...]
    h_sc[...] = _norm_mod(x, g_ref[...], sc_ref[0], sh_ref[0]).astype(BF16)
    a = jnp.dot(h_sc[...], win_ref[...], preferred_element_type=F32) + bin_ref[...]
    a = a[:, :d] * _sigmoid(a[:, d:])

    @pl.when(pl.program_id(0) % per_seq == 0)
    def _():
        aext_sc[:, 0:CONV_HALO, :] = jnp.zeros((lanes, CONV_HALO, 128), F32)

    for l in range(lanes):
        aext_sc[l, CONV_HALO:, :] = a[:, l * 128:(l + 1) * 128]

    rc = 32
    off = CONV_HALO - (CONV_WIDTH - 1)
    for l in range(lanes):
        cols = slice(l * 128, (l + 1) * 128)
        for base in range(0, tm, rc):
            acc = [dwb_ref[:, cols]] * (rc // 8)
            for k in range(CONV_WIDTH):
                w8 = dw_ref[k, :, cols]
                for j in range(rc // 8):
                    lo = base + 8 * j + off + k
                    acc[j] = acc[j] + w8 * aext_sc[l, lo:lo + 8, :]
            for j in range(rc // 8):
                conv_sc[base + 8 * j:base + 8 * j + 8, cols] = acc[j]
        aext_sc[l, 0:CONV_HALO, :] = aext_sc[l, tm:tm + CONV_HALO, :]

    c = conv_sc[...]
    mu = jnp.mean(c, axis=-1, keepdims=True)
    xc = c - mu
    y = xc * lax.rsqrt(jnp.mean(xc * xc, axis=-1, keepdims=True) + EPS)
    y = _silu(y * lng_ref[...] + lnb_ref[...]).astype(BF16)
    out = jnp.dot(y, wout_ref[...], preferred_element_type=F32) + bout_ref[...]
    o_ref[...] = x + gate_ref[0] * out


def _conformer(x2, norm_g, scale, shift, gate, w_in, b_in, dw_w, dw_b, ln_g, ln_b, w_out, b_out, seq_len, tm=256):
    tok, d = x2.shape
    per_seq = seq_len // tm
    row = lambda i: (i, 0)
    vec = pl.BlockSpec((1, 1, d), lambda i: (i // per_seq, 0, 0))
    return pl.pallas_call(
        functools.partial(_conformer_kernel, tm=tm, per_seq=per_seq),
        grid=(tok // tm,),
        in_specs=[pl.BlockSpec((tm, d), row), _const_spec((1, d)), vec, vec, vec,
                  _const_spec((d, 2 * d)), _const_spec((1, 2 * d)), _const_spec((CONV_WIDTH, 8, d)), _const_spec((8, d)),
                  _const_spec((1, d)), _const_spec((1, d)), _const_spec((d, d)), _const_spec((1, d))],
        out_specs=pl.BlockSpec((tm, d), row),
        out_shape=jax.ShapeDtypeStruct((tok, d), F32),
        scratch_shapes=[pltpu.VMEM((d // 128, tm + CONV_HALO, 128), F32), pltpu.VMEM((tm, d), F32),
                        pltpu.VMEM((tm, d), BF16)],
        compiler_params=_params(1),
        name="conformer_conv",
    )(x2, norm_g, scale, shift, gate, w_in.astype(BF16), b_in.reshape(1, -1),
      jnp.broadcast_to(dw_w[:, None, :], (CONV_WIDTH, 8, d)), jnp.broadcast_to(dw_b[None, :], (8, d)),
      ln_g.reshape(1, -1), ln_b.reshape(1, -1), w_out.astype(BF16), b_out.reshape(1, -1))


def kernel(x, c, mod_w, mod_b, norm_g, final_g, ab_w_in, ssm_a_re, ssm_a_im, ssm_log_dt, ssm_b_re, ssm_b_im, ssm_c_re, ssm_c_im, ssm_d, ssm_glu_w, ssm_glu_b, ab_w_out, rel_bias, cm_w_in, cm_b_in, cm_dw_w, cm_dw_b, cm_ln_g, cm_ln_b, cm_w_out, cm_b_out, ffn_w_up, ffn_w_gate, ffn_dw_w, ffn_dw_b, ffn_w_down):
    bsz, seq_len, d = x.shape
    tok = bsz * seq_len
    x2 = x.reshape(tok, d)
    mod = _modulation(c, mod_w, mod_b)
    vecs = [[mod[l, :, i * d:(i + 1) * d].reshape(bsz, 1, d) for i in range(6)] for l in range(2)]
    fin = final_g.reshape(1, d)

    sh1, sc1, g1, sh2, sc2, g2 = vecs[0]
    u, q, k, v = _in_projection(x2, norm_g[0, 0].reshape(1, d), sc1, sh1, ab_w_in[0].astype(BF16), seq_len)
    ops = _s5_prepare(ssm_a_re[0], ssm_a_im[0], ssm_log_dt[0], ssm_b_re[0], ssm_b_im[0], ssm_c_re[0], ssm_c_im[0])
    y_ssm = _s5_mixer(u.reshape(S5_T, bsz, seq_len // S5_T, D_SSM), ops, ssm_d[0], ssm_glu_w[0], ssm_glu_b[0])
    att = lambda a: a.reshape(bsz, seq_len, D_ATTN)
    y_att = _moba_attention(att(q), att(k), att(v), _bias_tiles(rel_bias), rel_bias[REL_BUCKETS - 1])
    w0 = _ffn_weights(ffn_w_up[0], ffn_w_gate[0], ffn_dw_w[0], ffn_dw_b[0], ffn_w_down[0])
    x2 = _conv_ffn(x2, (y_ssm.reshape(tok, D_SSM), y_att.reshape(tok, D_ATTN), ab_w_out[0].astype(BF16), g1),
                   norm_g[0, 1].reshape(1, d), sc2, sh2, g2, w0, fin, seq_len, final_norm=False)

    sh1, sc1, g1, sh2, sc2, g2 = vecs[1]
    x2 = _conformer(x2, norm_g[1, 0].reshape(1, d), sc1, sh1, g1, cm_w_in[0], cm_b_in[0], cm_dw_w[0], cm_dw_b[0],
                    cm_ln_g[0], cm_ln_b[0], cm_w_out[0], cm_b_out[0], seq_len)
    w1 = _ffn_weights(ffn_w_up[1], ffn_w_gate[1], ffn_dw_w[1], ffn_dw_b[1], ffn_w_down[1])
    x2 = _conv_ffn(x2, None, norm_g[1, 1].reshape(1, d), sc2, sh2, g2, w1, fin, seq_len, final_norm=True)
    return x2.reshape(bsz, seq_len, d)
```

```python
import functools
import math

import numpy as np
import jax
import jax.numpy as jnp
from jax import lax
from jax.experimental import pallas as pl
from jax.experimental.pallas import tpu as pltpu

F32 = jnp.float32
BF16 = jnp.bfloat16

D_MODEL = 1024
D_SSM = 512
SSM_GROUP = 16
SSM_GROUPS = 32
SSM_STATE = 64
D_ATTN = 512
HEAD_DIM = 64
N_HEADS = 8
MOBA_BLOCK = 256
MOBA_TOPK = 3
REL_BUCKETS = 32
REL_MAX_DIST = 128
CONV_WIDTH = 31
FFN_HIDDEN = 2816
FFN_CONV_WIDTH = 3
EPS = 1e-6

NEG = -1e30
LOG2E = math.log2(math.e)

V7X_VMEM_BYTES = 64 * 1024 * 1024
VMEM_LIMIT = V7X_VMEM_BYTES - 8 * 1024 * 1024

S5_T = 4
S5_CB = 32
S5_ROW_STRIDE = S5_CB + 8
HALF = 256
S5_QUAD = 4
FFN_CHUNK = 256
N_FFN_CHUNKS = FFN_HIDDEN // FFN_CHUNK
CONV_HALO = 32
CONV_CHUNK = 256
FFN_HALO = 8
ATTN_GROUP = 4
PV_ROWS = HEAD_DIM + 16


def _sigmoid(x):
    return 0.5 * jnp.tanh(0.5 * x) + 0.5


def _silu(x):
    return x * _sigmoid(x)


def _gelu_tanh(x):
    c = math.sqrt(2.0 / math.pi)
    return 0.5 * x * (1.0 + jnp.tanh(c * (x + 0.044715 * (x * x * x))))


def _norm_mod(x, g, scale, shift):
    ms = jnp.mean(x * x, axis=-1, keepdims=True)
    y = x * lax.rsqrt(ms + EPS) * g
    return y * (1.0 + scale) + shift


def _params(n_axes, vmem=VMEM_LIMIT, flags=None):
    return pltpu.CompilerParams(dimension_semantics=("arbitrary",) * n_axes, vmem_limit_bytes=vmem, flags=flags)


def _const_spec(shape):
    nd = len(shape)
    return pl.BlockSpec(shape, lambda *_: (0,) * nd, pipeline_mode=pl.Buffered(1))


def _mod_kernel(c_ref, w_ref, b_ref, o_ref):
    c = c_ref[...]
    cs = _silu(c).astype(BF16)
    o_ref[0] = jnp.dot(cs, w_ref[0].astype(BF16), preferred_element_type=F32) + b_ref[0]


def _modulation(c, mod_w, mod_b):
    depth, d, n = mod_w.shape
    bsz = c.shape[0]
    nt = 1536
    return pl.pallas_call(
        _mod_kernel,
        grid=(depth, n // nt),
        in_specs=[
            pl.BlockSpec((bsz, d), lambda l, j: (0, 0)),
            pl.BlockSpec((1, d, nt), lambda l, j: (l, 0, j)),
            pl.BlockSpec((1, 1, nt), lambda l, j: (l, 0, j)),
        ],
        out_specs=pl.BlockSpec((1, bsz, nt), lambda l, j: (l, 0, j)),
        out_shape=jax.ShapeDtypeStruct((depth, bsz, n), F32),
        compiler_params=_params(2),
        name="modulation",
    )(c, mod_w, mod_b.reshape(depth, 1, n))


def _inproj_kernel(x_ref, g_ref, sc_ref, sh_ref, w_ref, u_ref, q_ref, k_ref, v_ref, u_sc):
    h = _norm_mod(x_ref[...], g_ref[...], sc_ref[0], sh_ref[0]).astype(BF16)
    p = jnp.dot(h, w_ref[...], preferred_element_type=F32)
    tm = p.shape[0]
    lanes = D_SSM // 128
    for l in range(lanes):
        u_sc[l] = p[:, l * 128:(l + 1) * 128]
    for s in range(S5_T):
        u_ref[s] = jnp.concatenate([u_sc[l, pl.ds(s, tm // S5_T, stride=S5_T), :] for l in range(lanes)], axis=-1)
    q_ref[...] = (p[:, D_SSM:D_SSM + D_ATTN] * (HEAD_DIM ** -0.5 * LOG2E)).astype(BF16)
    k_ref[...] = p[:, D_SSM + D_ATTN:D_SSM + 2 * D_ATTN].astype(BF16)
    v_ref[...] = p[:, D_SSM + 2 * D_ATTN:].astype(BF16)


def _in_projection(x2, g, scale, shift, w, seq_len, tm=512):
    tok, d = x2.shape
    per_seq = seq_len // tm
    n = w.shape[1]
    row = lambda i: (i, 0)
    bat = lambda i: (i // per_seq, 0, 0)
    return pl.pallas_call(
        _inproj_kernel,
        grid=(tok // tm,),
        in_specs=[
            pl.BlockSpec((tm, d), row),
            _const_spec((1, d)),
            pl.BlockSpec((1, 1, d), bat),
            pl.BlockSpec((1, 1, d), bat),
            _const_spec((d, n)),
        ],
        out_specs=[
            pl.BlockSpec((S5_T, tm // S5_T, D_SSM), lambda i: (0, i, 0)),
            pl.BlockSpec((tm, D_ATTN), row),
            pl.BlockSpec((tm, D_ATTN), row),
            pl.BlockSpec((tm, D_ATTN), row),
        ],
        out_shape=[
            jax.ShapeDtypeStruct((S5_T, tok // S5_T, D_SSM), F32),
            jax.ShapeDtypeStruct((tok, D_ATTN), BF16),
            jax.ShapeDtypeStruct((tok, D_ATTN), BF16),
            jax.ShapeDtypeStruct((tok, D_ATTN), BF16),
        ],
        scratch_shapes=[pltpu.VMEM((D_SSM // 128, tm, 128), F32)],
        compiler_params=_params(1),
        name="in_projection",
    )(x2, g, scale, shift, w)


def _s5_prep_kernel(lre_r, lim_r, ldt_r, lre_c, lim_c, ldt_c, btr, bti, cre, cim, ctr, cti,
                    kt_ref, sre_ref, sim_ref, ore_ref, oim_ref, at_ref):
    def discretise(lre, lim, ldt):
        dt = jnp.exp(ldt)
        mag = jnp.exp(lre * dt)
        return mag * jnp.cos(lim * dt), mag * jnp.sin(lim * dt)

    lre, lim = lre_r[...], lim_r[...]
    ar, ai = discretise(lre, lim, ldt_r[...])
    den = lre * lre + lim * lim
    nr = ar - 1.0
    coef_re = (nr * lre + ai * lim) / den
    coef_im = (ai * lre - nr * lim) / den
    br, bi = btr[...], bti[...]
    zr = coef_re * br - coef_im * bi
    zi = coef_re * bi + coef_im * br
    c_re, c_im = cre[...], cim[...]
    for k in range(S5_T):
        sre_ref[S5_T - 1 - k] = zr
        sim_ref[S5_T - 1 - k] = zi
        for h in range(SSM_GROUP):
            kt_ref[k, h] = jnp.sum(c_re[:, h:h + 1, :] * zr - c_im[:, h:h + 1, :] * zi, axis=-1)
        zr, zi = ar * zr - ai * zi, ar * zi + ai * zr

    acr, aci = discretise(lre_c[...], lim_c[...], ldt_c[...])
    pr, pi = acr, aci
    ct_re, ct_im = ctr[...], cti[...]
    for t in range(S5_T):
        ore_ref[t] = ct_re * pr - ct_im * pi
        oim_ref[t] = -ct_re * pi - ct_im * pr
        pr, pi = acr * pr - aci * pi, acr * pi + aci * pr

    qr, qi = ar, ai
    for _ in range(S5_T - 1):
        qr, qi = ar * qr - ai * qi, ar * qi + ai * qr
    at_ref[0] = qr
    at_ref[1] = qi


def _s5_prepare(a_re, a_im, log_dt, b_re, b_im, c_re, c_im):
    g, p, h, t = SSM_GROUPS, SSM_STATE, SSM_GROUP, S5_T
    ins = [
        a_re.reshape(g, 1, p), a_im.reshape(g, 1, p), log_dt.reshape(g, 1, 1),
        a_re.reshape(g, p, 1), a_im.reshape(g, p, 1), log_dt.reshape(g, 1, 1),
        b_re.transpose(0, 2, 1), b_im.transpose(0, 2, 1), c_re, c_im,
        c_re.transpose(0, 2, 1), c_im.transpose(0, 2, 1),
    ]
    full = lambda s: pl.BlockSpec(s, lambda: (0,) * len(s))
    out_shapes = [(t, h, g, h), (t, g, h, p), (t, g, h, p), (t, g, p, h), (t, g, p, h), (2, g, 1, p)]
    kt, sre, sim, ore, oim, at = pl.pallas_call(
        _s5_prep_kernel,
        in_specs=[full(a.shape) for a in ins],
        out_specs=[full(s) for s in out_shapes],
        out_shape=[jax.ShapeDtypeStruct(s, F32) for s in out_shapes],
        name="s5_prepare",
    )(*ins)

    q4 = S5_QUAD
    nq = g // q4
    eye = jnp.eye(q4, dtype=F32)
    ktg = kt.transpose(0, 2, 3, 1).reshape(t, nq, q4, h, h)
    steps = jnp.arange(t)
    lag_is = ((steps[None, :] - steps[:, None])[None] == steps[:, None, None]).astype(F32)
    toe = jnp.einsum("kst,kqaxy,ac->qsaxtcy", lag_is, ktg, eye).reshape(nq, t * q4 * h, t * q4 * h)

    def s_quads(s):
        return jnp.einsum("sqaxp,ac->qsaxcp", s.reshape(t, nq, q4, h, p), eye).reshape(nq, t * q4 * h, q4 * p)

    def o_quads(o):
        return jnp.einsum("tqapy,ac->qaptcy", o.reshape(t, nq, q4, p, h), eye).reshape(nq, q4 * p, t * q4 * h)

    smat = jnp.concatenate([s_quads(sre), s_quads(sim)], axis=-1)
    omat = jnp.concatenate([o_quads(ore), o_quads(oim)], axis=-2)
    atr = at[0].reshape(1, g * p)
    ati = at[1].reshape(1, g * p)
    return toe.astype(BF16), smat.astype(BF16), omat.astype(BF16), atr, ati


def _s5_kernel(u_ref, toe_ref, smat_ref, omat_ref, atr_ref, ati_ref, d_ref, gw_ref, gb_ref,
               y_ref, yq_sc, s_sc, xp_sc, cr_sc, ci_sc, y_sc, *, bsz):
    rows = bsz * S5_CB
    nq = SSM_GROUPS // S5_QUAD
    qch = S5_QUAD * SSM_GROUP
    qst = S5_QUAD * SSM_STATE

    @pl.when(pl.program_id(0) == 0)
    def _():
        cr_sc[...] = jnp.zeros_like(cr_sc)
        ci_sc[...] = jnp.zeros_like(ci_sc)

    u = [u_ref[s].reshape(rows, D_SSM) for s in range(S5_T)]
    xq = [jnp.concatenate([u[s][:, q * qch:(q + 1) * qch] for s in range(S5_T)], axis=-1).astype(BF16)
          for q in range(nq)]

    lt = 2 * qst // 128
    for q in range(nq):
        st = jnp.dot(xq[q], smat_ref[q], preferred_element_type=F32)
        for l in range(lt):
            for b in range(bsz):
                s_sc[q * lt + l, b * S5_ROW_STRIDE:b * S5_ROW_STRIDE + S5_CB, :] = (
                    st[b * S5_CB:(b + 1) * S5_CB, l * 128:(l + 1) * 128])

    half_lt = lt // 2
    for q in range(nq):
        for l in range(half_lt):
            cols = slice(q * qst + l * 128, q * qst + (l + 1) * 128)
            a_r = atr_ref[:, cols]
            a_i = ati_ref[:, cols]
            xr = cr_sc[:, cols]
            xi = ci_sc[:, cols]
            t_re = q * lt + l
            t_im = q * lt + half_lt + l
            for c in range(S5_CB):
                idx = pl.ds(c, bsz, stride=S5_ROW_STRIDE)
                xp_sc[t_re, idx, :] = xr
                xp_sc[t_im, idx, :] = xi
                sr = s_sc[t_re, idx, :]
                si = s_sc[t_im, idx, :]
                xr, xi = a_r * xr - a_i * xi + sr, a_r * xi + a_i * xr + si
            cr_sc[:, cols] = xr
            ci_sc[:, cols] = xi

    def xp_tile(i):
        return jnp.concatenate([xp_sc[i, b * S5_ROW_STRIDE:b * S5_ROW_STRIDE + S5_CB, :] for b in range(bsz)], axis=0)

    for q in range(nq):
        xpq = jnp.concatenate([xp_tile(q * lt + l) for l in range(lt)], axis=-1).astype(BF16)
        yq_sc[q] = (jnp.dot(xq[q], toe_ref[q], preferred_element_type=F32)
                    + jnp.dot(xpq, omat_ref[q], preferred_element_type=F32))

    gw = gw_ref[...]
    for t in range(S5_T):
        yt = jnp.concatenate([yq_sc[q, :, t * qch:(t + 1) * qch] for q in range(nq)], axis=-1)
        y = _gelu_tanh(yt + d_ref[...] * u[t])
        z = jnp.dot(y.astype(BF16), gw, preferred_element_type=F32) + gb_ref[...]
        out = y * _sigmoid(z)
        for b in range(bsz):
            for l in range(D_SSM // 128):
                y_sc[l, pl.ds(b * S5_T * S5_CB + t, S5_CB, stride=S5_T), :] = (
                    out[b * S5_CB:(b + 1) * S5_CB, l * 128:(l + 1) * 128])
    span = S5_T * S5_CB
    for b in range(bsz):
        y_ref[b] = jnp.concatenate([y_sc[l, b * span:(b + 1) * span, :] for l in range(D_SSM // 128)],
                                   axis=-1).astype(BF16)


def _s5_mixer(u, ops, d_skip, glu_w, glu_b):
    _, bsz, nchunk, _ = u.shape
    seq_len = nchunk * S5_T
    toe, smat, omat, atr, ati = ops
    rows = bsz * S5_CB
    state_w = 2 * SSM_GROUPS * SSM_STATE
    assert S5_T * S5_QUAD * SSM_GROUP == HALF and S5_QUAD * SSM_STATE == HALF
    return pl.pallas_call(
        functools.partial(_s5_kernel, bsz=bsz),
        grid=(nchunk // S5_CB,),
        in_specs=[
            pl.BlockSpec((S5_T, bsz, S5_CB, D_SSM), lambda i: (0, 0, i, 0)),
            _const_spec(toe.shape), _const_spec(smat.shape), _const_spec(omat.shape),
            _const_spec(atr.shape), _const_spec(ati.shape),
            _const_spec((1, D_SSM)), _const_spec((D_SSM, D_SSM)), _const_spec((1, D_SSM)),
        ],
        out_specs=pl.BlockSpec((bsz, S5_T * S5_CB, D_SSM), lambda i: (0, i, 0)),
        out_shape=jax.ShapeDtypeStruct((bsz, seq_len, D_SSM), BF16),
        scratch_shapes=[
            pltpu.VMEM((SSM_GROUPS // S5_QUAD, rows, HALF), F32),
            pltpu.VMEM((state_w // 128, bsz * S5_ROW_STRIDE, 128), F32),
            pltpu.VMEM((state_w // 128, bsz * S5_ROW_STRIDE, 128), F32),
            pltpu.VMEM((bsz, state_w // 2), F32),
            pltpu.VMEM((bsz, state_w // 2), F32),
            pltpu.VMEM((D_SSM // 128, rows * S5_T, 128), F32),
        ],
        compiler_params=_params(1),
        name="s5_mixer",
    )(u, toe, smat, omat, atr, ati, d_skip.reshape(1, D_SSM), glu_w.astype(BF16), glu_b.reshape(1, D_SSM))


def _rel_bucket_np(dist):
    n = np.maximum(dist, 0)
    max_exact = REL_BUCKETS // 2
    nf = np.maximum(n, 1).astype(np.float64)
    large = max_exact + (np.log(nf / max_exact) / math.log(REL_MAX_DIST / max_exact)
                         * (REL_BUCKETS - max_exact)).astype(np.int64)
    large = np.minimum(large, REL_BUCKETS - 1)
    return np.where(n < max_exact, n, large).astype(np.int32)


def _bias_bucket_tiles():
    ko = np.arange(MOBA_BLOCK)[:, None]
    qo = np.arange(MOBA_BLOCK)[None, :]
    own = np.where(qo >= ko, _rel_bucket_np(qo - ko), -1)
    prev = _rel_bucket_np(qo - ko + MOBA_BLOCK)
    return np.stack([own, prev]).astype(np.int32)


assert int(_rel_bucket_np(np.arange(MOBA_BLOCK + 1, 8 * MOBA_BLOCK)).min()) == REL_BUCKETS - 1


def _bias_kernel(tab_ref, idx_ref, o_ref):
    h = pl.program_id(0)
    for t in range(2):
        idx = idx_ref[t]
        acc = jnp.full(idx.shape, NEG, F32)
        for b in range(REL_BUCKETS):
            acc = jnp.where(idx == b, tab_ref[h, b] * LOG2E, acc)
        o_ref[0, t] = acc


def _bias_tiles(rel_bias):
    idx = jnp.asarray(_bias_bucket_tiles())
    blk = MOBA_BLOCK
    return pl.pallas_call(
        _bias_kernel,
        grid=(N_HEADS,),
        in_specs=[
            pl.BlockSpec(memory_space=pltpu.SMEM),
            pl.BlockSpec((2, blk, blk), lambda h: (0, 0, 0)),
        ],
        out_specs=pl.BlockSpec((1, 2, blk, blk), lambda h: (h, 0, 0, 0)),
        out_shape=jax.ShapeDtypeStruct((N_HEADS, 2, blk, blk), F32),
        compiler_params=_params(1),
        name="moba_bias_tiles",
    )(rel_bias.T, idx)


def _attn_kernel(far_ref, q_ref, qn_ref, k_ref, v_ref, bias_ref, o_ref,
                 kmean_sc, vt_sc, mfar_sc, msel_sc, rows_sc, s0_sc, s1_sc, gmax0_sc, gmax1_sc, m_sc, acc_sc, *, nb):
    s_bufs = (s0_sc, s1_sc)
    gmax_bufs = (gmax0_sc, gmax1_sc)
    hp = pl.program_id(1)
    qt = pl.program_id(2)
    n_tiles = nb // 2
    blk = MOBA_BLOCK
    qw = 2 * blk
    qlane = lax.broadcasted_iota(jnp.int32, (1, qw), 1)
    lane = lax.broadcasted_iota(jnp.int32, (1, 2 * HEAD_DIM), 1)
    head_mask = [lane < HEAD_DIM, lane >= HEAD_DIM]
    nt = (((1,), (1,)), ((), ()))

    @pl.when(qt == 0)
    def _():
        for j in range(nb):
            kb = k_ref[0, j * blk:(j + 1) * blk, :].astype(F32)
            km = jnp.mean(kb, axis=0, keepdims=True)
            for hh in range(2):
                kmean_sc[hh, j:j + 1, :] = jnp.where(head_mask[hh], km, 0.0)
            vt = v_ref[0, j * blk:(j + 1) * blk, :].astype(F32).T.astype(BF16)
            ones_row = jnp.where(lax.broadcasted_iota(jnp.int32, (PV_ROWS - HEAD_DIM, blk), 0) == 0, 1.0, 0.0)
            for hh in range(2):
                vt_sc[hh, j, 0:HEAD_DIM, :] = vt[hh * HEAD_DIM:(hh + 1) * HEAD_DIM, :]
                vt_sc[hh, j, HEAD_DIM:, :] = ones_row.astype(BF16)

    jidx = lax.broadcasted_iota(jnp.int32, (nb, qw), 0)

    def pv(hh, j, p):
        return jnp.dot(vt_sc[hh, j], p.astype(BF16), preferred_element_type=F32)

    def mask_row(ref, hh, j):
        return jnp.where(j >= 0, ref[hh, pl.ds(jnp.maximum(j, 0), 1), :], NEG)

    class Tile:
        def __init__(self, t, q2):
            self.q2 = q2
            self.top = 2 * t + 1
            self.own = 2 * t + jnp.where(qlane >= blk, 1, 0)
            self.n_groups = self.top // ATTN_GROUP + 1
            self.qm = [jnp.where(head_mask[hh], q2, jnp.zeros_like(q2)) for hh in range(2)]

        def select(self):
            for hh in range(2):
                gate = lax.dot_general(kmean_sc[hh].astype(BF16), self.q2, nt, preferred_element_type=F32)
                rank = jnp.zeros((nb, qw), F32)
                for jp in range(nb):
                    row = gate[jp:jp + 1, :]
                    beats = (row > gate) | ((row == gate) & (jidx > jp))
                    rank = rank + jnp.where(beats & (self.own > jp), 1.0, 0.0)
                sel = (rank < float(MOBA_TOPK)) & (jidx < self.own)
                mfar_sc[hh] = jnp.where(sel, far_ref[2 * hp + hh] * LOG2E, NEG)
                msel_sc[hh] = jnp.where(sel, 0.0, NEG)

        def scores(self, hh, j):
            kb = k_ref[0, pl.ds(pl.multiple_of(j * blk, blk), blk), :]
            return lax.dot_general(kb, self.qm[hh], nt, preferred_element_type=F32)

        def block_ids(self, g):
            js = [self.top - ATTN_GROUP * g - jj for jj in range(ATTN_GROUP)]
            return js, [jnp.maximum(j, 0) for j in js]

        def row_term(self, g, jj, hh, j):
            far = mask_row(mfar_sc, hh, j)
            if jj > 2:
                return far
            second = qlane >= blk
            if jj == 0:
                special = jnp.where(second, 0.0, NEG)
            elif jj == 1:
                special = jnp.where(second, mask_row(msel_sc, hh, j), 0.0)
            else:
                special = jnp.where(second, far, mask_row(msel_sc, hh, j))
            return jnp.where(g == 0, special, far)

    class PhaseA:
        def __init__(self, tile, g, buf, first):
            self.tile, self.g, self.first = tile, g, first
            self.js, self.jc = tile.block_ids(g)
            self.s_buf, self.gmax_buf = s_bufs[buf], gmax_bufs[buf]
            self.gmax = [None, None]

        def block(self, hh, jj):
            s = self.tile.scores(hh, self.jc[jj])
            if self.first and jj < 3:
                tiles = [(None, 0), (0, 1), (1, None)][jj]
                parts = [s[:, h * blk:(h + 1) * blk] if t is None else s[:, h * blk:(h + 1) * blk] + bias_ref[hh, t]
                         for h, t in enumerate(tiles)]
                s = jnp.concatenate(parts, axis=1)
            self.s_buf[hh, jj] = s
            cm = jnp.max(s, axis=0, keepdims=True) + self.tile.row_term(self.g, jj, hh, self.js[jj])
            self.gmax[hh] = cm if self.gmax[hh] is None else jnp.maximum(self.gmax[hh], cm)

        def finish(self):
            for hh in range(2):
                self.gmax_buf[hh] = self.gmax[hh]

    class PhaseB:
        def __init__(self, tile, g, buf, stash_rows=False):
            self.tile, self.g, self.stash_rows = tile, g, stash_rows
            self.js, self.jc = tile.block_ids(g)
            self.s_buf = s_bufs[buf]
            self.m_old = [m_sc[hh] for hh in range(2)]
            self.m_new = [jnp.maximum(self.m_old[hh], gmax_bufs[buf][hh]) for hh in range(2)]
            self.acc = [None, None]
            if stash_rows:
                for hh in range(2):
                    for jj in range(ATTN_GROUP):
                        rows_sc[hh, jj] = tile.row_term(g, jj, hh, self.js[jj])

        def block(self, hh, jj):
            row = rows_sc[hh, jj] if self.stash_rows else self.tile.row_term(self.g, jj, hh, self.js[jj])
            p = jnp.exp2(self.s_buf[hh, jj] - (self.m_new[hh] - row))
            pa = pv(hh, self.jc[jj], p)
            self.acc[hh] = pa if self.acc[hh] is None else self.acc[hh] + pa

        def finish(self):
            for hh in range(2):
                alpha = jnp.exp2(self.m_old[hh] - self.m_new[hh])
                m_sc[hh] = self.m_new[hh]
                acc_sc[hh] = alpha * acc_sc[hh] + self.acc[hh]

    def run(*phases):
        for hh in range(2):
            for jj in range(ATTN_GROUP):
                for ph in phases:
                    ph.block(hh, jj)
        for ph in phases:
            ph.finish()

    cur = Tile(qt, q_ref[0])

    @pl.when(qt == 0)
    def _():
        cur.select()
        run(PhaseA(cur, 0, 0, True))

    for hh in range(2):
        m_sc[hh] = gmax0_sc[hh]
        acc_sc[hh] = jnp.zeros((PV_ROWS, qw), F32)

    def pipelined(g, carry):
        for p in range(2):
            @pl.when(g % 2 == p)
            def _():
                run(PhaseB(cur, g, p), PhaseA(cur, g + 1, 1 - p, False))

        return carry

    lax.fori_loop(0, cur.n_groups - 1, pipelined, 0)

    def write_out():
        out = jnp.concatenate([acc_sc[hh, 0:HEAD_DIM, :] / acc_sc[hh, HEAD_DIM:HEAD_DIM + 1, :] for hh in range(2)],
                              axis=0)
        o_ref[0] = out.T.astype(BF16)

    last = cur.n_groups - 1
    for p in range(2):
        @pl.when((last % 2 == p) & (qt < n_tiles - 1))
        def _():
            nxt = Tile(qt + 1, qn_ref[0])
            tail = PhaseB(cur, last, p, stash_rows=True)
            nxt.select()
            run(tail, PhaseA(nxt, 0, 0, True))
            write_out()

        @pl.when((last % 2 == p) & (qt == n_tiles - 1))
        def _():
            run(PhaseB(cur, last, p))
            write_out()


def _moba_attention(q, k, v, bias_tiles, far_bias):
    bsz, seq_len, _ = q.shape
    blk = MOBA_BLOCK
    nb = seq_len // blk
    pair = 2 * HEAD_DIM
    qw = 2 * blk
    assert nb % 2 == 0
    return pl.pallas_call(
        functools.partial(_attn_kernel, nb=nb),
        grid=(bsz, N_HEADS // 2, nb // 2),
        in_specs=[
            pl.BlockSpec(memory_space=pltpu.SMEM),
            pl.BlockSpec((1, qw, pair), lambda b, h, i: (b, i, h)),
            pl.BlockSpec((1, qw, pair), lambda b, h, i: (b, jnp.minimum(i + 1, nb // 2 - 1), h)),
            pl.BlockSpec((1, seq_len, pair), lambda b, h, i: (b, 0, h)),
            pl.BlockSpec((1, seq_len, pair), lambda b, h, i: (b, 0, h)),
            pl.BlockSpec((2, 2, blk, blk), lambda b, h, i: (h, 0, 0, 0)),
        ],
        out_specs=pl.BlockSpec((1, qw, pair), lambda b, h, i: (b, i, h)),
        out_shape=jax.ShapeDtypeStruct((bsz, seq_len, D_ATTN), BF16),
        scratch_shapes=[
            pltpu.VMEM((2, nb, pair), F32),
            pltpu.VMEM((2, nb, PV_ROWS, blk), BF16),
            pltpu.VMEM((2, nb, qw), F32),
            pltpu.VMEM((2, nb, qw), F32),
            pltpu.VMEM((2, ATTN_GROUP, 1, qw), F32),
            pltpu.VMEM((2, ATTN_GROUP, blk, qw), F32),
            pltpu.VMEM((2, ATTN_GROUP, blk, qw), F32),
            pltpu.VMEM((2, 1, qw), F32),
            pltpu.VMEM((2, 1, qw), F32),
            pltpu.VMEM((2, 1, qw), F32),
            pltpu.VMEM((2, PV_ROWS, qw), F32),
        ],
        compiler_params=_params(3),
        name="moba_attention",
    )(far_bias, q, q, k, v, bias_tiles)


def _ffn_body(x, g_ref, sc_ref, sh_ref, gate_ref, wg_ref, wu_ref, cw_ref, wd_ref, fin_ref, o_ref,
              gext_sc, carry_sc, acc_sc, h_sc, *, tm, per_seq, final_norm):
    @pl.when(pl.program_id(0) % per_seq == 0)
    def _():
        carry_sc[...] = jnp.zeros_like(carry_sc)

    h_sc[...] = _norm_mod(x, g_ref[...], sc_ref[0], sh_ref[0]).astype(BF16)

    def gate_up(j):
        cols = slice(j * FFN_CHUNK, (j + 1) * FFN_CHUNK)
        h = h_sc[...]
        return (jnp.dot(h, wg_ref[:, cols], preferred_element_type=F32),
                jnp.dot(h, wu_ref[:, cols], preferred_element_type=F32))

    nxt = gate_up(0)
    for j in range(N_FFN_CHUNKS):
        cols = slice(j * FFN_CHUNK, (j + 1) * FFN_CHUNK)
        gpre, up = nxt
        if j + 1 < N_FFN_CHUNKS:
            nxt = gate_up(j + 1)
        lane_tiles = FFN_CHUNK // 128
        for l in range(lane_tiles):
            gext_sc[j, l, 0:FFN_HALO, :] = carry_sc[j, :, l * 128:(l + 1) * 128]
            gext_sc[j, l, FFN_HALO:, :] = gpre[:, l * 128:(l + 1) * 128]
        carry_sc[j] = gpre[tm - FFN_HALO:, :]

        def back(n):
            return jnp.concatenate([gext_sc[j, l, FFN_HALO - n:FFN_HALO - n + tm, :] for l in range(lane_tiles)],
                                   axis=-1)

        cw = cw_ref[:, cols]
        conv = cw[0:1] * back(2) + cw[1:2] * back(1) + cw[2:3] * gpre + cw[3:4]
        act = (_silu(conv) * up).astype(BF16)
        part = jnp.dot(act, wd_ref[cols, :], preferred_element_type=F32)
        if j == 0:
            acc_sc[...] = part
        else:
            acc_sc[...] += part
    out = x + gate_ref[0] * acc_sc[...]
    if final_norm:
        ms = jnp.mean(out * out, axis=-1, keepdims=True)
        out = out * lax.rsqrt(ms + EPS) * fin_ref[...]
    o_ref[...] = out


def _ffn0_kernel(x_ref, ys_ref, ya_ref, wo_ref, g1_ref, g_ref, sc_ref, sh_ref, gate_ref,
                 wg_ref, wu_ref, cw_ref, wd_ref, fin_ref, o_ref, gext_sc, carry_sc, acc_sc, h_sc, **kw):
    y = (jnp.dot(ys_ref[...], wo_ref[0:D_SSM, :], preferred_element_type=F32)
         + jnp.dot(ya_ref[...], wo_ref[D_SSM:, :], preferred_element_type=F32))
    x = x_ref[...] + g1_ref[0] * y
    _ffn_body(x, g_ref, sc_ref, sh_ref, gate_ref, wg_ref, wu_ref, cw_ref, wd_ref, fin_ref, o_ref,
              gext_sc, carry_sc, acc_sc, h_sc, **kw)


def _ffn1_kernel(x_ref, g_ref, sc_ref, sh_ref, gate_ref, wg_ref, wu_ref, cw_ref, wd_ref, fin_ref,
                 o_ref, gext_sc, carry_sc, acc_sc, h_sc, **kw):
    _ffn_body(x_ref[...], g_ref, sc_ref, sh_ref, gate_ref, wg_ref, wu_ref, cw_ref, wd_ref, fin_ref, o_ref,
              gext_sc, carry_sc, acc_sc, h_sc, **kw)


def _ffn_weights(w_up, w_gate, dw_w, dw_b, w_down):
    cw = jnp.concatenate([dw_w, dw_b[None, :]], axis=0)
    return w_gate.astype(BF16), w_up.astype(BF16), cw, w_down.astype(BF16)


def _conv_ffn(x2, mixer, norm_g, scale, shift, gate, weights, final_g, seq_len, final_norm, tm=256):
    tok, d = x2.shape
    per_seq = seq_len // tm
    wg, wu, cw, wd = weights
    row = lambda i: (i, 0)
    bat = lambda i: (i // per_seq, 0, 0)
    vec = pl.BlockSpec((1, 1, d), bat)
    common_specs = [_const_spec((1, d)), vec, vec, vec,
                    _const_spec(wg.shape), _const_spec(wu.shape), _const_spec(cw.shape), _const_spec(wd.shape),
                    _const_spec((1, d))]
    common_args = [norm_g, scale, shift, gate, wg, wu, cw, wd, final_g]
    kw = dict(tm=tm, per_seq=per_seq, final_norm=final_norm)
    if mixer is None:
        body = functools.partial(_ffn1_kernel, **kw)
        specs = [pl.BlockSpec((tm, d), row)] + common_specs
        args = [x2] + common_args
    else:
        ys, ya, wo, g1 = mixer
        body = functools.partial(_ffn0_kernel, **kw)
        specs = [pl.BlockSpec((tm, d), row), pl.BlockSpec((tm, D_SSM), row), pl.BlockSpec((tm, D_ATTN), row),
                 _const_spec(wo.shape), vec] + common_specs
        args = [x2, ys, ya, wo, g1] + common_args
    return pl.pallas_call(
        body,
        grid=(tok // tm,),
        in_specs=specs,
        out_specs=pl.BlockSpec((tm, d), row),
        out_shape=jax.ShapeDtypeStruct((tok, d), F32),
        scratch_shapes=[
            pltpu.VMEM((N_FFN_CHUNKS, FFN_CHUNK // 128, tm + FFN_HALO, 128), F32),
            pltpu.VMEM((N_FFN_CHUNKS, FFN_HALO, FFN_CHUNK), F32),
            pltpu.VMEM((tm, d), F32),
            pltpu.VMEM((tm, d), BF16),
        ],
        compiler_params=_params(1),
        name="conv_ffn_final" if final_norm else "conv_ffn",
    )(*args)


def _conformer_kernel(x_ref, g_ref, sc_ref, sh_ref, gate_ref, win_ref, bin_ref, dw_ref, dwb_ref,
                      lng_ref, lnb_ref, wout_ref, bout_ref, o_ref, aext_sc, conv_sc, h_sc, *, tm, per_seq):
    d = x_ref.shape[-1]
    lanes = d // 128
    x = x_ref[...]
    h_sc[...] = _norm_mod(x, g_ref[...], sc_ref[0], sh_ref[0]).astype(BF16)

    @pl.when(pl.program_id(0) % per_seq == 0)
    def _():
        aext_sc[:, 0:CONV_HALO, :] = jnp.zeros((lanes, CONV_HALO, 128), F32)

    rc = 32
    off = CONV_HALO - (CONV_WIDTH - 1)
    for c0 in range(0, d, CONV_CHUNK):
        cc = slice(c0, c0 + CONV_CHUNK)
        gc = slice(d + c0, d + c0 + CONV_CHUNK)
        h = h_sc[...]
        a = ((jnp.dot(h, win_ref[:, cc], preferred_element_type=F32) + bin_ref[:, cc])
             * _sigmoid(jnp.dot(h, win_ref[:, gc], preferred_element_type=F32) + bin_ref[:, gc]))
        for l in range(c0 // 128, (c0 + CONV_CHUNK) // 128):
            cols = slice(l * 128, (l + 1) * 128)
            aext_sc[l, CONV_HALO:, :] = a[:, l * 128 - c0:(l + 1) * 128 - c0]
            for base in range(0, tm, rc):
                acc = [dwb_ref[:, cols]] * (rc // 8)
                for k in range(CONV_WIDTH):
                    w8 = dw_ref[k, :, cols]
                    for j in range(rc // 8):
                        lo = base + 8 * j + off + k
                        acc[j] = acc[j] + w8 * aext_sc[l, lo:lo + 8, :]
                for j in range(rc // 8):
                    conv_sc[base + 8 * j:base + 8 * j + 8, cols] = acc[j]
            aext_sc[l, 0:CONV_HALO, :] = aext_sc[l, tm:tm + CONV_HALO, :]

    c = conv_sc[...]
    mu = jnp.mean(c, axis=-1, keepdims=True)
    xc = c - mu
    y = xc * lax.rsqrt(jnp.mean(xc * xc, axis=-1, keepdims=True) + EPS)
    y = _silu(y * lng_ref[...] + lnb_ref[...]).astype(BF16)
    out = jnp.dot(y, wout_ref[...], preferred_element_type=F32) + bout_ref[...]
    o_ref[...] = x + gate_ref[0] * out


def _conformer(x2, norm_g, scale, shift, gate, w_in, b_in, dw_w, dw_b, ln_g, ln_b, w_out, b_out, seq_len, tm=256):
    tok, d = x2.shape
    per_seq = seq_len // tm
    row = lambda i: (i, 0)
    vec = pl.BlockSpec((1, 1, d), lambda i: (i // per_seq, 0, 0))
    return pl.pallas_call(
        functools.partial(_conformer_kernel, tm=tm, per_seq=per_seq),
        grid=(tok // tm,),
        in_specs=[pl.BlockSpec((tm, d), row), _const_spec((1, d)), vec, vec, vec,
                  _const_spec((d, 2 * d)), _const_spec((1, 2 * d)), _const_spec((CONV_WIDTH, 8, d)), _const_spec((8, d)),
                  _const_spec((1, d)), _const_spec((1, d)), _const_spec((d, d)), _const_spec((1, d))],
        out_specs=pl.BlockSpec((tm, d), row),
        out_shape=jax.ShapeDtypeStruct((tok, d), F32),
        scratch_shapes=[pltpu.VMEM((d // 128, tm + CONV_HALO, 128), F32), pltpu.VMEM((tm, d), F32),
                        pltpu.VMEM((tm, d), BF16)],
        compiler_params=_params(1),
        name="conformer_conv",
    )(x2, norm_g, scale, shift, gate, w_in.astype(BF16), b_in.reshape(1, -1),
      jnp.broadcast_to(dw_w[:, None, :], (CONV_WIDTH, 8, d)), jnp.broadcast_to(dw_b[None, :], (8, d)),
      ln_g.reshape(1, -1), ln_b.reshape(1, -1), w_out.astype(BF16), b_out.reshape(1, -1))


def kernel(x, c, mod_w, mod_b, norm_g, final_g, ab_w_in, ssm_a_re, ssm_a_im, ssm_log_dt, ssm_b_re, ssm_b_im, ssm_c_re, ssm_c_im, ssm_d, ssm_glu_w, ssm_glu_b, ab_w_out, rel_bias, cm_w_in, cm_b_in, cm_dw_w, cm_dw_b, cm_ln_g, cm_ln_b, cm_w_out, cm_b_out, ffn_w_up, ffn_w_gate, ffn_dw_w, ffn_dw_b, ffn_w_down):
    bsz, seq_len, d = x.shape
    tok = bsz * seq_len
    x2 = x.reshape(tok, d)
    mod = _modulation(c, mod_w, mod_b)
    vecs = [[mod[l, :, i * d:(i + 1) * d].reshape(bsz, 1, d) for i in range(6)] for l in range(2)]
    fin = final_g.reshape(1, d)

    sh1, sc1, g1, sh2, sc2, g2 = vecs[0]
    u, q, k, v = _in_projection(x2, norm_g[0, 0].reshape(1, d), sc1, sh1, ab_w_in[0].astype(BF16), seq_len)
    ops = _s5_prepare(ssm_a_re[0], ssm_a_im[0], ssm_log_dt[0], ssm_b_re[0], ssm_b_im[0], ssm_c_re[0], ssm_c_im[0])
    y_ssm = _s5_mixer(u.reshape(S5_T, bsz, seq_len // S5_T, D_SSM), ops, ssm_d[0], ssm_glu_w[0], ssm_glu_b[0])
    att = lambda a: a.reshape(bsz, seq_len, D_ATTN)
    y_att = _moba_attention(att(q), att(k), att(v), _bias_tiles(rel_bias), rel_bias[REL_BUCKETS - 1])
    w0 = _ffn_weights(ffn_w_up[0], ffn_w_gate[0], ffn_dw_w[0], ffn_dw_b[0], ffn_w_down[0])
    x2 = _conv_ffn(x2, (y_ssm.reshape(tok, D_SSM), y_att.reshape(tok, D_ATTN), ab_w_out[0].astype(BF16), g1),
                   norm_g[0, 1].reshape(1, d), sc2, sh2, g2, w0, fin, seq_len, final_norm=False)

    sh1, sc1, g1, sh2, sc2, g2 = vecs[1]
    x2 = _conformer(x2, norm_g[1, 0].reshape(1, d), sc1, sh1, g1, cm_w_in[0], cm_b_in[0], cm_dw_w[0], cm_dw_b[0],
                    cm_ln_g[0], cm_ln_b[0], cm_w_out[0], cm_b_out[0], seq_len)
    w1 = _ffn_weights(ffn_w_up[1], ffn_w_gate[1], ffn_dw_w[1], ffn_dw_b[1], ffn_w_down[1])
    x2 = _conv_ffn(x2, None, norm_g[1, 1].reshape(1, d), sc2, sh2, g2, w1, fin, seq_len, final_norm=True)
    return x2.reshape(bsz, seq_len, d)
```

```python
import functools
import math

import numpy as np
import jax
import jax.numpy as jnp
from jax import lax
from jax.experimental import pallas as pl
from jax.experimental.pallas import tpu as pltpu

F32 = jnp.float32
BF16 = jnp.bfloat16

D_MODEL = 1024
D_SSM = 512
SSM_GROUP = 16
SSM_GROUPS = 32
SSM_STATE = 64
D_ATTN = 512
HEAD_DIM = 64
N_HEADS = 8
MOBA_BLOCK = 256
MOBA_TOPK = 3
REL_BUCKETS = 32
REL_MAX_DIST = 128
CONV_WIDTH = 31
FFN_HIDDEN = 2816
FFN_CONV_WIDTH = 3
EPS = 1e-6

NEG = -1e30
LOG2E = math.log2(math.e)

V7X_VMEM_BYTES = 64 * 1024 * 1024
VMEM_LIMIT = V7X_VMEM_BYTES - 8 * 1024 * 1024

S5_T = 4
S5_CB = 32
S5_ROW_STRIDE = S5_CB + 8
LANES = 128
MXU_TILE = 256
S5_QUAD = 4
FFN_CHUNK = 256
N_FFN_CHUNKS = FFN_HIDDEN // FFN_CHUNK
CONV_HALO = 32
CONV_CHUNK = 256
FFN_HALO = 8
ATTN_GROUP = 4
PV_ROWS = HEAD_DIM + 16


def _sigmoid(x):
    return 0.5 * jnp.tanh(0.5 * x) + 0.5


def _silu(x):
    return x * _sigmoid(x)


def _gelu_tanh(x):
    c = math.sqrt(2.0 / math.pi)
    return 0.5 * x * (1.0 + jnp.tanh(c * (x + 0.044715 * (x * x * x))))


def _norm_mod(x, g, scale, shift):
    ms = jnp.mean(x * x, axis=-1, keepdims=True)
    y = x * lax.rsqrt(ms + EPS) * g
    return y * (1.0 + scale) + shift


def _params(n_axes, vmem=VMEM_LIMIT, flags=None):
    return pltpu.CompilerParams(dimension_semantics=("arbitrary",) * n_axes, vmem_limit_bytes=vmem, flags=flags)


def _const_spec(shape):
    nd = len(shape)
    return pl.BlockSpec(shape, lambda *_: (0,) * nd, pipeline_mode=pl.Buffered(1))


def _mod_kernel(c_ref, w_ref, b_ref, o_ref):
    c = c_ref[...]
    cs = _silu(c).astype(BF16)
    o_ref[0] = jnp.dot(cs, w_ref[0].astype(BF16), preferred_element_type=F32) + b_ref[0]


def _modulation(c, mod_w, mod_b):
    depth, d, n = mod_w.shape
    bsz = c.shape[0]
    nt = 1536
    return pl.pallas_call(
        _mod_kernel,
        grid=(depth, n // nt),
        in_specs=[
            pl.BlockSpec((bsz, d), lambda l, j: (0, 0)),
            pl.BlockSpec((1, d, nt), lambda l, j: (l, 0, j)),
            pl.BlockSpec((1, 1, nt), lambda l, j: (l, 0, j)),
        ],
        out_specs=pl.BlockSpec((1, bsz, nt), lambda l, j: (l, 0, j)),
        out_shape=jax.ShapeDtypeStruct((depth, bsz, n), F32),
        compiler_params=_params(2),
        name="modulation",
    )(c, mod_w, mod_b.reshape(depth, 1, n))


def _inproj_kernel(x_ref, g_ref, sc_ref, sh_ref, w_ref, u_ref, q_ref, k_ref, v_ref, u_sc):
    h = _norm_mod(x_ref[...], g_ref[...], sc_ref[0], sh_ref[0]).astype(BF16)
    p = jnp.dot(h, w_ref[...], preferred_element_type=F32)
    tm = p.shape[0]
    lanes = D_SSM // LANES
    for l in range(lanes):
        u_sc[l] = p[:, l * LANES:(l + 1) * LANES]
    for s in range(S5_T):
        u_ref[s] = jnp.concatenate([u_sc[l, pl.ds(s, tm // S5_T, stride=S5_T), :] for l in range(lanes)], axis=-1)
    q_ref[...] = (p[:, D_SSM:D_SSM + D_ATTN] * (HEAD_DIM ** -0.5 * LOG2E)).astype(BF16)
    k_ref[...] = p[:, D_SSM + D_ATTN:D_SSM + 2 * D_ATTN].astype(BF16)
    v_ref[...] = p[:, D_SSM + 2 * D_ATTN:].astype(BF16)


def _in_projection(x2, g, scale, shift, w, seq_len, tm=1024):
    tok, d = x2.shape
    per_seq = seq_len // tm
    n = w.shape[1]
    row = lambda i: (i, 0)
    bat = lambda i: (i // per_seq, 0, 0)
    return pl.pallas_call(
        _inproj_kernel,
        grid=(tok // tm,),
        in_specs=[
            pl.BlockSpec((tm, d), row),
            _const_spec((1, d)),
            pl.BlockSpec((1, 1, d), bat),
            pl.BlockSpec((1, 1, d), bat),
            _const_spec((d, n)),
        ],
        out_specs=[
            pl.BlockSpec((S5_T, tm // S5_T, D_SSM), lambda i: (0, i, 0)),
            pl.BlockSpec((tm, D_ATTN), row),
            pl.BlockSpec((tm, D_ATTN), row),
            pl.BlockSpec((tm, D_ATTN), row),
        ],
        out_shape=[
            jax.ShapeDtypeStruct((S5_T, tok // S5_T, D_SSM), F32),
            jax.ShapeDtypeStruct((tok, D_ATTN), BF16),
            jax.ShapeDtypeStruct((tok, D_ATTN), BF16),
            jax.ShapeDtypeStruct((tok, D_ATTN), BF16),
        ],
        scratch_shapes=[pltpu.VMEM((D_SSM // LANES, tm, LANES), F32)],
        compiler_params=_params(1),
        name="in_projection",
    )(x2, g, scale, shift, w)


def _s5_prep_kernel(lre_r, lim_r, ldt_r, lre_c, lim_c, ldt_c, btr, bti, cre, cim, ctr, cti,
                    kt_ref, sre_ref, sim_ref, ore_ref, oim_ref, at_ref):
    def discretise(lre, lim, ldt):
        dt = jnp.exp(ldt)
        mag = jnp.exp(lre * dt)
        return mag * jnp.cos(lim * dt), mag * jnp.sin(lim * dt)

    lre, lim = lre_r[...], lim_r[...]
    ar, ai = discretise(lre, lim, ldt_r[...])
    den = lre * lre + lim * lim
    nr = ar - 1.0
    coef_re = (nr * lre + ai * lim) / den
    coef_im = (ai * lre - nr * lim) / den
    br, bi = btr[...], bti[...]
    zr = coef_re * br - coef_im * bi
    zi = coef_re * bi + coef_im * br
    c_re, c_im = cre[...], cim[...]
    for k in range(S5_T):
        sre_ref[S5_T - 1 - k] = zr
        sim_ref[S5_T - 1 - k] = zi
        for h in range(SSM_GROUP):
            kt_ref[k, h] = jnp.sum(c_re[:, h:h + 1, :] * zr - c_im[:, h:h + 1, :] * zi, axis=-1)
        zr, zi = ar * zr - ai * zi, ar * zi + ai * zr

    acr, aci = discretise(lre_c[...], lim_c[...], ldt_c[...])
    pr, pi = acr, aci
    ct_re, ct_im = ctr[...], cti[...]
    for t in range(S5_T):
        ore_ref[t] = ct_re * pr - ct_im * pi
        oim_ref[t] = -ct_re * pi - ct_im * pr
        pr, pi = acr * pr - aci * pi, acr * pi + aci * pr

    qr, qi = ar, ai
    for _ in range(S5_T - 1):
        qr, qi = ar * qr - ai * qi, ar * qi + ai * qr
    at_ref[0] = qr
    at_ref[1] = qi


def _s5_prepare(a_re, a_im, log_dt, b_re, b_im, c_re, c_im):
    g, p, h, t = SSM_GROUPS, SSM_STATE, SSM_GROUP, S5_T
    ins = [
        a_re.reshape(g, 1, p), a_im.reshape(g, 1, p), log_dt.reshape(g, 1, 1),
        a_re.reshape(g, p, 1), a_im.reshape(g, p, 1), log_dt.reshape(g, 1, 1),
        b_re.transpose(0, 2, 1), b_im.transpose(0, 2, 1), c_re, c_im,
        c_re.transpose(0, 2, 1), c_im.transpose(0, 2, 1),
    ]
    full = lambda s: pl.BlockSpec(s, lambda: (0,) * len(s))
    out_shapes = [(t, h, g, h), (t, g, h, p), (t, g, h, p), (t, g, p, h), (t, g, p, h), (2, g, 1, p)]
    kt, sre, sim, ore, oim, at = pl.pallas_call(
        _s5_prep_kernel,
        in_specs=[full(a.shape) for a in ins],
        out_specs=[full(s) for s in out_shapes],
        out_shape=[jax.ShapeDtypeStruct(s, F32) for s in out_shapes],
        name="s5_prepare",
    )(*ins)

    q4 = S5_QUAD
    nq = g // q4
    eye = jnp.eye(q4, dtype=F32)
    ktg = kt.transpose(0, 2, 3, 1).reshape(t, nq, q4, h, h)
    steps = jnp.arange(t)
    lag_is = ((steps[None, :] - steps[:, None])[None] == steps[:, None, None]).astype(F32)
    toe = jnp.einsum("kst,kqaxy,ac->qsaxtcy", lag_is, ktg, eye).reshape(nq, t * q4 * h, t * q4 * h)

    def s_quads(s):
        return jnp.einsum("sqaxp,ac->qsaxcp", s.reshape(t, nq, q4, h, p), eye).reshape(nq, t * q4 * h, q4 * p)

    def o_quads(o):
        return jnp.einsum("tqapy,ac->qaptcy", o.reshape(t, nq, q4, p, h), eye).reshape(nq, q4 * p, t * q4 * h)

    smat = jnp.concatenate([s_quads(sre), s_quads(sim)], axis=-1)
    omat = jnp.concatenate([o_quads(ore), o_quads(oim)], axis=-2)
    atr = at[0].reshape(1, g * p)
    ati = at[1].reshape(1, g * p)
    return toe.astype(BF16), smat.astype(BF16), omat.astype(BF16), atr, ati


def _s5_kernel(u_ref, toe_ref, smat_ref, omat_ref, atr_ref, ati_ref, d_ref, gw_ref, gb_ref,
               y_ref, yq_sc, s_sc, xp_sc, cr_sc, ci_sc, y_sc, *, bsz):
    rows = bsz * S5_CB
    nq = SSM_GROUPS // S5_QUAD
    qch = S5_QUAD * SSM_GROUP
    qst = S5_QUAD * SSM_STATE

    @pl.when(pl.program_id(0) == 0)
    def _():
        cr_sc[...] = jnp.zeros_like(cr_sc)
        ci_sc[...] = jnp.zeros_like(ci_sc)

    u = [u_ref[s].reshape(rows, D_SSM) for s in range(S5_T)]
    xq = [jnp.concatenate([u[s][:, q * qch:(q + 1) * qch] for s in range(S5_T)], axis=-1).astype(BF16)
          for q in range(nq)]

    lt = 2 * qst // LANES
    for q in range(nq):
        st = jnp.dot(xq[q], smat_ref[q], preferred_element_type=F32)
        for l in range(lt):
            for b in range(bsz):
                s_sc[q * lt + l, b * S5_ROW_STRIDE:b * S5_ROW_STRIDE + S5_CB, :] = (
                    st[b * S5_CB:(b + 1) * S5_CB, l * LANES:(l + 1) * LANES])

    half_lt = lt // 2
    for q in range(nq):
        for l in range(half_lt):
            cols = slice(q * qst + l * LANES, q * qst + (l + 1) * LANES)
            a_r = atr_ref[:, cols]
            a_i = ati_ref[:, cols]
            xr = cr_sc[:, cols]
            xi = ci_sc[:, cols]
            t_re = q * lt + l
            t_im = q * lt + half_lt + l
            for c in range(S5_CB):
                idx = pl.ds(c, bsz, stride=S5_ROW_STRIDE)
                xp_sc[t_re, idx, :] = xr
                xp_sc[t_im, idx, :] = xi
                sr = s_sc[t_re, idx, :]
                si = s_sc[t_im, idx, :]
                xr, xi = a_r * xr - a_i * xi + sr, a_r * xi + a_i * xr + si
            cr_sc[:, cols] = xr
            ci_sc[:, cols] = xi

    def xp_tile(i):
        return jnp.concatenate([xp_sc[i, b * S5_ROW_STRIDE:b * S5_ROW_STRIDE + S5_CB, :] for b in range(bsz)], axis=0)

    for q in range(nq):
        xpq = jnp.concatenate([xp_tile(q * lt + l) for l in range(lt)], axis=-1).astype(BF16)
        yq_sc[q] = (jnp.dot(xq[q], toe_ref[q], preferred_element_type=F32)
                    + jnp.dot(xpq, omat_ref[q], preferred_element_type=F32))

    gw = gw_ref[...]
    for t in range(S5_T):
        yt = jnp.concatenate([yq_sc[q, :, t * qch:(t + 1) * qch] for q in range(nq)], axis=-1)
        y = _gelu_tanh(yt + d_ref[...] * u[t])
        z = jnp.dot(y.astype(BF16), gw, preferred_element_type=F32) + gb_ref[...]
        out = y * _sigmoid(z)
        for b in range(bsz):
            for l in range(D_SSM // LANES):
                y_sc[l, pl.ds(b * S5_T * S5_CB + t, S5_CB, stride=S5_T), :] = (
                    out[b * S5_CB:(b + 1) * S5_CB, l * LANES:(l + 1) * LANES])
    span = S5_T * S5_CB
    for b in range(bsz):
        y_ref[b] = jnp.concatenate([y_sc[l, b * span:(b + 1) * span, :] for l in range(D_SSM // LANES)],
                                   axis=-1).astype(BF16)


def _s5_mixer(u, ops, d_skip, glu_w, glu_b):
    _, bsz, nchunk, _ = u.shape
    seq_len = nchunk * S5_T
    toe, smat, omat, atr, ati = ops
    rows = bsz * S5_CB
    state_w = 2 * SSM_GROUPS * SSM_STATE
    assert S5_T * S5_QUAD * SSM_GROUP == MXU_TILE and S5_QUAD * SSM_STATE == MXU_TILE
    return pl.pallas_call(
        functools.partial(_s5_kernel, bsz=bsz),
        grid=(nchunk // S5_CB,),
        in_specs=[
            pl.BlockSpec((S5_T, bsz, S5_CB, D_SSM), lambda i: (0, 0, i, 0)),
            _const_spec(toe.shape), _const_spec(smat.shape), _const_spec(omat.shape),
            _const_spec(atr.shape), _const_spec(ati.shape),
            _const_spec((1, D_SSM)), _const_spec((D_SSM, D_SSM)), _const_spec((1, D_SSM)),
        ],
        out_specs=pl.BlockSpec((bsz, S5_T * S5_CB, D_SSM), lambda i: (0, i, 0)),
        out_shape=jax.ShapeDtypeStruct((bsz, seq_len, D_SSM), BF16),
        scratch_shapes=[
            pltpu.VMEM((SSM_GROUPS // S5_QUAD, rows, MXU_TILE), F32),
            pltpu.VMEM((state_w // LANES, bsz * S5_ROW_STRIDE, LANES), F32),
            pltpu.VMEM((state_w // LANES, bsz * S5_ROW_STRIDE, LANES), F32),
            pltpu.VMEM((bsz, state_w // 2), F32),
            pltpu.VMEM((bsz, state_w // 2), F32),
            pltpu.VMEM((D_SSM // LANES, rows * S5_T, LANES), F32),
        ],
        compiler_params=_params(1),
        name="s5_mixer",
    )(u, toe, smat, omat, atr, ati, d_skip.reshape(1, D_SSM), glu_w.astype(BF16), glu_b.reshape(1, D_SSM))


def _rel_bucket_np(dist):
    n = np.maximum(dist, 0)
    max_exact = REL_BUCKETS // 2
    nf = np.maximum(n, 1).astype(np.float64)
    large = max_exact + (np.log(nf / max_exact) / math.log(REL_MAX_DIST / max_exact)
                         * (REL_BUCKETS - max_exact)).astype(np.int64)
    large = np.minimum(large, REL_BUCKETS - 1)
    return np.where(n < max_exact, n, large).astype(np.int32)


def _bias_bucket_tiles():
    ko = np.arange(MOBA_BLOCK)[:, None]
    qo = np.arange(MOBA_BLOCK)[None, :]
    own = np.where(qo >= ko, _rel_bucket_np(qo - ko), -1)
    prev = _rel_bucket_np(qo - ko + MOBA_BLOCK)
    return np.stack([own, prev]).astype(np.int32)


assert int(_rel_bucket_np(np.arange(MOBA_BLOCK + 1, 8 * MOBA_BLOCK)).min()) == REL_BUCKETS - 1


def _bias_kernel(tab_ref, idx_ref, o_ref):
    h = pl.program_id(0)
    for t in range(2):
        idx = idx_ref[t]
        acc = jnp.full(idx.shape, NEG, F32)
        for b in range(REL_BUCKETS):
            acc = jnp.where(idx == b, tab_ref[h, b] * LOG2E, acc)
        o_ref[0, t] = acc


def _bias_tiles(rel_bias):
    idx = jnp.asarray(_bias_bucket_tiles())
    blk = MOBA_BLOCK
    return pl.pallas_call(
        _bias_kernel,
        grid=(N_HEADS,),
        in_specs=[
            pl.BlockSpec(memory_space=pltpu.SMEM),
            pl.BlockSpec((2, blk, blk), lambda h: (0, 0, 0)),
        ],
        out_specs=pl.BlockSpec((1, 2, blk, blk), lambda h: (h, 0, 0, 0)),
        out_shape=jax.ShapeDtypeStruct((N_HEADS, 2, blk, blk), F32),
        compiler_params=_params(1),
        name="moba_bias_tiles",
    )(rel_bias.T, idx)


def _attn_kernel(far_ref, q_ref, qn_ref, k_ref, v_ref, bias_ref, o_ref,
                 kmean_sc, vt_sc, mfar_sc, msel_sc, rows_sc, s0_sc, s1_sc, gmax0_sc, gmax1_sc, m_sc, acc_sc, *, nb):
    s_bufs = (s0_sc, s1_sc)
    gmax_bufs = (gmax0_sc, gmax1_sc)
    hp = pl.program_id(1)
    qt = pl.program_id(2)
    n_tiles = nb // 2
    blk = MOBA_BLOCK
    qw = 2 * blk
    qlane = lax.broadcasted_iota(jnp.int32, (1, qw), 1)
    lane = lax.broadcasted_iota(jnp.int32, (1, 2 * HEAD_DIM), 1)
    head_mask = [lane < HEAD_DIM, lane >= HEAD_DIM]
    nt = (((1,), (1,)), ((), ()))

    @pl.when(qt == 0)
    def _():
        for j in range(nb):
            kb = k_ref[0, j * blk:(j + 1) * blk, :].astype(F32)
            km = jnp.mean(kb, axis=0, keepdims=True)
            for hh in range(2):
                kmean_sc[hh, j:j + 1, :] = jnp.where(head_mask[hh], km, 0.0)
            vt = v_ref[0, j * blk:(j + 1) * blk, :].astype(F32).T.astype(BF16)
            ones_row = jnp.where(lax.broadcasted_iota(jnp.int32, (PV_ROWS - HEAD_DIM, blk), 0) == 0, 1.0, 0.0)
            for hh in range(2):
                vt_sc[hh, j, 0:HEAD_DIM, :] = vt[hh * HEAD_DIM:(hh + 1) * HEAD_DIM, :]
                vt_sc[hh, j, HEAD_DIM:, :] = ones_row.astype(BF16)

    jidx = lax.broadcasted_iota(jnp.int32, (nb, qw), 0)

    def pv(hh, j, p):
        return jnp.dot(vt_sc[hh, j], p.astype(BF16), preferred_element_type=F32)

    def mask_row(ref, hh, j):
        return jnp.where(j >= 0, ref[hh, pl.ds(jnp.maximum(j, 0), 1), :], NEG)

    class Tile:
        def __init__(self, t, q2):
            self.q2 = q2
            self.top = 2 * t + 1
            self.own = 2 * t + jnp.where(qlane >= blk, 1, 0)
            self.n_groups = self.top // ATTN_GROUP + 1
            self.qm = [jnp.where(head_mask[hh], q2, jnp.zeros_like(q2)) for hh in range(2)]

        def select(self):
            for hh in range(2):
                gate = lax.dot_general(kmean_sc[hh].astype(BF16), self.q2, nt, preferred_element_type=F32)
                rank = jnp.zeros((nb, qw), F32)
                for jp in range(nb):
                    row = gate[jp:jp + 1, :]
                    beats = (row > gate) | ((row == gate) & (jidx > jp))
                    rank = rank + jnp.where(beats & (self.own > jp), 1.0, 0.0)
                sel = (rank < float(MOBA_TOPK)) & (jidx < self.own)
                mfar_sc[hh] = jnp.where(sel, far_ref[2 * hp + hh] * LOG2E, NEG)
                msel_sc[hh] = jnp.where(sel, 0.0, NEG)

        def scores(self, hh, j):
            kb = k_ref[0, pl.ds(pl.multiple_of(j * blk, blk), blk), :]
            return lax.dot_general(kb, self.qm[hh], nt, preferred_element_type=F32)

        def block_ids(self, g):
            js = [self.top - ATTN_GROUP * g - jj for jj in range(ATTN_GROUP)]
            return js, [jnp.maximum(j, 0) for j in js]

        def row_term(self, g, jj, hh, j):
            far = mask_row(mfar_sc, hh, j)
            if jj > 2:
                return far
            second = qlane >= blk
            if jj == 0:
                special = jnp.where(second, 0.0, NEG)
            elif jj == 1:
                special = jnp.where(second, mask_row(msel_sc, hh, j), 0.0)
            else:
                special = jnp.where(second, far, mask_row(msel_sc, hh, j))
            return jnp.where(g == 0, special, far)

    class PhaseA:
        def __init__(self, tile, g, buf, first):
            self.tile, self.g, self.first = tile, g, first
            self.js, self.jc = tile.block_ids(g)
            self.s_buf, self.gmax_buf = s_bufs[buf], gmax_bufs[buf]
            self.gmax = [None, None]

        def block(self, hh, jj):
            s = self.tile.scores(hh, self.jc[jj])
            if self.first and jj < 3:
                tiles = [(None, 0), (0, 1), (1, None)][jj]
                parts = [s[:, h * blk:(h + 1) * blk] if t is None else s[:, h * blk:(h + 1) * blk] + bias_ref[hh, t]
                         for h, t in enumerate(tiles)]
                s = jnp.concatenate(parts, axis=1)
            self.s_buf[hh, jj] = s
            cm = jnp.max(s, axis=0, keepdims=True) + self.tile.row_term(self.g, jj, hh, self.js[jj])
            self.gmax[hh] = cm if self.gmax[hh] is None else jnp.maximum(self.gmax[hh], cm)

        def finish(self):
            for hh in range(2):
                self.gmax_buf[hh] = self.gmax[hh]

    class PhaseB:
        def __init__(self, tile, g, buf, stash_rows=False):
            self.tile, self.g, self.stash_rows = tile, g, stash_rows
            self.js, self.jc = tile.block_ids(g)
            self.s_buf = s_bufs[buf]
            self.m_old = [m_sc[hh] for hh in range(2)]
            self.m_new = [jnp.maximum(self.m_old[hh], gmax_bufs[buf][hh]) for hh in range(2)]
            self.acc = [None, None]
            if stash_rows:
                for hh in range(2):
                    for jj in range(ATTN_GROUP):
                        rows_sc[hh, jj] = tile.row_term(g, jj, hh, self.js[jj])

        def block(self, hh, jj):
            row = rows_sc[hh, jj] if self.stash_rows else self.tile.row_term(self.g, jj, hh, self.js[jj])
            p = jnp.exp2(self.s_buf[hh, jj] - (self.m_new[hh] - row))
            pa = pv(hh, self.jc[jj], p)
            self.acc[hh] = pa if self.acc[hh] is None else self.acc[hh] + pa

        def finish(self):
            for hh in range(2):
                alpha = jnp.exp2(self.m_old[hh] - self.m_new[hh])
                m_sc[hh] = self.m_new[hh]
                acc_sc[hh] = alpha * acc_sc[hh] + self.acc[hh]

    def run(*phases):
        for hh in range(2):
            for jj in range(ATTN_GROUP):
                for ph in phases:
                    ph.block(hh, jj)
        for ph in phases:
            ph.finish()

    cur = Tile(qt, q_ref[0])

    @pl.when(qt == 0)
    def _():
        cur.select()
        run(PhaseA(cur, 0, 0, True))

    for hh in range(2):
        m_sc[hh] = gmax0_sc[hh]
        acc_sc[hh] = jnp.zeros((PV_ROWS, qw), F32)

    def pipelined(g, carry):
        for p in range(2):
            @pl.when(g % 2 == p)
            def _():
                run(PhaseB(cur, g, p), PhaseA(cur, g + 1, 1 - p, False))

        return carry

    lax.fori_loop(0, cur.n_groups - 1, pipelined, 0)

    def write_out():
        out = jnp.concatenate([acc_sc[hh, 0:HEAD_DIM, :] / acc_sc[hh, HEAD_DIM:HEAD_DIM + 1, :] for hh in range(2)],
                              axis=0)
        o_ref[0] = out.T.astype(BF16)

    last = cur.n_groups - 1
    for p in range(2):
        @pl.when((last % 2 == p) & (qt < n_tiles - 1))
        def _():
            nxt = Tile(qt + 1, qn_ref[0])
            tail = PhaseB(cur, last, p, stash_rows=True)
            nxt.select()
            run(tail, PhaseA(nxt, 0, 0, True))
            write_out()

        @pl.when((last % 2 == p) & (qt == n_tiles - 1))
        def _():
            run(PhaseB(cur, last, p))
            write_out()


def _moba_attention(q, k, v, bias_tiles, far_bias):
    bsz, seq_len, _ = q.shape
    blk = MOBA_BLOCK
    nb = seq_len // blk
    pair = 2 * HEAD_DIM
    qw = 2 * blk
    assert nb % 2 == 0
    return pl.pallas_call(
        functools.partial(_attn_kernel, nb=nb),
        grid=(bsz, N_HEADS // 2, nb // 2),
        in_specs=[
            pl.BlockSpec(memory_space=pltpu.SMEM),
            pl.BlockSpec((1, qw, pair), lambda b, h, i: (b, i, h)),
            pl.BlockSpec((1, qw, pair), lambda b, h, i: (b, jnp.minimum(i + 1, nb // 2 - 1), h)),
            pl.BlockSpec((1, seq_len, pair), lambda b, h, i: (b, 0, h)),
            pl.BlockSpec((1, seq_len, pair), lambda b, h, i: (b, 0, h)),
            pl.BlockSpec((2, 2, blk, blk), lambda b, h, i: (h, 0, 0, 0)),
        ],
        out_specs=pl.BlockSpec((1, qw, pair), lambda b, h, i: (b, i, h)),
        out_shape=jax.ShapeDtypeStruct((bsz, seq_len, D_ATTN), BF16),
        scratch_shapes=[
            pltpu.VMEM((2, nb, pair), F32),
            pltpu.VMEM((2, nb, PV_ROWS, blk), BF16),
            pltpu.VMEM((2, nb, qw), F32),
            pltpu.VMEM((2, nb, qw), F32),
            pltpu.VMEM((2, ATTN_GROUP, 1, qw), F32),
            pltpu.VMEM((2, ATTN_GROUP, blk, qw), F32),
            pltpu.VMEM((2, ATTN_GROUP, blk, qw), F32),
            pltpu.VMEM((2, 1, qw), F32),
            pltpu.VMEM((2, 1, qw), F32),
            pltpu.VMEM((2, 1, qw), F32),
            pltpu.VMEM((2, PV_ROWS, qw), F32),
        ],
        compiler_params=_params(3),
        name="moba_attention",
    )(far_bias, q, q, k, v, bias_tiles)


def _ffn_body(x, g_ref, sc_ref, sh_ref, gate_ref, wg_ref, wu_ref, cw_ref, wd_ref, fin_ref, o_ref,
              gext_sc, carry_sc, acc_sc, h_sc, *, tm, per_seq, final_norm):
    @pl.when(pl.program_id(0) % per_seq == 0)
    def _():
        carry_sc[...] = jnp.zeros_like(carry_sc)

    h_sc[...] = _norm_mod(x, g_ref[...], sc_ref[0], sh_ref[0]).astype(BF16)

    def gate_up(j):
        cols = slice(j * FFN_CHUNK, (j + 1) * FFN_CHUNK)
        h = h_sc[...]
        return (jnp.dot(h, wg_ref[:, cols], preferred_element_type=F32),
                jnp.dot(h, wu_ref[:, cols], preferred_element_type=F32))

    nxt = gate_up(0)
    for j in range(N_FFN_CHUNKS):
        cols = slice(j * FFN_CHUNK, (j + 1) * FFN_CHUNK)
        gpre, up = nxt
        if j + 1 < N_FFN_CHUNKS:
            nxt = gate_up(j + 1)
        lane_tiles = FFN_CHUNK // LANES
        for l in range(lane_tiles):
            gext_sc[j, l, 0:FFN_HALO, :] = carry_sc[j, :, l * LANES:(l + 1) * LANES]
            gext_sc[j, l, FFN_HALO:, :] = gpre[:, l * LANES:(l + 1) * LANES]
        carry_sc[j] = gpre[tm - FFN_HALO:, :]

        def back(n):
            return jnp.concatenate([gext_sc[j, l, FFN_HALO - n:FFN_HALO - n + tm, :] for l in range(lane_tiles)],
                                   axis=-1)

        cw = cw_ref[:, cols]
        conv = cw[0:1] * back(2) + cw[1:2] * back(1) + cw[2:3] * gpre + cw[3:4]
        act = (_silu(conv) * up).astype(BF16)
        part = jnp.dot(act, wd_ref[cols, :], preferred_element_type=F32)
        if j == 0:
            acc_sc[...] = part
        else:
            acc_sc[...] += part
    out = x + gate_ref[0] * acc_sc[...]
    if final_norm:
        ms = jnp.mean(out * out, axis=-1, keepdims=True)
        out = out * lax.rsqrt(ms + EPS) * fin_ref[...]
    o_ref[...] = out


def _ffn0_kernel(x_ref, ys_ref, ya_ref, wo_ref, g1_ref, g_ref, sc_ref, sh_ref, gate_ref,
                 wg_ref, wu_ref, cw_ref, wd_ref, fin_ref, o_ref, gext_sc, carry_sc, acc_sc, h_sc, **kw):
    y = (jnp.dot(ys_ref[...], wo_ref[0:D_SSM, :], preferred_element_type=F32)
         + jnp.dot(ya_ref[...], wo_ref[D_SSM:, :], preferred_element_type=F32))
    x = x_ref[...] + g1_ref[0] * y
    _ffn_body(x, g_ref, sc_ref, sh_ref, gate_ref, wg_ref, wu_ref, cw_ref, wd_ref, fin_ref, o_ref,
              gext_sc, carry_sc, acc_sc, h_sc, **kw)


def _ffn1_kernel(x_ref, g_ref, sc_ref, sh_ref, gate_ref, wg_ref, wu_ref, cw_ref, wd_ref, fin_ref,
                 o_ref, gext_sc, carry_sc, acc_sc, h_sc, **kw):
    _ffn_body(x_ref[...], g_ref, sc_ref, sh_ref, gate_ref, wg_ref, wu_ref, cw_ref, wd_ref, fin_ref, o_ref,
              gext_sc, carry_sc, acc_sc, h_sc, **kw)


def _ffn_weights(w_up, w_gate, dw_w, dw_b, w_down, layer):
    cw = jnp.concatenate([dw_w[layer], dw_b[layer][None, :]], axis=0)
    return w_gate.astype(BF16), w_up.astype(BF16), cw, w_down.astype(BF16), layer


def _conv_ffn(x2, mixer, norm_g, scale, shift, gate, weights, final_g, seq_len, final_norm, tm=256):
    tok, d = x2.shape
    per_seq = seq_len // tm
    wg, wu, cw, wd, layer = weights
    row = lambda i: (i, 0)
    bat = lambda i: (i // per_seq, 0, 0)
    vec = pl.BlockSpec((1, 1, d), bat)
    layer_spec = lambda w: pl.BlockSpec((None,) + w.shape[1:], lambda i: (layer, 0, 0), pipeline_mode=pl.Buffered(1))
    common_specs = [_const_spec((1, d)), vec, vec, vec,
                    layer_spec(wg), layer_spec(wu), _const_spec(cw.shape), layer_spec(wd),
                    _const_spec((1, d))]
    common_args = [norm_g, scale, shift, gate, wg, wu, cw, wd, final_g]
    kw = dict(tm=tm, per_seq=per_seq, final_norm=final_norm)
    if mixer is None:
        body = functools.partial(_ffn1_kernel, **kw)
        specs = [pl.BlockSpec((tm, d), row)] + common_specs
        args = [x2] + common_args
    else:
        ys, ya, wo, g1 = mixer
        body = functools.partial(_ffn0_kernel, **kw)
        specs = [pl.BlockSpec((tm, d), row), pl.BlockSpec((tm, D_SSM), row), pl.BlockSpec((tm, D_ATTN), row),
                 _const_spec(wo.shape), vec] + common_specs
        args = [x2, ys, ya, wo, g1] + common_args
    return pl.pallas_call(
        body,
        grid=(tok // tm,),
        in_specs=specs,
        out_specs=pl.BlockSpec((tm, d), row),
        out_shape=jax.ShapeDtypeStruct((tok, d), F32),
        scratch_shapes=[
            pltpu.VMEM((N_FFN_CHUNKS, FFN_CHUNK // LANES, tm + FFN_HALO, LANES), F32),
            pltpu.VMEM((N_FFN_CHUNKS, FFN_HALO, FFN_CHUNK), F32),
            pltpu.VMEM((tm, d), F32),
            pltpu.VMEM((tm, d), BF16),
        ],
        compiler_params=_params(1),
        name="conv_ffn_final" if final_norm else "conv_ffn",
    )(*args)


def _conformer_kernel(x_ref, g_ref, sc_ref, sh_ref, gate_ref, win_ref, bin_ref, dw_ref, dwb_ref,
                      lng_ref, lnb_ref, wout_ref, bout_ref, o_ref, aext_sc, conv_sc, h_sc, *, tm, per_seq):
    d = x_ref.shape[-1]
    lanes = d // LANES
    x = x_ref[...]
    h_sc[...] = _norm_mod(x, g_ref[...], sc_ref[0], sh_ref[0]).astype(BF16)

    @pl.when(pl.program_id(0) % per_seq == 0)
    def _():
        aext_sc[:, 0:CONV_HALO, :] = jnp.zeros((lanes, CONV_HALO, LANES), F32)

    rc = 32
    off = CONV_HALO - (CONV_WIDTH - 1)
    for c0 in range(0, d, CONV_CHUNK):
        cc = slice(c0, c0 + CONV_CHUNK)
        gc = slice(d + c0, d + c0 + CONV_CHUNK)
        h = h_sc[...]
        a = ((jnp.dot(h, win_ref[:, cc], preferred_element_type=F32) + bin_ref[:, cc])
             * _sigmoid(jnp.dot(h, win_ref[:, gc], preferred_element_type=F32) + bin_ref[:, gc]))
        for l in range(c0 // LANES, (c0 + CONV_CHUNK) // LANES):
            cols = slice(l * LANES, (l + 1) * LANES)
            aext_sc[l, CONV_HALO:, :] = a[:, l * LANES - c0:(l + 1) * LANES - c0]
            for base in range(0, tm, rc):
                acc = [dwb_ref[:, cols]] * (rc // 8)
                for k in range(CONV_WIDTH):
                    w8 = dw_ref[k, :, cols]
                    for j in range(rc // 8):
                        lo = base + 8 * j + off + k
                        acc[j] = acc[j] + w8 * aext_sc[l, lo:lo + 8, :]
                for j in range(rc // 8):
                    conv_sc[base + 8 * j:base + 8 * j + 8, cols] = acc[j]
            aext_sc[l, 0:CONV_HALO, :] = aext_sc[l, tm:tm + CONV_HALO, :]

    c = conv_sc[...]
    mu = jnp.mean(c, axis=-1, keepdims=True)
    xc = c - mu
    y = xc * lax.rsqrt(jnp.mean(xc * xc, axis=-1, keepdims=True) + EPS)
    y = _silu(y * lng_ref[...] + lnb_ref[...]).astype(BF16)
    out = jnp.dot(y, wout_ref[...], preferred_element_type=F32) + bout_ref[...]
    o_ref[...] = x + gate_ref[0] * out


def _conformer(x2, norm_g, scale, shift, gate, w_in, b_in, dw_w, dw_b, ln_g, ln_b, w_out, b_out, seq_len, tm=512):
    tok, d = x2.shape
    per_seq = seq_len // tm
    row = lambda i: (i, 0)
    vec = pl.BlockSpec((1, 1, d), lambda i: (i // per_seq, 0, 0))
    return pl.pallas_call(
        functools.partial(_conformer_kernel, tm=tm, per_seq=per_seq),
        grid=(tok // tm,),
        in_specs=[pl.BlockSpec((tm, d), row), _const_spec((1, d)), vec, vec, vec,
                  _const_spec((d, 2 * d)), _const_spec((1, 2 * d)), _const_spec((CONV_WIDTH, 8, d)), _const_spec((8, d)),
                  _const_spec((1, d)), _const_spec((1, d)), _const_spec((d, d)), _const_spec((1, d))],
        out_specs=pl.BlockSpec((tm, d), row),
        out_shape=jax.ShapeDtypeStruct((tok, d), F32),
        scratch_shapes=[pltpu.VMEM((d // LANES, tm + CONV_HALO, LANES), F32), pltpu.VMEM((tm, d), F32),
                        pltpu.VMEM((tm, d), BF16)],
        compiler_params=_params(1),
        name="conformer_conv",
    )(x2, norm_g, scale, shift, gate, w_in.astype(BF16), b_in.reshape(1, -1),
      jnp.broadcast_to(dw_w[:, None, :], (CONV_WIDTH, 8, d)), jnp.broadcast_to(dw_b[None, :], (8, d)),
      ln_g.reshape(1, -1), ln_b.reshape(1, -1), w_out.astype(BF16), b_out.reshape(1, -1))


def kernel(x, c, mod_w, mod_b, norm_g, final_g, ab_w_in, ssm_a_re, ssm_a_im, ssm_log_dt, ssm_b_re, ssm_b_im, ssm_c_re, ssm_c_im, ssm_d, ssm_glu_w, ssm_glu_b, ab_w_out, rel_bias, cm_w_in, cm_b_in, cm_dw_w, cm_dw_b, cm_ln_g, cm_ln_b, cm_w_out, cm_b_out, ffn_w_up, ffn_w_gate, ffn_dw_w, ffn_dw_b, ffn_w_down):
    bsz, seq_len, d = x.shape
    tok = bsz * seq_len
    x2 = x.reshape(tok, d)
    mod = _modulation(c, mod_w, mod_b)
    vecs = [[mod[l, :, i * d:(i + 1) * d].reshape(bsz, 1, d) for i in range(6)] for l in range(2)]
    fin = final_g.reshape(1, d)

    sh1, sc1, g1, sh2, sc2, g2 = vecs[0]
    u, q, k, v = _in_projection(x2, norm_g[0, 0].reshape(1, d), sc1, sh1, ab_w_in[0].astype(BF16), seq_len)
    ops = _s5_prepare(ssm_a_re[0], ssm_a_im[0], ssm_log_dt[0], ssm_b_re[0], ssm_b_im[0], ssm_c_re[0], ssm_c_im[0])
    y_ssm = _s5_mixer(u.reshape(S5_T, bsz, seq_len // S5_T, D_SSM), ops, ssm_d[0], ssm_glu_w[0], ssm_glu_b[0])
    att = lambda a: a.reshape(bsz, seq_len, D_ATTN)
    y_att = _moba_attention(att(q), att(k), att(v), _bias_tiles(rel_bias), rel_bias[REL_BUCKETS - 1])
    w0 = _ffn_weights(ffn_w_up, ffn_w_gate, ffn_dw_w, ffn_dw_b, ffn_w_down, 0)
    x2 = _conv_ffn(x2, (y_ssm.reshape(tok, D_SSM), y_att.reshape(tok, D_ATTN), ab_w_out[0].astype(BF16), g1),
                   norm_g[0, 1].reshape(1, d), sc2, sh2, g2, w0, fin, seq_len, final_norm=False)

    sh1, sc1, g1, sh2, sc2, g2 = vecs[1]
    x2 = _conformer(x2, norm_g[1, 0].reshape(1, d), sc1, sh1, g1, cm_w_in[0], cm_b_in[0], cm_dw_w[0], cm_dw_b[0],
                    cm_ln_g[0], cm_ln_b[0], cm_w_out[0], cm_b_out[0], seq_len)
    w1 = _ffn_weights(ffn_w_up, ffn_w_gate, ffn_dw_w, ffn_dw_b, ffn_w_down, 1)
    x2 = _conv_ffn(x2, None, norm_g[1, 1].reshape(1, d), sc2, sh2, g2, w1, fin, seq_len, final_norm=True)
    return x2.reshape(bsz, seq_len, d)
```

```python
import functools
import math

import numpy as np
import jax
import jax.numpy as jnp
from jax import lax
from jax.experimental import pallas as pl
from jax.experimental.pallas import tpu as pltpu

F32 = jnp.float32
BF16 = jnp.bfloat16

D_MODEL = 1024
D_SSM = 512
SSM_GROUP = 16
SSM_GROUPS = 32
SSM_STATE = 64
D_ATTN = 512
HEAD_DIM = 64
N_HEADS = 8
MOBA_BLOCK = 256
MOBA_TOPK = 3
REL_BUCKETS = 32
REL_MAX_DIST = 128
CONV_WIDTH = 31
FFN_HIDDEN = 2816
FFN_CONV_WIDTH = 3
EPS = 1e-6

NEG = -1e30
LOG2E = math.log2(math.e)

V7X_VMEM_BYTES = 64 * 1024 * 1024
VMEM_LIMIT = V7X_VMEM_BYTES - 8 * 1024 * 1024

S5_T = 4
S5_CB = 32
S5_ROW_STRIDE = S5_CB + 8
LANES = 128
MXU_TILE = 256
S5_QUAD = 4
FFN_CHUNK = 256
N_FFN_CHUNKS = FFN_HIDDEN // FFN_CHUNK
CONV_HALO = 32
CONV_CHUNK = 256
FFN_HALO = 8
ATTN_GROUP = 4
PV_ROWS = HEAD_DIM + 16


def _sigmoid(x):
    return 0.5 * jnp.tanh(0.5 * x) + 0.5


def _silu(x):
    return x * _sigmoid(x)


def _gelu_tanh(x):
    c = math.sqrt(2.0 / math.pi)
    return 0.5 * x * (1.0 + jnp.tanh(c * (x + 0.044715 * (x * x * x))))


def _norm_mod(x, g, scale, shift):
    ms = jnp.mean(x * x, axis=-1, keepdims=True)
    y = x * lax.rsqrt(ms + EPS) * g
    return y * (1.0 + scale) + shift


def _params(n_axes, vmem=VMEM_LIMIT, flags=None):
    return pltpu.CompilerParams(dimension_semantics=("arbitrary",) * n_axes, vmem_limit_bytes=vmem, flags=flags)


def _const_spec(shape):
    nd = len(shape)
    return pl.BlockSpec(shape, lambda *_: (0,) * nd, pipeline_mode=pl.Buffered(1))


def _mod_kernel(c_ref, w_ref, b_ref, o_ref):
    c = c_ref[...]
    cs = _silu(c).astype(BF16)
    o_ref[0] = jnp.dot(cs, w_ref[0].astype(BF16), preferred_element_type=F32) + b_ref[0]


def _modulation(c, mod_w, mod_b):
    depth, d, n = mod_w.shape
    bsz = c.shape[0]
    nt = 1536
    return pl.pallas_call(
        _mod_kernel,
        grid=(depth, n // nt),
        in_specs=[
            pl.BlockSpec((bsz, d), lambda l, j: (0, 0)),
            pl.BlockSpec((1, d, nt), lambda l, j: (l, 0, j)),
            pl.BlockSpec((1, 1, nt), lambda l, j: (l, 0, j)),
        ],
        out_specs=pl.BlockSpec((1, bsz, nt), lambda l, j: (l, 0, j)),
        out_shape=jax.ShapeDtypeStruct((depth, bsz, n), F32),
        compiler_params=_params(2),
        name="modulation",
    )(c, mod_w, mod_b.reshape(depth, 1, n))


def _inproj_kernel(x_ref, g_ref, sc_ref, sh_ref, w_ref, u_ref, q_ref, k_ref, v_ref, u_sc):
    h = _norm_mod(x_ref[...], g_ref[...], sc_ref[0], sh_ref[0]).astype(BF16)
    p = jnp.dot(h, w_ref[...], preferred_element_type=F32)
    tm = p.shape[0]
    lanes = D_SSM // LANES
    for l in range(lanes):
        u_sc[l] = p[:, l * LANES:(l + 1) * LANES]
    for s in range(S5_T):
        u_ref[s] = jnp.concatenate([u_sc[l, pl.ds(s, tm // S5_T, stride=S5_T), :] for l in range(lanes)], axis=-1)
    q_ref[...] = (p[:, D_SSM:D_SSM + D_ATTN] * (HEAD_DIM ** -0.5 * LOG2E)).astype(BF16)
    k_ref[...] = p[:, D_SSM + D_ATTN:D_SSM + 2 * D_ATTN].astype(BF16)
    v_ref[...] = p[:, D_SSM + 2 * D_ATTN:].astype(BF16)


def _in_projection(x2, g, scale, shift, w, seq_len, tm=1024):
    tok, d = x2.shape
    per_seq = seq_len // tm
    n = w.shape[1]
    row = lambda i: (i, 0)
    bat = lambda i: (i // per_seq, 0, 0)
    return pl.pallas_call(
        _inproj_kernel,
        grid=(tok // tm,),
        in_specs=[
            pl.BlockSpec((tm, d), row),
            _const_spec((1, d)),
            pl.BlockSpec((1, 1, d), bat),
            pl.BlockSpec((1, 1, d), bat),
            _const_spec((d, n)),
        ],
        out_specs=[
            pl.BlockSpec((S5_T, tm // S5_T, D_SSM), lambda i: (0, i, 0)),
            pl.BlockSpec((tm, D_ATTN), row),
            pl.BlockSpec((tm, D_ATTN), row),
            pl.BlockSpec((tm, D_ATTN), row),
        ],
        out_shape=[
            jax.ShapeDtypeStruct((S5_T, tok // S5_T, D_SSM), F32),
            jax.ShapeDtypeStruct((tok, D_ATTN), BF16),
            jax.ShapeDtypeStruct((tok, D_ATTN), BF16),
            jax.ShapeDtypeStruct((tok, D_ATTN), BF16),
        ],
        scratch_shapes=[pltpu.VMEM((D_SSM // LANES, tm, LANES), F32)],
        compiler_params=_params(1),
        name="in_projection",
    )(x2, g, scale, shift, w)


def _s5_prep_kernel(lre_r, lim_r, ldt_r, lre_c, lim_c, ldt_c, btr, bti, cre, cim, ctr, cti,
                    kt_ref, sre_ref, sim_ref, ore_ref, oim_ref, at_ref):
    def discretise(lre, lim, ldt):
        dt = jnp.exp(ldt)
        mag = jnp.exp(lre * dt)
        return mag * jnp.cos(lim * dt), mag * jnp.sin(lim * dt)

    lre, lim = lre_r[...], lim_r[...]
    ar, ai = discretise(lre, lim, ldt_r[...])
    den = lre * lre + lim * lim
    nr = ar - 1.0
    coef_re = (nr * lre + ai * lim) / den
    coef_im = (ai * lre - nr * lim) / den
    br, bi = btr[...], bti[...]
    zr = coef_re * br - coef_im * bi
    zi = coef_re * bi + coef_im * br
    c_re, c_im = cre[...], cim[...]
    for k in range(S5_T):
        sre_ref[S5_T - 1 - k] = zr
        sim_ref[S5_T - 1 - k] = zi
        for h in range(SSM_GROUP):
            kt_ref[k, h] = jnp.sum(c_re[:, h:h + 1, :] * zr - c_im[:, h:h + 1, :] * zi, axis=-1)
        zr, zi = ar * zr - ai * zi, ar * zi + ai * zr

    acr, aci = discretise(lre_c[...], lim_c[...], ldt_c[...])
    pr, pi = acr, aci
    ct_re, ct_im = ctr[...], cti[...]
    for t in range(S5_T):
        ore_ref[t] = ct_re * pr - ct_im * pi
        oim_ref[t] = -ct_re * pi - ct_im * pr
        pr, pi = acr * pr - aci * pi, acr * pi + aci * pr

    qr, qi = ar, ai
    for _ in range(S5_T - 1):
        qr, qi = ar * qr - ai * qi, ar * qi + ai * qr
    at_ref[0] = qr
    at_ref[1] = qi


def _s5_prepare(a_re, a_im, log_dt, b_re, b_im, c_re, c_im):
    g, p, h, t = SSM_GROUPS, SSM_STATE, SSM_GROUP, S5_T
    ins = [
        a_re.reshape(g, 1, p), a_im.reshape(g, 1, p), log_dt.reshape(g, 1, 1),
        a_re.reshape(g, p, 1), a_im.reshape(g, p, 1), log_dt.reshape(g, 1, 1),
        b_re.transpose(0, 2, 1), b_im.transpose(0, 2, 1), c_re, c_im,
        c_re.transpose(0, 2, 1), c_im.transpose(0, 2, 1),
    ]
    full = lambda s: pl.BlockSpec(s, lambda: (0,) * len(s))
    out_shapes = [(t, h, g, h), (t, g, h, p), (t, g, h, p), (t, g, p, h), (t, g, p, h), (2, g, 1, p)]
    kt, sre, sim, ore, oim, at = pl.pallas_call(
        _s5_prep_kernel,
        in_specs=[full(a.shape) for a in ins],
        out_specs=[full(s) for s in out_shapes],
        out_shape=[jax.ShapeDtypeStruct(s, F32) for s in out_shapes],
        name="s5_prepare",
    )(*ins)

    q4 = S5_QUAD
    nq = g // q4
    eye = jnp.eye(q4, dtype=F32)
    ktg = kt.transpose(0, 2, 3, 1).reshape(t, nq, q4, h, h)
    steps = jnp.arange(t)
    lag_is = ((steps[None, :] - steps[:, None])[None] == steps[:, None, None]).astype(F32)
    toe = jnp.einsum("kst,kqaxy,ac->qsaxtcy", lag_is, ktg, eye).reshape(nq, t * q4 * h, t * q4 * h)

    def s_quads(s):
        return jnp.einsum("sqaxp,ac->qsaxcp", s.reshape(t, nq, q4, h, p), eye).reshape(nq, t * q4 * h, q4 * p)

    def o_quads(o):
        return jnp.einsum("tqapy,ac->qaptcy", o.reshape(t, nq, q4, p, h), eye).reshape(nq, q4 * p, t * q4 * h)

    smat = jnp.concatenate([s_quads(sre), s_quads(sim)], axis=-1)
    omat = jnp.concatenate([o_quads(ore), o_quads(oim)], axis=-2)
    atr = at[0].reshape(1, g * p)
    ati = at[1].reshape(1, g * p)
    return toe.astype(BF16), smat.astype(BF16), omat.astype(BF16), atr, ati


def _s5_kernel(u_ref, toe_ref, smat_ref, omat_ref, atr_ref, ati_ref, d_ref, gw_ref, gb_ref,
               y_ref, yq_sc, s_sc, xp_sc, cr_sc, ci_sc, y_sc, *, bsz):
    rows = bsz * S5_CB
    nq = SSM_GROUPS // S5_QUAD
    qch = S5_QUAD * SSM_GROUP
    qst = S5_QUAD * SSM_STATE

    @pl.when(pl.program_id(0) == 0)
    def _():
        cr_sc[...] = jnp.zeros_like(cr_sc)
        ci_sc[...] = jnp.zeros_like(ci_sc)

    u = [u_ref[s].reshape(rows, D_SSM) for s in range(S5_T)]
    xq = [jnp.concatenate([u[s][:, q * qch:(q + 1) * qch] for s in range(S5_T)], axis=-1).astype(BF16)
          for q in range(nq)]

    lt = 2 * qst // LANES
    for q in range(nq):
        st = jnp.dot(xq[q], smat_ref[q], preferred_element_type=F32)
        for l in range(lt):
            for b in range(bsz):
                s_sc[q * lt + l, b * S5_ROW_STRIDE:b * S5_ROW_STRIDE + S5_CB, :] = (
                    st[b * S5_CB:(b + 1) * S5_CB, l * LANES:(l + 1) * LANES])

    for q in range(nq):
        yq_sc[q] = jnp.dot(xq[q], toe_ref[q], preferred_element_type=F32)

    half_lt = lt // 2
    for q in range(nq):
        for l in range(half_lt):
            cols = slice(q * qst + l * LANES, q * qst + (l + 1) * LANES)
            a_r = atr_ref[:, cols]
            a_i = ati_ref[:, cols]
            xr = cr_sc[:, cols]
            xi = ci_sc[:, cols]
            t_re = q * lt + l
            t_im = q * lt + half_lt + l
            for c in range(S5_CB):
                idx = pl.ds(c, bsz, stride=S5_ROW_STRIDE)
                xp_sc[t_re, idx, :] = xr
                xp_sc[t_im, idx, :] = xi
                sr = s_sc[t_re, idx, :]
                si = s_sc[t_im, idx, :]
                xr, xi = a_r * xr - a_i * xi + sr, a_r * xi + a_i * xr + si
            cr_sc[:, cols] = xr
            ci_sc[:, cols] = xi

    def xp_tile(i):
        return jnp.concatenate([xp_sc[i, b * S5_ROW_STRIDE:b * S5_ROW_STRIDE + S5_CB, :] for b in range(bsz)], axis=0)

    for q in range(nq):
        xpq = jnp.concatenate([xp_tile(q * lt + l) for l in range(lt)], axis=-1).astype(BF16)
        yq_sc[q] += jnp.dot(xpq, omat_ref[q], preferred_element_type=F32)

    gw = gw_ref[...]
    for t in range(S5_T):
        yt = jnp.concatenate([yq_sc[q, :, t * qch:(t + 1) * qch] for q in range(nq)], axis=-1)
        y = _gelu_tanh(yt + d_ref[...] * u[t])
        z = jnp.dot(y.astype(BF16), gw, preferred_element_type=F32) + gb_ref[...]
        out = y * _sigmoid(z)
        for b in range(bsz):
            for l in range(D_SSM // LANES):
                y_sc[l, pl.ds(b * S5_T * S5_CB + t, S5_CB, stride=S5_T), :] = (
                    out[b * S5_CB:(b + 1) * S5_CB, l * LANES:(l + 1) * LANES])
    span = S5_T * S5_CB
    for b in range(bsz):
        y_ref[b] = jnp.concatenate([y_sc[l, b * span:(b + 1) * span, :] for l in range(D_SSM // LANES)],
                                   axis=-1).astype(BF16)


def _s5_mixer(u, ops, d_skip, glu_w, glu_b):
    _, bsz, nchunk, _ = u.shape
    seq_len = nchunk * S5_T
    toe, smat, omat, atr, ati = ops
    rows = bsz * S5_CB
    state_w = 2 * SSM_GROUPS * SSM_STATE
    assert S5_T * S5_QUAD * SSM_GROUP == MXU_TILE and S5_QUAD * SSM_STATE == MXU_TILE
    return pl.pallas_call(
        functools.partial(_s5_kernel, bsz=bsz),
        grid=(nchunk // S5_CB,),
        in_specs=[
            pl.BlockSpec((S5_T, bsz, S5_CB, D_SSM), lambda i: (0, 0, i, 0)),
            _const_spec(toe.shape), _const_spec(smat.shape), _const_spec(omat.shape),
            _const_spec(atr.shape), _const_spec(ati.shape),
            _const_spec((1, D_SSM)), _const_spec((D_SSM, D_SSM)), _const_spec((1, D_SSM)),
        ],
        out_specs=pl.BlockSpec((bsz, S5_T * S5_CB, D_SSM), lambda i: (0, i, 0)),
        out_shape=jax.ShapeDtypeStruct((bsz, seq_len, D_SSM), BF16),
        scratch_shapes=[
            pltpu.VMEM((SSM_GROUPS // S5_QUAD, rows, MXU_TILE), F32),
            pltpu.VMEM((state_w // LANES, bsz * S5_ROW_STRIDE, LANES), F32),
            pltpu.VMEM((state_w // LANES, bsz * S5_ROW_STRIDE, LANES), F32),
            pltpu.VMEM((bsz, state_w // 2), F32),
            pltpu.VMEM((bsz, state_w // 2), F32),
            pltpu.VMEM((D_SSM // LANES, rows * S5_T, LANES), F32),
        ],
        compiler_params=_params(1),
        name="s5_mixer",
    )(u, toe, smat, omat, atr, ati, d_skip.reshape(1, D_SSM), glu_w.astype(BF16), glu_b.reshape(1, D_SSM))


def _rel_bucket_np(dist):
    n = np.maximum(dist, 0)
    max_exact = REL_BUCKETS // 2
    nf = np.maximum(n, 1).astype(np.float64)
    large = max_exact + (np.log(nf / max_exact) / math.log(REL_MAX_DIST / max_exact)
                         * (REL_BUCKETS - max_exact)).astype(np.int64)
    large = np.minimum(large, REL_BUCKETS - 1)
    return np.where(n < max_exact, n, large).astype(np.int32)


def _bias_bucket_tiles():
    ko = np.arange(MOBA_BLOCK)[:, None]
    qo = np.arange(MOBA_BLOCK)[None, :]
    own = np.where(qo >= ko, _rel_bucket_np(qo - ko), -1)
    prev = _rel_bucket_np(qo - ko + MOBA_BLOCK)
    return np.stack([own, prev]).astype(np.int32)


assert int(_rel_bucket_np(np.arange(MOBA_BLOCK + 1, 8 * MOBA_BLOCK)).min()) == REL_BUCKETS - 1


def _bias_kernel(tab_ref, idx_ref, o_ref):
    h = pl.program_id(0)
    for t in range(2):
        idx = idx_ref[t]
        acc = jnp.full(idx.shape, NEG, F32)
        for b in range(REL_BUCKETS):
            acc = jnp.where(idx == b, tab_ref[h, b] * LOG2E, acc)
        o_ref[0, t] = acc


def _bias_tiles(rel_bias):
    idx = jnp.asarray(_bias_bucket_tiles())
    blk = MOBA_BLOCK
    return pl.pallas_call(
        _bias_kernel,
        grid=(N_HEADS,),
        in_specs=[
            pl.BlockSpec(memory_space=pltpu.SMEM),
            pl.BlockSpec((2, blk, blk), lambda h: (0, 0, 0)),
        ],
        out_specs=pl.BlockSpec((1, 2, blk, blk), lambda h: (h, 0, 0, 0)),
        out_shape=jax.ShapeDtypeStruct((N_HEADS, 2, blk, blk), F32),
        compiler_params=_params(1),
        name="moba_bias_tiles",
    )(rel_bias.T, idx)


def _attn_kernel(far_ref, q_ref, qn_ref, k_ref, v_ref, bias_ref, o_ref,
                 kmean_sc, vt_sc, mfar_sc, msel_sc, rows_sc, s0_sc, s1_sc, gmax0_sc, gmax1_sc, m_sc, acc_sc, *, nb):
    s_bufs = (s0_sc, s1_sc)
    gmax_bufs = (gmax0_sc, gmax1_sc)
    hp = pl.program_id(1)
    qt = pl.program_id(2)
    n_tiles = nb // 2
    blk = MOBA_BLOCK
    qw = 2 * blk
    qlane = lax.broadcasted_iota(jnp.int32, (1, qw), 1)
    lane = lax.broadcasted_iota(jnp.int32, (1, 2 * HEAD_DIM), 1)
    head_mask = [lane < HEAD_DIM, lane >= HEAD_DIM]
    nt = (((1,), (1,)), ((), ()))

    @pl.when(qt == 0)
    def _():
        for j in range(nb):
            kb = k_ref[0, j * blk:(j + 1) * blk, :].astype(F32)
            km = jnp.mean(kb, axis=0, keepdims=True)
            for hh in range(2):
                kmean_sc[hh, j:j + 1, :] = jnp.where(head_mask[hh], km, 0.0)
            vt = v_ref[0, j * blk:(j + 1) * blk, :].astype(F32).T.astype(BF16)
            ones_row = jnp.where(lax.broadcasted_iota(jnp.int32, (PV_ROWS - HEAD_DIM, blk), 0) == 0, 1.0, 0.0)
            for hh in range(2):
                vt_sc[hh, j, 0:HEAD_DIM, :] = vt[hh * HEAD_DIM:(hh + 1) * HEAD_DIM, :]
                vt_sc[hh, j, HEAD_DIM:, :] = ones_row.astype(BF16)

    jidx = lax.broadcasted_iota(jnp.int32, (nb, qw), 0)

    def pv(hh, j, p):
        return jnp.dot(vt_sc[hh, j], p.astype(BF16), preferred_element_type=F32)

    def mask_row(ref, hh, j):
        return jnp.where(j >= 0, ref[hh, pl.ds(jnp.maximum(j, 0), 1), :], NEG)

    class Tile:
        def __init__(self, t, q2):
            self.q2 = q2
            self.top = 2 * t + 1
            self.own = 2 * t + jnp.where(qlane >= blk, 1, 0)
            self.n_groups = self.top // ATTN_GROUP + 1
            self.qm = [jnp.where(head_mask[hh], q2, jnp.zeros_like(q2)) for hh in range(2)]

        def select(self):
            for hh in range(2):
                gate = lax.dot_general(kmean_sc[hh].astype(BF16), self.q2, nt, preferred_element_type=F32)
                rank = jnp.zeros((nb, qw), F32)
                for jp in range(nb):
                    row = gate[jp:jp + 1, :]
                    beats = (row > gate) | ((row == gate) & (jidx > jp))
                    rank = rank + jnp.where(beats & (self.own > jp), 1.0, 0.0)
                sel = (rank < float(MOBA_TOPK)) & (jidx < self.own)
                mfar_sc[hh] = jnp.where(sel, far_ref[2 * hp + hh] * LOG2E, NEG)
                msel_sc[hh] = jnp.where(sel, 0.0, NEG)

        def scores(self, hh, j):
            kb = k_ref[0, pl.ds(pl.multiple_of(j * blk, blk), blk), :]
            return lax.dot_general(kb, self.qm[hh], nt, preferred_element_type=F32)

        def block_ids(self, g):
            js = [self.top - ATTN_GROUP * g - jj for jj in range(ATTN_GROUP)]
            return js, [jnp.maximum(j, 0) for j in js]

        def row_term(self, g, jj, hh, j):
            far = mask_row(mfar_sc, hh, j)
            if jj > 2:
                return far
            second = qlane >= blk
            if jj == 0:
                special = jnp.where(second, 0.0, NEG)
            elif jj == 1:
                special = jnp.where(second, mask_row(msel_sc, hh, j), 0.0)
            else:
                special = jnp.where(second, far, mask_row(msel_sc, hh, j))
            return jnp.where(g == 0, special, far)

    class PhaseA:
        def __init__(self, tile, g, buf, first):
            self.tile, self.g, self.first = tile, g, first
            self.js, self.jc = tile.block_ids(g)
            self.s_buf, self.gmax_buf = s_bufs[buf], gmax_bufs[buf]
            self.gmax = [None, None]

        def block(self, hh, jj):
            s = self.tile.scores(hh, self.jc[jj])
            if self.first and jj < 3:
                tiles = [(None, 0), (0, 1), (1, None)][jj]
                parts = [s[:, h * blk:(h + 1) * blk] if t is None else s[:, h * blk:(h + 1) * blk] + bias_ref[hh, t]
                         for h, t in enumerate(tiles)]
                s = jnp.concatenate(parts, axis=1)
            self.s_buf[hh, jj] = s
            cm = jnp.max(s, axis=0, keepdims=True) + self.tile.row_term(self.g, jj, hh, self.js[jj])
            self.gmax[hh] = cm if self.gmax[hh] is None else jnp.maximum(self.gmax[hh], cm)

        def finish(self):
            for hh in range(2):
                self.gmax_buf[hh] = self.gmax[hh]

    class PhaseB:
        def __init__(self, tile, g, buf, stash_rows=False):
            self.tile, self.g, self.stash_rows = tile, g, stash_rows
            self.js, self.jc = tile.block_ids(g)
            self.s_buf = s_bufs[buf]
            self.m_old = [m_sc[hh] for hh in range(2)]
            self.m_new = [jnp.maximum(self.m_old[hh], gmax_bufs[buf][hh]) for hh in range(2)]
            self.acc = [None, None]
            if stash_rows:
                for hh in range(2):
                    for jj in range(ATTN_GROUP):
                        rows_sc[hh, jj] = tile.row_term(g, jj, hh, self.js[jj])

        def block(self, hh, jj):
            row = rows_sc[hh, jj] if self.stash_rows else self.tile.row_term(self.g, jj, hh, self.js[jj])
            p = jnp.exp2(self.s_buf[hh, jj] - (self.m_new[hh] - row))
            pa = pv(hh, self.jc[jj], p)
            self.acc[hh] = pa if self.acc[hh] is None else self.acc[hh] + pa

        def finish(self):
            for hh in range(2):
                alpha = jnp.exp2(self.m_old[hh] - self.m_new[hh])
                m_sc[hh] = self.m_new[hh]
                acc_sc[hh] = alpha * acc_sc[hh] + self.acc[hh]

    def run(*phases):
        for hh in range(2):
            for jj in range(ATTN_GROUP):
                for ph in phases:
                    ph.block(hh, jj)
        for ph in phases:
            ph.finish()

    cur = Tile(qt, q_ref[0])

    @pl.when(qt == 0)
    def _():
        cur.select()
        run(PhaseA(cur, 0, 0, True))

    for hh in range(2):
        m_sc[hh] = gmax0_sc[hh]
        acc_sc[hh] = jnp.zeros((PV_ROWS, qw), F32)

    def pipelined(g, carry):
        for p in range(2):
            @pl.when(g % 2 == p)
            def _():
                run(PhaseA(cur, g + 1, 1 - p, False), PhaseB(cur, g, p))

        return carry

    lax.fori_loop(0, cur.n_groups - 1, pipelined, 0)

    def write_out():
        out = jnp.concatenate([acc_sc[hh, 0:HEAD_DIM, :] / acc_sc[hh, HEAD_DIM:HEAD_DIM + 1, :] for hh in range(2)],
                              axis=0)
        o_ref[0] = out.T.astype(BF16)

    last = cur.n_groups - 1
    for p in range(2):
        @pl.when((last % 2 == p) & (qt < n_tiles - 1))
        def _():
            nxt = Tile(qt + 1, qn_ref[0])
            tail = PhaseB(cur, last, p, stash_rows=True)
            nxt.select()
            run(tail, PhaseA(nxt, 0, 0, True))
            write_out()

        @pl.when((last % 2 == p) & (qt == n_tiles - 1))
        def _():
            run(PhaseB(cur, last, p))
            write_out()


def _moba_attention(q, k, v, bias_tiles, far_bias):
    bsz, seq_len, _ = q.shape
    blk = MOBA_BLOCK
    nb = seq_len // blk
    pair = 2 * HEAD_DIM
    qw = 2 * blk
    assert nb % 2 == 0
    return pl.pallas_call(
        functools.partial(_attn_kernel, nb=nb),
        grid=(bsz, N_HEADS // 2, nb // 2),
        in_specs=[
            pl.BlockSpec(memory_space=pltpu.SMEM),
            pl.BlockSpec((1, qw, pair), lambda b, h, i: (b, i, h)),
            pl.BlockSpec((1, qw, pair), lambda b, h, i: (b, jnp.minimum(i + 1, nb // 2 - 1), h)),
            pl.BlockSpec((1, seq_len, pair), lambda b, h, i: (b, 0, h)),
            pl.BlockSpec((1, seq_len, pair), lambda b, h, i: (b, 0, h)),
            pl.BlockSpec((2, 2, blk, blk), lambda b, h, i: (h, 0, 0, 0)),
        ],
        out_specs=pl.BlockSpec((1, qw, pair), lambda b, h, i: (b, i, h)),
        out_shape=jax.ShapeDtypeStruct((bsz, seq_len, D_ATTN), BF16),
        scratch_shapes=[
            pltpu.VMEM((2, nb, pair), F32),
            pltpu.VMEM((2, nb, PV_ROWS, blk), BF16),
            pltpu.VMEM((2, nb, qw), F32),
            pltpu.VMEM((2, nb, qw), F32),
            pltpu.VMEM((2, ATTN_GROUP, 1, qw), F32),
            pltpu.VMEM((2, ATTN_GROUP, blk, qw), F32),
            pltpu.VMEM((2, ATTN_GROUP, blk, qw), F32),
            pltpu.VMEM((2, 1, qw), F32),
            pltpu.VMEM((2, 1, qw), F32),
            pltpu.VMEM((2, 1, qw), F32),
            pltpu.VMEM((2, PV_ROWS, qw), F32),
        ],
        compiler_params=_params(3),
        name="moba_attention",
    )(far_bias, q, q, k, v, bias_tiles)


def _ffn_body(x, g_ref, sc_ref, sh_ref, gate_ref, wg_ref, wu_ref, cw_ref, wd_ref, fin_ref, o_ref,
              gext_sc, carry_sc, acc_sc, h_sc, *, tm, per_seq, final_norm):
    @pl.when(pl.program_id(0) % per_seq == 0)
    def _():
        carry_sc[...] = jnp.zeros_like(carry_sc)

    h_sc[...] = _norm_mod(x, g_ref[...], sc_ref[0], sh_ref[0]).astype(BF16)

    def gate_up(j):
        cols = slice(j * FFN_CHUNK, (j + 1) * FFN_CHUNK)
        h = h_sc[...]
        return (jnp.dot(h, wg_ref[:, cols], preferred_element_type=F32),
                jnp.dot(h, wu_ref[:, cols], preferred_element_type=F32))

    nxt = gate_up(0)
    for j in range(N_FFN_CHUNKS):
        cols = slice(j * FFN_CHUNK, (j + 1) * FFN_CHUNK)
        gpre, up = nxt
        if j + 1 < N_FFN_CHUNKS:
            nxt = gate_up(j + 1)
        lane_tiles = FFN_CHUNK // LANES
        for l in range(lane_tiles):
            gext_sc[j, l, 0:FFN_HALO, :] = carry_sc[j, :, l * LANES:(l + 1) * LANES]
            gext_sc[j, l, FFN_HALO:, :] = gpre[:, l * LANES:(l + 1) * LANES]
        carry_sc[j] = gpre[tm - FFN_HALO:, :]

        def back(n):
            return jnp.concatenate([gext_sc[j, l, FFN_HALO - n:FFN_HALO - n + tm, :] for l in range(lane_tiles)],
                                   axis=-1)

        cw = cw_ref[:, cols]
        conv = cw[0:1] * back(2) + cw[1:2] * back(1) + cw[2:3] * gpre + cw[3:4]
        act = (_silu(conv) * up).astype(BF16)
        part = jnp.dot(act, wd_ref[cols, :], preferred_element_type=F32)
        if j == 0:
            acc_sc[...] = part
        else:
            acc_sc[...] += part
    out = x + gate_ref[0] * acc_sc[...]
    if final_norm:
        ms = jnp.mean(out * out, axis=-1, keepdims=True)
        out = out * lax.rsqrt(ms + EPS) * fin_ref[...]
    o_ref[...] = out


def _ffn0_kernel(x_ref, ys_ref, ya_ref, wo_ref, g1_ref, g_ref, sc_ref, sh_ref, gate_ref,
                 wg_ref, wu_ref, cw_ref, wd_ref, fin_ref, o_ref, gext_sc, carry_sc, acc_sc, h_sc, **kw):
    y = (jnp.dot(ys_ref[...], wo_ref[0:D_SSM, :], preferred_element_type=F32)
         + jnp.dot(ya_ref[...], wo_ref[D_SSM:, :], preferred_element_type=F32))
    x = x_ref[...] + g1_ref[0] * y
    _ffn_body(x, g_ref, sc_ref, sh_ref, gate_ref, wg_ref, wu_ref, cw_ref, wd_ref, fin_ref, o_ref,
              gext_sc, carry_sc, acc_sc, h_sc, **kw)


def _ffn1_kernel(x_ref, g_ref, sc_ref, sh_ref, gate_ref, wg_ref, wu_ref, cw_ref, wd_ref, fin_ref,
                 o_ref, gext_sc, carry_sc, acc_sc, h_sc, **kw):
    _ffn_body(x_ref[...], g_ref, sc_ref, sh_ref, gate_ref, wg_ref, wu_ref, cw_ref, wd_ref, fin_ref, o_ref,
              gext_sc, carry_sc, acc_sc, h_sc, **kw)


def _ffn_weights(w_up, w_gate, dw_w, dw_b, w_down, layer):
    cw = jnp.concatenate([dw_w[layer], dw_b[layer][None, :]], axis=0)
    return w_gate.astype(BF16), w_up.astype(BF16), cw, w_down.astype(BF16), layer


def _conv_ffn(x2, mixer, norm_g, scale, shift, gate, weights, final_g, seq_len, final_norm, tm=256):
    tok, d = x2.shape
    per_seq = seq_len // tm
    wg, wu, cw, wd, layer = weights
    row = lambda i: (i, 0)
    bat = lambda i: (i // per_seq, 0, 0)
    vec = pl.BlockSpec((1, 1, d), bat)
    layer_spec = lambda w: pl.BlockSpec((None,) + w.shape[1:], lambda i: (layer, 0, 0), pipeline_mode=pl.Buffered(1))
    common_specs = [_const_spec((1, d)), vec, vec, vec,
                    layer_spec(wg), layer_spec(wu), _const_spec(cw.shape), layer_spec(wd),
                    _const_spec((1, d))]
    common_args = [norm_g, scale, shift, gate, wg, wu, cw, wd, final_g]
    kw = dict(tm=tm, per_seq=per_seq, final_norm=final_norm)
    if mixer is None:
        body = functools.partial(_ffn1_kernel, **kw)
        specs = [pl.BlockSpec((tm, d), row)] + common_specs
        args = [x2] + common_args
    else:
        ys, ya, wo, g1 = mixer
        body = functools.partial(_ffn0_kernel, **kw)
        specs = [pl.BlockSpec((tm, d), row), pl.BlockSpec((tm, D_SSM), row), pl.BlockSpec((tm, D_ATTN), row),
                 _const_spec(wo.shape), vec] + common_specs
        args = [x2, ys, ya, wo, g1] + common_args
    return pl.pallas_call(
        body,
        grid=(tok // tm,),
        in_specs=specs,
        out_specs=pl.BlockSpec((tm, d), row),
        out_shape=jax.ShapeDtypeStruct((tok, d), F32),
        scratch_shapes=[
            pltpu.VMEM((N_FFN_CHUNKS, FFN_CHUNK // LANES, tm + FFN_HALO, LANES), F32),
            pltpu.VMEM((N_FFN_CHUNKS, FFN_HALO, FFN_CHUNK), F32),
            pltpu.VMEM((tm, d), F32),
            pltpu.VMEM((tm, d), BF16),
        ],
        compiler_params=_params(1),
        name="conv_ffn_final" if final_norm else "conv_ffn",
    )(*args)


def _conformer_kernel(x_ref, g_ref, sc_ref, sh_ref, gate_ref, win_ref, bin_ref, dw_ref, dwb_ref,
                      lng_ref, lnb_ref, wout_ref, bout_ref, o_ref, aext_sc, conv_sc, h_sc, *, tm, per_seq):
    d = x_ref.shape[-1]
    lanes = d // LANES
    x = x_ref[...]
    h_sc[...] = _norm_mod(x, g_ref[...], sc_ref[0], sh_ref[0]).astype(BF16)

    @pl.when(pl.program_id(0) % per_seq == 0)
    def _():
        aext_sc[:, 0:CONV_HALO, :] = jnp.zeros((lanes, CONV_HALO, LANES), F32)

    rc = 32
    off = CONV_HALO - (CONV_WIDTH - 1)
    for c0 in range(0, d, CONV_CHUNK):
        cc = slice(c0, c0 + CONV_CHUNK)
        gc = slice(d + c0, d + c0 + CONV_CHUNK)
        h = h_sc[...]
        a = ((jnp.dot(h, win_ref[:, cc], preferred_element_type=F32) + bin_ref[:, cc])
             * _sigmoid(jnp.dot(h, win_ref[:, gc], preferred_element_type=F32) + bin_ref[:, gc]))
        for l in range(c0 // LANES, (c0 + CONV_CHUNK) // LANES):
            cols = slice(l * LANES, (l + 1) * LANES)
            aext_sc[l, CONV_HALO:, :] = a[:, l * LANES - c0:(l + 1) * LANES - c0]
            for base in range(0, tm, rc):
                acc = [dwb_ref[:, cols]] * (rc // 8)
                for k in range(CONV_WIDTH):
                    w8 = dw_ref[k, :, cols]
                    for j in range(rc // 8):
                        lo = base + 8 * j + off + k
                        acc[j] = acc[j] + w8 * aext_sc[l, lo:lo + 8, :]
                for j in range(rc // 8):
                    conv_sc[base + 8 * j:base + 8 * j + 8, cols] = acc[j]
            aext_sc[l, 0:CONV_HALO, :] = aext_sc[l, tm:tm + CONV_HALO, :]

    c = conv_sc[...]
    mu = jnp.mean(c, axis=-1, keepdims=True)
    xc = c - mu
    y = xc * lax.rsqrt(jnp.mean(xc * xc, axis=-1, keepdims=True) + EPS)
    y = _silu(y * lng_ref[...] + lnb_ref[...]).astype(BF16)
    out = jnp.dot(y, wout_ref[...], preferred_element_type=F32) + bout_ref[...]
    o_ref[...] = x + gate_ref[0] * out


def _conformer(x2, norm_g, scale, shift, gate, w_in, b_in, dw_w, dw_b, ln_g, ln_b, w_out, b_out, seq_len, tm=512):
    tok, d = x2.shape
    per_seq = seq_len // tm
    row = lambda i: (i, 0)
    vec = pl.BlockSpec((1, 1, d), lambda i: (i // per_seq, 0, 0))
    return pl.pallas_call(
        functools.partial(_conformer_kernel, tm=tm, per_seq=per_seq),
        grid=(tok // tm,),
        in_specs=[pl.BlockSpec((tm, d), row), _const_spec((1, d)), vec, vec, vec,
                  _const_spec((d, 2 * d)), _const_spec((1, 2 * d)), _const_spec((CONV_WIDTH, 8, d)), _const_spec((8, d)),
                  _const_spec((1, d)), _const_spec((1, d)), _const_spec((d, d)), _const_spec((1, d))],
        out_specs=pl.BlockSpec((tm, d), row),
        out_shape=jax.ShapeDtypeStruct((tok, d), F32),
        scratch_shapes=[pltpu.VMEM((d // LANES, tm + CONV_HALO, LANES), F32), pltpu.VMEM((tm, d), F32),
                        pltpu.VMEM((tm, d), BF16)],
        compiler_params=_params(1),
        name="conformer_conv",
    )(x2, norm_g, scale, shift, gate, w_in.astype(BF16), b_in.reshape(1, -1),
      jnp.broadcast_to(dw_w[:, None, :], (CONV_WIDTH, 8, d)), jnp.broadcast_to(dw_b[None, :], (8, d)),
      ln_g.reshape(1, -1), ln_b.reshape(1, -1), w_out.astype(BF16), b_out.reshape(1, -1))


def kernel(x, c, mod_w, mod_b, norm_g, final_g, ab_w_in, ssm_a_re, ssm_a_im, ssm_log_dt, ssm_b_re, ssm_b_im, ssm_c_re, ssm_c_im, ssm_d, ssm_glu_w, ssm_glu_b, ab_w_out, rel_bias, cm_w_in, cm_b_in, cm_dw_w, cm_dw_b, cm_ln_g, cm_ln_b, cm_w_out, cm_b_out, ffn_w_up, ffn_w_gate, ffn_dw_w, ffn_dw_b, ffn_w_down):
    bsz, seq_len, d = x.shape
    tok = bsz * seq_len
    x2 = x.reshape(tok, d)
    mod = _modulation(c, mod_w, mod_b)
    vecs = [[mod[l, :, i * d:(i + 1) * d].reshape(bsz, 1, d) for i in range(6)] for l in range(2)]
    fin = final_g.reshape(1, d)

    sh1, sc1, g1, sh2, sc2, g2 = vecs[0]
    u, q, k, v = _in_projection(x2, norm_g[0, 0].reshape(1, d), sc1, sh1, ab_w_in[0].astype(BF16), seq_len)
    ops = _s5_prepare(ssm_a_re[0], ssm_a_im[0], ssm_log_dt[0], ssm_b_re[0], ssm_b_im[0], ssm_c_re[0], ssm_c_im[0])
    y_ssm = _s5_mixer(u.reshape(S5_T, bsz, seq_len // S5_T, D_SSM), ops, ssm_d[0], ssm_glu_w[0], ssm_glu_b[0])
    att = lambda a: a.reshape(bsz, seq_len, D_ATTN)
    y_att = _moba_attention(att(q), att(k), att(v), _bias_tiles(rel_bias), rel_bias[REL_BUCKETS - 1])
    w0 = _ffn_weights(ffn_w_up, ffn_w_gate, ffn_dw_w, ffn_dw_b, ffn_w_down, 0)
    x2 = _conv_ffn(x2, (y_ssm.reshape(tok, D_SSM), y_att.reshape(tok, D_ATTN), ab_w_out[0].astype(BF16), g1),
                   norm_g[0, 1].reshape(1, d), sc2, sh2, g2, w0, fin, seq_len, final_norm=False)

    sh1, sc1, g1, sh2, sc2, g2 = vecs[1]
    x2 = _conformer(x2, norm_g[1, 0].reshape(1, d), sc1, sh1, g1, cm_w_in[0], cm_b_in[0], cm_dw_w[0], cm_dw_b[0],
                    cm_ln_g[0], cm_ln_b[0], cm_w_out[0], cm_b_out[0], seq_len)
    w1 = _ffn_weights(ffn_w_up, ffn_w_gate, ffn_dw_w, ffn_dw_b, ffn_w_down, 1)
    x2 = _conv_ffn(x2, None, norm_g[1, 1].reshape(1, d), sc2, sh2, g2, w1, fin, seq_len, final_norm=True)
    return x2.reshape(bsz, seq_len, d)
```

```python
import functools
import math

import numpy as np
import jax
import jax.numpy as jnp
from jax import lax
from jax.experimental import pallas as pl
from jax.experimental.pallas import tpu as pltpu

F32 = jnp.float32
BF16 = jnp.bfloat16

D_MODEL = 1024
D_SSM = 512
SSM_GROUP = 16
SSM_GROUPS = 32
SSM_STATE = 64
D_ATTN = 512
HEAD_DIM = 64
N_HEADS = 8
MOBA_BLOCK = 256
MOBA_TOPK = 3
REL_BUCKETS = 32
REL_MAX_DIST = 128
CONV_WIDTH = 31
FFN_HIDDEN = 2816
FFN_CONV_WIDTH = 3
EPS = 1e-6

NEG = -1e30
LOG2E = math.log2(math.e)

V7X_VMEM_BYTES = 64 * 1024 * 1024
VMEM_LIMIT = V7X_VMEM_BYTES - 8 * 1024 * 1024

S5_T = 4
S5_CB = 32
S5_ROW_STRIDE = S5_CB + 8
LANES = 128
MXU_TILE = 256
S5_QUAD = 4
FFN_CHUNK = 256
N_FFN_CHUNKS = FFN_HIDDEN // FFN_CHUNK
CONV_HALO = 32
CONV_CHUNK = 256
FFN_HALO = 8
ATTN_GROUP = 4
PV_ROWS = HEAD_DIM + 16


def _sigmoid(x):
    return 0.5 * jnp.tanh(0.5 * x) + 0.5


def _silu(x):
    return x * _sigmoid(x)


def _gelu_tanh(x):
    c = math.sqrt(2.0 / math.pi)
    return 0.5 * x * (1.0 + jnp.tanh(c * (x + 0.044715 * (x * x * x))))


def _norm_mod(x, g, scale, shift):
    ms = jnp.mean(x * x, axis=-1, keepdims=True)
    y = x * lax.rsqrt(ms + EPS) * g
    return y * (1.0 + scale) + shift


def _params(n_axes, vmem=VMEM_LIMIT, flags=None):
    return pltpu.CompilerParams(dimension_semantics=("arbitrary",) * n_axes, vmem_limit_bytes=vmem, flags=flags)


def _const_spec(shape):
    nd = len(shape)
    return pl.BlockSpec(shape, lambda *_: (0,) * nd, pipeline_mode=pl.Buffered(1))


def _mod_kernel(c_ref, w_ref, b_ref, o_ref):
    c = c_ref[...]
    cs = _silu(c).astype(BF16)
    o_ref[0] = jnp.dot(cs, w_ref[0].astype(BF16), preferred_element_type=F32) + b_ref[0]


def _modulation(c, mod_w, mod_b):
    depth, d, n = mod_w.shape
    bsz = c.shape[0]
    nt = 1536
    return pl.pallas_call(
        _mod_kernel,
        grid=(depth, n // nt),
        in_specs=[
            pl.BlockSpec((bsz, d), lambda l, j: (0, 0)),
            pl.BlockSpec((1, d, nt), lambda l, j: (l, 0, j)),
            pl.BlockSpec((1, 1, nt), lambda l, j: (l, 0, j)),
        ],
        out_specs=pl.BlockSpec((1, bsz, nt), lambda l, j: (l, 0, j)),
        out_shape=jax.ShapeDtypeStruct((depth, bsz, n), F32),
        compiler_params=_params(2),
        name="modulation",
    )(c, mod_w, mod_b.reshape(depth, 1, n))


def _inproj_kernel(x_ref, g_ref, sc_ref, sh_ref, w_ref, u_ref, q_ref, k_ref, v_ref, u_sc):
    h = _norm_mod(x_ref[...], g_ref[...], sc_ref[0], sh_ref[0]).astype(BF16)
    p = jnp.dot(h, w_ref[...], preferred_element_type=F32)
    tm = p.shape[0]
    lanes = D_SSM // LANES
    for l in range(lanes):
        u_sc[l] = p[:, l * LANES:(l + 1) * LANES]
    for s in range(S5_T):
        u_ref[s] = jnp.concatenate([u_sc[l, pl.ds(s, tm // S5_T, stride=S5_T), :] for l in range(lanes)], axis=-1)
    q_ref[...] = (p[:, D_SSM:D_SSM + D_ATTN] * (HEAD_DIM ** -0.5 * LOG2E)).astype(BF16)
    k_ref[...] = p[:, D_SSM + D_ATTN:D_SSM + 2 * D_ATTN].astype(BF16)
    v_ref[...] = p[:, D_SSM + 2 * D_ATTN:].astype(BF16)


def _in_projection(x2, g, scale, shift, w, seq_len, tm=1024):
    tok, d = x2.shape
    per_seq = seq_len // tm
    n = w.shape[1]
    row = lambda i: (i, 0)
    bat = lambda i: (i // per_seq, 0, 0)
    return pl.pallas_call(
        _inproj_kernel,
        grid=(tok // tm,),
        in_specs=[
            pl.BlockSpec((tm, d), row),
            _const_spec((1, d)),
            pl.BlockSpec((1, 1, d), bat),
            pl.BlockSpec((1, 1, d), bat),
            _const_spec((d, n)),
        ],
        out_specs=[
            pl.BlockSpec((S5_T, tm // S5_T, D_SSM), lambda i: (0, i, 0)),
            pl.BlockSpec((tm, D_ATTN), row),
            pl.BlockSpec((tm, D_ATTN), row),
            pl.BlockSpec((tm, D_ATTN), row),
        ],
        out_shape=[
            jax.ShapeDtypeStruct((S5_T, tok // S5_T, D_SSM), F32),
            jax.ShapeDtypeStruct((tok, D_ATTN), BF16),
            jax.ShapeDtypeStruct((tok, D_ATTN), BF16),
            jax.ShapeDtypeStruct((tok, D_ATTN), BF16),
        ],
        scratch_shapes=[pltpu.VMEM((D_SSM // LANES, tm, LANES), F32)],
        compiler_params=_params(1),
        name="in_projection",
    )(x2, g, scale, shift, w)


def _s5_prep_kernel(lre_r, lim_r, ldt_r, lre_c, lim_c, ldt_c, btr, bti, cre, cim, ctr, cti,
                    kt_ref, sre_ref, sim_ref, ore_ref, oim_ref, at_ref):
    def discretise(lre, lim, ldt):
        dt = jnp.exp(ldt)
        mag = jnp.exp(lre * dt)
        return mag * jnp.cos(lim * dt), mag * jnp.sin(lim * dt)

    lre, lim = lre_r[...], lim_r[...]
    ar, ai = discretise(lre, lim, ldt_r[...])
    den = lre * lre + lim * lim
    nr = ar - 1.0
    coef_re = (nr * lre + ai * lim) / den
    coef_im = (ai * lre - nr * lim) / den
    br, bi = btr[...], bti[...]
    zr = coef_re * br - coef_im * bi
    zi = coef_re * bi + coef_im * br
    c_re, c_im = cre[...], cim[...]
    for k in range(S5_T):
        sre_ref[S5_T - 1 - k] = zr
        sim_ref[S5_T - 1 - k] = zi
        for h in range(SSM_GROUP):
            kt_ref[k, h] = jnp.sum(c_re[:, h:h + 1, :] * zr - c_im[:, h:h + 1, :] * zi, axis=-1)
        zr, zi = ar * zr - ai * zi, ar * zi + ai * zr

    acr, aci = discretise(lre_c[...], lim_c[...], ldt_c[...])
    pr, pi = acr, aci
    ct_re, ct_im = ctr[...], cti[...]
    for t in range(S5_T):
        ore_ref[t] = ct_re * pr - ct_im * pi
        oim_ref[t] = -ct_re * pi - ct_im * pr
        pr, pi = acr * pr - aci * pi, acr * pi + aci * pr

    qr, qi = ar, ai
    for _ in range(S5_T - 1):
        qr, qi = ar * qr - ai * qi, ar * qi + ai * qr
    at_ref[0] = qr
    at_ref[1] = qi


def _s5_prepare(a_re, a_im, log_dt, b_re, b_im, c_re, c_im):
    g, p, h, t = SSM_GROUPS, SSM_STATE, SSM_GROUP, S5_T
    ins = [
        a_re.reshape(g, 1, p), a_im.reshape(g, 1, p), log_dt.reshape(g, 1, 1),
        a_re.reshape(g, p, 1), a_im.reshape(g, p, 1), log_dt.reshape(g, 1, 1),
        b_re.transpose(0, 2, 1), b_im.transpose(0, 2, 1), c_re, c_im,
        c_re.transpose(0, 2, 1), c_im.transpose(0, 2, 1),
    ]
    full = lambda s: pl.BlockSpec(s, lambda: (0,) * len(s))
    out_shapes = [(t, h, g, h), (t, g, h, p), (t, g, h, p), (t, g, p, h), (t, g, p, h), (2, g, 1, p)]
    kt, sre, sim, ore, oim, at = pl.pallas_call(
        _s5_prep_kernel,
        in_specs=[full(a.shape) for a in ins],
        out_specs=[full(s) for s in out_shapes],
        out_shape=[jax.ShapeDtypeStruct(s, F32) for s in out_shapes],
        name="s5_prepare",
    )(*ins)

    q4 = S5_QUAD
    nq = g // q4
    eye = jnp.eye(q4, dtype=F32)
    ktg = kt.transpose(0, 2, 3, 1).reshape(t, nq, q4, h, h)
    steps = jnp.arange(t)
    lag_is = ((steps[None, :] - steps[:, None])[None] == steps[:, None, None]).astype(F32)
    toe = jnp.einsum("kst,kqaxy,ac->qsaxtcy", lag_is, ktg, eye).reshape(nq, t * q4 * h, t * q4 * h)

    def s_quads(s):
        return jnp.einsum("sqaxp,ac->qsaxcp", s.reshape(t, nq, q4, h, p), eye).reshape(nq, t * q4 * h, q4 * p)

    def o_quads(o):
        return jnp.einsum("tqapy,ac->qaptcy", o.reshape(t, nq, q4, p, h), eye).reshape(nq, q4 * p, t * q4 * h)

    smat = jnp.concatenate([s_quads(sre), s_quads(sim)], axis=-1)
    omat = jnp.concatenate([o_quads(ore), o_quads(oim)], axis=-2)
    atr = at[0].reshape(1, g * p)
    ati = at[1].reshape(1, g * p)
    return toe.astype(BF16), smat.astype(BF16), omat.astype(BF16), atr, ati


def _s5_kernel(u_ref, toe_ref, smat_ref, omat_ref, atr_ref, ati_ref, d_ref, gw_ref, gb_ref,
               y_ref, yq_sc, s_sc, xp_sc, cr_sc, ci_sc, y_sc, *, bsz):
    rows = bsz * S5_CB
    nq = SSM_GROUPS // S5_QUAD
    qch = S5_QUAD * SSM_GROUP
    qst = S5_QUAD * SSM_STATE

    @pl.when(pl.program_id(0) == 0)
    def _():
        cr_sc[...] = jnp.zeros_like(cr_sc)
        ci_sc[...] = jnp.zeros_like(ci_sc)

    u = [u_ref[s].reshape(rows, D_SSM) for s in range(S5_T)]
    xq = [jnp.concatenate([u[s][:, q * qch:(q + 1) * qch] for s in range(S5_T)], axis=-1).astype(BF16)
          for q in range(nq)]

    lt = 2 * qst // LANES
    for q in range(nq):
        st = jnp.dot(xq[q], smat_ref[q], preferred_element_type=F32)
        for l in range(lt):
            for b in range(bsz):
                s_sc[q * lt + l, b * S5_ROW_STRIDE:b * S5_ROW_STRIDE + S5_CB, :] = (
                    st[b * S5_CB:(b + 1) * S5_CB, l * LANES:(l + 1) * LANES])

    for q in range(nq):
        yq_sc[q] = jnp.dot(xq[q], toe_ref[q], preferred_element_type=F32)

    half_lt = lt // 2
    for q in range(nq):
        for l in range(half_lt):
            cols = slice(q * qst + l * LANES, q * qst + (l + 1) * LANES)
            a_r = atr_ref[:, cols]
            a_i = ati_ref[:, cols]
            xr = cr_sc[:, cols]
            xi = ci_sc[:, cols]
            t_re = q * lt + l
            t_im = q * lt + half_lt + l
            for c in range(S5_CB):
                idx = pl.ds(c, bsz, stride=S5_ROW_STRIDE)
                xp_sc[t_re, idx, :] = xr
                xp_sc[t_im, idx, :] = xi
                sr = s_sc[t_re, idx, :]
                si = s_sc[t_im, idx, :]
                xr, xi = a_r * xr - a_i * xi + sr, a_r * xi + a_i * xr + si
            cr_sc[:, cols] = xr
            ci_sc[:, cols] = xi

    def xp_tile(i):
        return jnp.concatenate([xp_sc[i, b * S5_ROW_STRIDE:b * S5_ROW_STRIDE + S5_CB, :] for b in range(bsz)], axis=0)

    for q in range(nq):
        xpq = jnp.concatenate([xp_tile(q * lt + l) for l in range(lt)], axis=-1).astype(BF16)
        yq_sc[q] += jnp.dot(xpq, omat_ref[q], preferred_element_type=F32)

    gw = gw_ref[...]
    for t in range(S5_T):
        yt = jnp.concatenate([yq_sc[q, :, t * qch:(t + 1) * qch] for q in range(nq)], axis=-1)
        y = _gelu_tanh(yt + d_ref[...] * u[t])
        z = jnp.dot(y.astype(BF16), gw, preferred_element_type=F32) + gb_ref[...]
        out = y * _sigmoid(z)
        for b in range(bsz):
            for l in range(D_SSM // LANES):
                y_sc[l, pl.ds(b * S5_T * S5_CB + t, S5_CB, stride=S5_T), :] = (
                    out[b * S5_CB:(b + 1) * S5_CB, l * LANES:(l + 1) * LANES])
    span = S5_T * S5_CB
    for b in range(bsz):
        y_ref[b] = jnp.concatenate([y_sc[l, b * span:(b + 1) * span, :] for l in range(D_SSM // LANES)],
                                   axis=-1).astype(BF16)


def _s5_mixer(u, ops, d_skip, glu_w, glu_b):
    _, bsz, nchunk, _ = u.shape
    seq_len = nchunk * S5_T
    toe, smat, omat, atr, ati = ops
    rows = bsz * S5_CB
    state_w = 2 * SSM_GROUPS * SSM_STATE
    assert S5_T * S5_QUAD * SSM_GROUP == MXU_TILE and S5_QUAD * SSM_STATE == MXU_TILE
    return pl.pallas_call(
        functools.partial(_s5_kernel, bsz=bsz),
        grid=(nchunk // S5_CB,),
        in_specs=[
            pl.BlockSpec((S5_T, bsz, S5_CB, D_SSM), lambda i: (0, 0, i, 0)),
            _const_spec(toe.shape), _const_spec(smat.shape), _const_spec(omat.shape),
            _const_spec(atr.shape), _const_spec(ati.shape),
            _const_spec((1, D_SSM)), _const_spec((D_SSM, D_SSM)), _const_spec((1, D_SSM)),
        ],
        out_specs=pl.BlockSpec((bsz, S5_T * S5_CB, D_SSM), lambda i: (0, i, 0)),
        out_shape=jax.ShapeDtypeStruct((bsz, seq_len, D_SSM), BF16),
        scratch_shapes=[
            pltpu.VMEM((SSM_GROUPS // S5_QUAD, rows, MXU_TILE), F32),
            pltpu.VMEM((state_w // LANES, bsz * S5_ROW_STRIDE, LANES), F32),
            pltpu.VMEM((state_w // LANES, bsz * S5_ROW_STRIDE, LANES), F32),
            pltpu.VMEM((bsz, state_w // 2), F32),
            pltpu.VMEM((bsz, state_w // 2), F32),
            pltpu.VMEM((D_SSM // LANES, rows * S5_T, LANES), F32),
        ],
        compiler_params=_params(1),
        name="s5_mixer",
    )(u, toe, smat, omat, atr, ati, d_skip.reshape(1, D_SSM), glu_w.astype(BF16), glu_b.reshape(1, D_SSM))


def _rel_bucket_np(dist):
    n = np.maximum(dist, 0)
    max_exact = REL_BUCKETS // 2
    nf = np.maximum(n, 1).astype(np.float64)
    large = max_exact + (np.log(nf / max_exact) / math.log(REL_MAX_DIST / max_exact)
                         * (REL_BUCKETS - max_exact)).astype(np.int64)
    large = np.minimum(large, REL_BUCKETS - 1)
    return np.where(n < max_exact, n, large).astype(np.int32)


def _bias_bucket_tiles():
    ko = np.arange(MOBA_BLOCK)[:, None]
    qo = np.arange(MOBA_BLOCK)[None, :]
    own = np.where(qo >= ko, _rel_bucket_np(qo - ko), -1)
    prev = _rel_bucket_np(qo - ko + MOBA_BLOCK)
    return np.stack([own, prev]).astype(np.int32)


assert int(_rel_bucket_np(np.arange(MOBA_BLOCK + 1, 8 * MOBA_BLOCK)).min()) == REL_BUCKETS - 1


def _bias_kernel(tab_ref, idx_ref, o_ref):
    h = pl.program_id(0)
    for t in range(2):
        idx = idx_ref[t]
        acc = jnp.full(idx.shape, NEG, F32)
        for b in range(REL_BUCKETS):
            acc = jnp.where(idx == b, tab_ref[h, b] * LOG2E, acc)
        o_ref[0, t] = acc


def _bias_tiles(rel_bias):
    idx = jnp.asarray(_bias_bucket_tiles())
    blk = MOBA_BLOCK
    return pl.pallas_call(
        _bias_kernel,
        grid=(N_HEADS,),
        in_specs=[
            pl.BlockSpec(memory_space=pltpu.SMEM),
            pl.BlockSpec((2, blk, blk), lambda h: (0, 0, 0)),
        ],
        out_specs=pl.BlockSpec((1, 2, blk, blk), lambda h: (h, 0, 0, 0)),
        out_shape=jax.ShapeDtypeStruct((N_HEADS, 2, blk, blk), F32),
        compiler_params=_params(1),
        name="moba_bias_tiles",
    )(rel_bias.T, idx)


def _attn_kernel(far_ref, q_ref, qn_ref, k_ref, v_ref, bias_ref, o_ref,
                 kmean_sc, vt_sc, mfar_sc, msel_sc, rows_sc, s0_sc, s1_sc, gmax0_sc, gmax1_sc, m_sc, acc_sc, *, nb):
    s_bufs = (s0_sc, s1_sc)
    gmax_bufs = (gmax0_sc, gmax1_sc)
    hp = pl.program_id(1)
    qt = pl.program_id(2)
    n_tiles = nb // 2
    blk = MOBA_BLOCK
    qw = 2 * blk
    qlane = lax.broadcasted_iota(jnp.int32, (1, qw), 1)
    lane = lax.broadcasted_iota(jnp.int32, (1, 2 * HEAD_DIM), 1)
    head_mask = [lane < HEAD_DIM, lane >= HEAD_DIM]
    nt = (((1,), (1,)), ((), ()))

    @pl.when(qt == 0)
    def _():
        for j in range(nb):
            kb = k_ref[0, j * blk:(j + 1) * blk, :].astype(F32)
            km = jnp.mean(kb, axis=0, keepdims=True)
            for hh in range(2):
                kmean_sc[hh, j:j + 1, :] = jnp.where(head_mask[hh], km, 0.0)
            vt = v_ref[0, j * blk:(j + 1) * blk, :].astype(F32).T.astype(BF16)
            ones_row = jnp.where(lax.broadcasted_iota(jnp.int32, (PV_ROWS - HEAD_DIM, blk), 0) == 0, 1.0, 0.0)
            for hh in range(2):
                vt_sc[hh, j, 0:HEAD_DIM, :] = vt[hh * HEAD_DIM:(hh + 1) * HEAD_DIM, :]
                vt_sc[hh, j, HEAD_DIM:, :] = ones_row.astype(BF16)

    jidx = lax.broadcasted_iota(jnp.int32, (nb, qw), 0)

    def pv(hh, j, p):
        return jnp.dot(vt_sc[hh, j], p.astype(BF16), preferred_element_type=F32)

    def mask_row(ref, hh, j):
        return jnp.where(j >= 0, ref[hh, pl.ds(jnp.maximum(j, 0), 1), :], NEG)

    class Tile:
        def __init__(self, t, q2):
            self.q2 = q2
            self.top = 2 * t + 1
            self.own = 2 * t + jnp.where(qlane >= blk, 1, 0)
            self.n_groups = self.top // ATTN_GROUP + 1
            self.qm = [jnp.where(head_mask[hh], q2, jnp.zeros_like(q2)) for hh in range(2)]

        def select(self):
            for hh in range(2):
                gate = lax.dot_general(kmean_sc[hh].astype(BF16), self.q2, nt, preferred_element_type=F32)
                rank = jnp.zeros((nb, qw), F32)
                for jp in range(nb):
                    row = gate[jp:jp + 1, :]
                    beats = (row > gate) | ((row == gate) & (jidx > jp))
                    rank = rank + jnp.where(beats & (self.own > jp), 1.0, 0.0)
                sel = (rank < float(MOBA_TOPK)) & (jidx < self.own)
                mfar_sc[hh] = jnp.where(sel, far_ref[2 * hp + hh] * LOG2E, NEG)
                msel_sc[hh] = jnp.where(sel, 0.0, NEG)

        def scores(self, hh, j):
            kb = k_ref[0, pl.ds(pl.multiple_of(j * blk, blk), blk), :]
            return lax.dot_general(kb, self.qm[hh], nt, preferred_element_type=F32)

        def block_ids(self, g):
            js = [self.top - ATTN_GROUP * g - jj for jj in range(ATTN_GROUP)]
            return js, [jnp.maximum(j, 0) for j in js]

        def row_term(self, g, jj, hh, j):
            far = mask_row(mfar_sc, hh, j)
            if jj > 2:
                return far
            second = qlane >= blk
            if jj == 0:
                special = jnp.where(second, 0.0, NEG)
            elif jj == 1:
                special = jnp.where(second, mask_row(msel_sc, hh, j), 0.0)
            else:
                special = jnp.where(second, far, mask_row(msel_sc, hh, j))
            return jnp.where(g == 0, special, far)

    class PhaseA:
        def __init__(self, tile, g, buf, first):
            self.tile, self.g, self.first = tile, g, first
            self.js, self.jc = tile.block_ids(g)
            self.s_buf, self.gmax_buf = s_bufs[buf], gmax_bufs[buf]
            self.gmax = [None, None]

        def block(self, hh, jj):
            s = self.tile.scores(hh, self.jc[jj])
            if self.first and jj < 3:
                tiles = [(None, 0), (0, 1), (1, None)][jj]
                parts = [s[:, h * blk:(h + 1) * blk] if t is None else s[:, h * blk:(h + 1) * blk] + bias_ref[hh, t]
                         for h, t in enumerate(tiles)]
                s = jnp.concatenate(parts, axis=1)
            self.s_buf[hh, jj] = s
            cm = jnp.max(s, axis=0, keepdims=True) + self.tile.row_term(self.g, jj, hh, self.js[jj])
            self.gmax[hh] = cm if self.gmax[hh] is None else jnp.maximum(self.gmax[hh], cm)

        def finish(self):
            for hh in range(2):
                self.gmax_buf[hh] = self.gmax[hh]

    class PhaseB:
        def __init__(self, tile, g, buf, stash_rows=False):
            self.tile, self.g, self.stash_rows = tile, g, stash_rows
            self.js, self.jc = tile.block_ids(g)
            self.s_buf = s_bufs[buf]
            self.m_old = [m_sc[hh] for hh in range(2)]
            self.m_new = [jnp.maximum(self.m_old[hh], gmax_bufs[buf][hh]) for hh in range(2)]
            self.acc = [None, None]
            if stash_rows:
                for hh in range(2):
                    for jj in range(ATTN_GROUP):
                        rows_sc[hh, jj] = tile.row_term(g, jj, hh, self.js[jj])

        def block(self, hh, jj):
            row = rows_sc[hh, jj] if self.stash_rows else self.tile.row_term(self.g, jj, hh, self.js[jj])
            p = jnp.exp2(self.s_buf[hh, jj] - (self.m_new[hh] - row))
            pa = pv(hh, self.jc[jj], p)
            self.acc[hh] = pa if self.acc[hh] is None else self.acc[hh] + pa

        def finish(self):
            for hh in range(2):
                alpha = jnp.exp2(self.m_old[hh] - self.m_new[hh])
                m_sc[hh] = self.m_new[hh]
                acc_sc[hh] = alpha * acc_sc[hh] + self.acc[hh]

    def run(*phases):
        for hh in range(2):
            for jj in range(ATTN_GROUP):
                for ph in phases:
                    ph.block(hh, jj)
        for ph in phases:
            ph.finish()

    cur = Tile(qt, q_ref[0])

    @pl.when(qt == 0)
    def _():
        cur.select()
        run(PhaseA(cur, 0, 0, True))

    for hh in range(2):
        m_sc[hh] = gmax0_sc[hh]
        acc_sc[hh] = jnp.zeros((PV_ROWS, qw), F32)

    def pipelined(g, carry):
        for p in range(2):
            @pl.when(g % 2 == p)
            def _():
                run(PhaseA(cur, g + 1, 1 - p, False), PhaseB(cur, g, p))

        return carry

    lax.fori_loop(0, cur.n_groups - 1, pipelined, 0)

    def write_out():
        out = jnp.concatenate([acc_sc[hh, 0:HEAD_DIM, :] / acc_sc[hh, HEAD_DIM:HEAD_DIM + 1, :] for hh in range(2)],
                              axis=0)
        o_ref[0] = out.T.astype(BF16)

    last = cur.n_groups - 1
    for p in range(2):
        @pl.when((last % 2 == p) & (qt < n_tiles - 1))
        def _():
            nxt = Tile(qt + 1, qn_ref[0])
            tail = PhaseB(cur, last, p, stash_rows=True)
            nxt.select()
            run(tail, PhaseA(nxt, 0, 0, True))
            write_out()

        @pl.when((last % 2 == p) & (qt == n_tiles - 1))
        def _():
            run(PhaseB(cur, last, p))
            write_out()


def _moba_attention(q, k, v, bias_tiles, far_bias):
    bsz, seq_len, _ = q.shape
    blk = MOBA_BLOCK
    nb = seq_len // blk
    pair = 2 * HEAD_DIM
    qw = 2 * blk
    assert nb % 2 == 0
    return pl.pallas_call(
        functools.partial(_attn_kernel, nb=nb),
        grid=(bsz, N_HEADS // 2, nb // 2),
        in_specs=[
            pl.BlockSpec(memory_space=pltpu.SMEM),
            pl.BlockSpec((1, qw, pair), lambda b, h, i: (b, i, h)),
            pl.BlockSpec((1, qw, pair), lambda b, h, i: (b, jnp.minimum(i + 1, nb // 2 - 1), h)),
            pl.BlockSpec((1, seq_len, pair), lambda b, h, i: (b, 0, h)),
            pl.BlockSpec((1, seq_len, pair), lambda b, h, i: (b, 0, h)),
            pl.BlockSpec((2, 2, blk, blk), lambda b, h, i: (h, 0, 0, 0)),
        ],
        out_specs=pl.BlockSpec((1, qw, pair), lambda b, h, i: (b, i, h)),
        out_shape=jax.ShapeDtypeStruct((bsz, seq_len, D_ATTN), BF16),
        scratch_shapes=[
            pltpu.VMEM((2, nb, pair), F32),
            pltpu.VMEM((2, nb, PV_ROWS, blk), BF16),
            pltpu.VMEM((2, nb, qw), F32),
            pltpu.VMEM((2, nb, qw), F32),
            pltpu.VMEM((2, ATTN_GROUP, 1, qw), F32),
            pltpu.VMEM((2, ATTN_GROUP, blk, qw), F32),
            pltpu.VMEM((2, ATTN_GROUP, blk, qw), F32),
            pltpu.VMEM((2, 1, qw), F32),
            pltpu.VMEM((2, 1, qw), F32),
            pltpu.VMEM((2, 1, qw), F32),
            pltpu.VMEM((2, PV_ROWS, qw), F32),
        ],
        compiler_params=_params(3),
        name="moba_attention",
    )(far_bias, q, q, k, v, bias_tiles)


def _ffn_body(x, g_ref, sc_ref, sh_ref, gate_ref, wg_ref, wu_ref, cw_ref, wd_ref, fin_ref, o_ref,
              gext_sc, carry_sc, acc_sc, h_sc, *, tm, per_seq, final_norm):
    @pl.when(pl.program_id(0) % per_seq == 0)
    def _():
        carry_sc[...] = jnp.zeros_like(carry_sc)

    h_sc[...] = _norm_mod(x, g_ref[...], sc_ref[0], sh_ref[0]).astype(BF16)

    def gate_up(j):
        cols = slice(j * FFN_CHUNK, (j + 1) * FFN_CHUNK)
        h = h_sc[...]
        return (jnp.dot(h, wg_ref[:, cols], preferred_element_type=F32),
                jnp.dot(h, wu_ref[:, cols], preferred_element_type=F32))

    nxt = gate_up(0)
    for j in range(N_FFN_CHUNKS):
        cols = slice(j * FFN_CHUNK, (j + 1) * FFN_CHUNK)
        gpre, up = nxt
        if j + 1 < N_FFN_CHUNKS:
            nxt = gate_up(j + 1)
        lane_tiles = FFN_CHUNK // LANES
        for l in range(lane_tiles):
            gext_sc[j, l, 0:FFN_HALO, :] = carry_sc[j, :, l * LANES:(l + 1) * LANES]
            gext_sc[j, l, FFN_HALO:, :] = gpre[:, l * LANES:(l + 1) * LANES]
        carry_sc[j] = gpre[tm - FFN_HALO:, :]

        def back(n):
            return jnp.concatenate([gext_sc[j, l, FFN_HALO - n:FFN_HALO - n + tm, :] for l in range(lane_tiles)],
                                   axis=-1)

        cw = cw_ref[:, cols]
        half_c = cw[0:1] * back(2) + cw[1:2] * back(1) + cw[2:3] * gpre + cw[3:4]
        act = ((half_c * jnp.tanh(half_c) + half_c) * up).astype(BF16)
        part = jnp.dot(act, wd_ref[cols, :], preferred_element_type=F32)
        if j == 0:
            acc_sc[...] = part
        else:
            acc_sc[...] += part
    out = x + gate_ref[0] * acc_sc[...]
    if final_norm:
        ms = jnp.mean(out * out, axis=-1, keepdims=True)
        out = out * lax.rsqrt(ms + EPS) * fin_ref[...]
    o_ref[...] = out


def _ffn0_kernel(x_ref, ys_ref, ya_ref, wo_ref, g1_ref, g_ref, sc_ref, sh_ref, gate_ref,
                 wg_ref, wu_ref, cw_ref, wd_ref, fin_ref, o_ref, gext_sc, carry_sc, acc_sc, h_sc, **kw):
    y = (jnp.dot(ys_ref[...], wo_ref[0:D_SSM, :], preferred_element_type=F32)
         + jnp.dot(ya_ref[...], wo_ref[D_SSM:, :], preferred_element_type=F32))
    x = x_ref[...] + g1_ref[0] * y
    _ffn_body(x, g_ref, sc_ref, sh_ref, gate_ref, wg_ref, wu_ref, cw_ref, wd_ref, fin_ref, o_ref,
              gext_sc, carry_sc, acc_sc, h_sc, **kw)


def _ffn1_kernel(x_ref, g_ref, sc_ref, sh_ref, gate_ref, wg_ref, wu_ref, cw_ref, wd_ref, fin_ref,
                 o_ref, gext_sc, carry_sc, acc_sc, h_sc, **kw):
    _ffn_body(x_ref[...], g_ref, sc_ref, sh_ref, gate_ref, wg_ref, wu_ref, cw_ref, wd_ref, fin_ref, o_ref,
              gext_sc, carry_sc, acc_sc, h_sc, **kw)


def _ffn_weights(w_up, w_gate, dw_w, dw_b, w_down, layer):
    cw = 0.5 * jnp.concatenate([dw_w[layer], dw_b[layer][None, :]], axis=0)
    return w_gate.astype(BF16), w_up.astype(BF16), cw, w_down.astype(BF16), layer


def _conv_ffn(x2, mixer, norm_g, scale, shift, gate, weights, final_g, seq_len, final_norm, tm=256):
    tok, d = x2.shape
    per_seq = seq_len // tm
    wg, wu, cw, wd, layer = weights
    row = lambda i: (i, 0)
    bat = lambda i: (i // per_seq, 0, 0)
    vec = pl.BlockSpec((1, 1, d), bat)
    layer_spec = lambda w: pl.BlockSpec((None,) + w.shape[1:], lambda i: (layer, 0, 0), pipeline_mode=pl.Buffered(1))
    common_specs = [_const_spec((1, d)), vec, vec, vec,
                    layer_spec(wg), layer_spec(wu), _const_spec(cw.shape), layer_spec(wd),
                    _const_spec((1, d))]
    common_args = [norm_g, scale, shift, gate, wg, wu, cw, wd, final_g]
    kw = dict(tm=tm, per_seq=per_seq, final_norm=final_norm)
    if mixer is None:
        body = functools.partial(_ffn1_kernel, **kw)
        specs = [pl.BlockSpec((tm, d), row)] + common_specs
        args = [x2] + common_args
    else:
        ys, ya, wo, g1 = mixer
        body = functools.partial(_ffn0_kernel, **kw)
        specs = [pl.BlockSpec((tm, d), row), pl.BlockSpec((tm, D_SSM), row), pl.BlockSpec((tm, D_ATTN), row),
                 _const_spec(wo.shape), vec] + common_specs
        args = [x2, ys, ya, wo, g1] + common_args
    return pl.pallas_call(
        body,
        grid=(tok // tm,),
        in_specs=specs,
        out_specs=pl.BlockSpec((tm, d), row),
        out_shape=jax.ShapeDtypeStruct((tok, d), F32),
        scratch_shapes=[
            pltpu.VMEM((N_FFN_CHUNKS, FFN_CHUNK // LANES, tm + FFN_HALO, LANES), F32),
            pltpu.VMEM((N_FFN_CHUNKS, FFN_HALO, FFN_CHUNK), F32),
            pltpu.VMEM((tm, d), F32),
            pltpu.VMEM((tm, d), BF16),
        ],
        compiler_params=_params(1),
        name="conv_ffn_final" if final_norm else "conv_ffn",
    )(*args)


def _conformer_kernel(x_ref, g_ref, sc_ref, sh_ref, gate_ref, win_ref, bin_ref, dw_ref, dwb_ref,
                      lng_ref, lnb_ref, wout_ref, bout_ref, o_ref, aext_sc, conv_sc, h_sc, *, tm, per_seq):
    d = x_ref.shape[-1]
    lanes = d // LANES
    x = x_ref[...]
    h_sc[...] = _norm_mod(x, g_ref[...], sc_ref[0], sh_ref[0]).astype(BF16)

    @pl.when(pl.program_id(0) % per_seq == 0)
    def _():
        aext_sc[:, 0:CONV_HALO, :] = jnp.zeros((lanes, CONV_HALO, LANES), F32)

    rc = 32
    off = CONV_HALO - (CONV_WIDTH - 1)
    for c0 in range(0, d, CONV_CHUNK):
        cc = slice(c0, c0 + CONV_CHUNK)
        gc = slice(d + c0, d + c0 + CONV_CHUNK)
        h = h_sc[...]
        half_a = jnp.dot(h, win_ref[:, cc], preferred_element_type=F32) + bin_ref[:, cc]
        half_b = jnp.dot(h, win_ref[:, gc], preferred_element_type=F32) + bin_ref[:, gc]
        a = half_a * jnp.tanh(half_b) + half_a
        for l in range(c0 // LANES, (c0 + CONV_CHUNK) // LANES):
            cols = slice(l * LANES, (l + 1) * LANES)
            aext_sc[l, CONV_HALO:, :] = a[:, l * LANES - c0:(l + 1) * LANES - c0]
            for base in range(0, tm, rc):
                acc = [dwb_ref[:, cols]] * (rc // 8)
                for k in range(CONV_WIDTH):
                    w8 = dw_ref[k, :, cols]
                    for j in range(rc // 8):
                        lo = base + 8 * j + off + k
                        acc[j] = acc[j] + w8 * aext_sc[l, lo:lo + 8, :]
                for j in range(rc // 8):
                    conv_sc[base + 8 * j:base + 8 * j + 8, cols] = acc[j]
            aext_sc[l, 0:CONV_HALO, :] = aext_sc[l, tm:tm + CONV_HALO, :]

    c = conv_sc[...]
    mu = jnp.mean(c, axis=-1, keepdims=True)
    xc = c - mu
    y = xc * lax.rsqrt(jnp.mean(xc * xc, axis=-1, keepdims=True) + EPS)
    half_v = y * lng_ref[...] + lnb_ref[...]
    y = (half_v * jnp.tanh(half_v) + half_v).astype(BF16)
    out = jnp.dot(y, wout_ref[...], preferred_element_type=F32) + bout_ref[...]
    o_ref[...] = x + gate_ref[0] * out


def _conformer(x2, norm_g, scale, shift, gate, w_in, b_in, dw_w, dw_b, ln_g, ln_b, w_out, b_out, seq_len, tm=512):
    tok, d = x2.shape
    per_seq = seq_len // tm
    row = lambda i: (i, 0)
    vec = pl.BlockSpec((1, 1, d), lambda i: (i // per_seq, 0, 0))
    return pl.pallas_call(
        functools.partial(_conformer_kernel, tm=tm, per_seq=per_seq),
        grid=(tok // tm,),
        in_specs=[pl.BlockSpec((tm, d), row), _const_spec((1, d)), vec, vec, vec,
                  _const_spec((d, 2 * d)), _const_spec((1, 2 * d)), _const_spec((CONV_WIDTH, 8, d)), _const_spec((8, d)),
                  _const_spec((1, d)), _const_spec((1, d)), _const_spec((d, d)), _const_spec((1, d))],
        out_specs=pl.BlockSpec((tm, d), row),
        out_shape=jax.ShapeDtypeStruct((tok, d), F32),
        scratch_shapes=[pltpu.VMEM((d // LANES, tm + CONV_HALO, LANES), F32), pltpu.VMEM((tm, d), F32),
                        pltpu.VMEM((tm, d), BF16)],
        compiler_params=_params(1),
        name="conformer_conv",
    )(x2, norm_g, scale, shift, gate, (0.5 * w_in).astype(BF16), 0.5 * b_in.reshape(1, -1),
      jnp.broadcast_to(dw_w[:, None, :], (CONV_WIDTH, 8, d)), jnp.broadcast_to(dw_b[None, :], (8, d)),
      0.5 * ln_g.reshape(1, -1), 0.5 * ln_b.reshape(1, -1), w_out.astype(BF16), b_out.reshape(1, -1))


def kernel(x, c, mod_w, mod_b, norm_g, final_g, ab_w_in, ssm_a_re, ssm_a_im, ssm_log_dt, ssm_b_re, ssm_b_im, ssm_c_re, ssm_c_im, ssm_d, ssm_glu_w, ssm_glu_b, ab_w_out, rel_bias, cm_w_in, cm_b_in, cm_dw_w, cm_dw_b, cm_ln_g, cm_ln_b, cm_w_out, cm_b_out, ffn_w_up, ffn_w_gate, ffn_dw_w, ffn_dw_b, ffn_w_down):
    bsz, seq_len, d = x.shape
    tok = bsz * seq_len
    x2 = x.reshape(tok, d)
    mod = _modulation(c, mod_w, mod_b)
    vecs = [[mod[l, :, i * d:(i + 1) * d].reshape(bsz, 1, d) for i in range(6)] for l in range(2)]
    fin = final_g.reshape(1, d)

    sh1, sc1, g1, sh2, sc2, g2 = vecs[0]
    u, q, k, v = _in_projection(x2, norm_g[0, 0].reshape(1, d), sc1, sh1, ab_w_in[0].astype(BF16), seq_len)
    ops = _s5_prepare(ssm_a_re[0], ssm_a_im[0], ssm_log_dt[0], ssm_b_re[0], ssm_b_im[0], ssm_c_re[0], ssm_c_im[0])
    y_ssm = _s5_mixer(u.reshape(S5_T, bsz, seq_len // S5_T, D_SSM), ops, ssm_d[0], ssm_glu_w[0], ssm_glu_b[0])
    att = lambda a: a.reshape(bsz, seq_len, D_ATTN)
    y_att = _moba_attention(att(q), att(k), att(v), _bias_tiles(rel_bias), rel_bias[REL_BUCKETS - 1])
    w0 = _ffn_weights(ffn_w_up, ffn_w_gate, ffn_dw_w, ffn_dw_b, ffn_w_down, 0)
    x2 = _conv_ffn(x2, (y_ssm.reshape(tok, D_SSM), y_att.reshape(tok, D_ATTN), ab_w_out[0].astype(BF16), g1),
                   norm_g[0, 1].reshape(1, d), sc2, sh2, g2, w0, fin, seq_len, final_norm=False)

    sh1, sc1, g1, sh2, sc2, g2 = vecs[1]
    x2 = _conformer(x2, norm_g[1, 0].reshape(1, d), sc1, sh1, g1, cm_w_in[0], cm_b_in[0], cm_dw_w[0], cm_dw_b[0],
                    cm_ln_g[0], cm_ln_b[0], cm_w_out[0], cm_b_out[0], seq_len)
    w1 = _ffn_weights(ffn_w_up, ffn_w_gate, ffn_dw_w, ffn_dw_b, ffn_w_down, 1)
    x2 = _conv_ffn(x2, None, norm_g[1, 1].reshape(1, d), sc2, sh2, g2, w1, fin, seq_len, final_norm=True)
    return x2.reshape(bsz, seq_len, d)
```

```python
import functools
import math

import numpy as np
import jax
import jax.numpy as jnp
from jax import lax
from jax.experimental import pallas as pl
from jax.experimental.pallas import tpu as pltpu

F32 = jnp.float32
BF16 = jnp.bfloat16

D_MODEL = 1024
D_SSM = 512
SSM_GROUP = 16
SSM_GROUPS = 32
SSM_STATE = 64
D_ATTN = 512
HEAD_DIM = 64
N_HEADS = 8
MOBA_BLOCK = 256
MOBA_TOPK = 3
REL_BUCKETS = 32
REL_MAX_DIST = 128
CONV_WIDTH = 31
FFN_HIDDEN = 2816
FFN_CONV_WIDTH = 3
EPS = 1e-6

NEG = -1e30
LOG2E = math.log2(math.e)

V7X_VMEM_BYTES = 64 * 1024 * 1024
VMEM_LIMIT = V7X_VMEM_BYTES - 8 * 1024 * 1024

S5_T = 4
S5_CB = 32
S5_ROW_STRIDE = S5_CB + 8
LANES = 128
MXU_TILE = 256
S5_QUAD = 4
FFN_CHUNK = 256
N_FFN_CHUNKS = FFN_HIDDEN // FFN_CHUNK
CONV_HALO = 32
CONV_CHUNK = 256
FFN_HALO = 8
ATTN_GROUP = 4
PV_ROWS = HEAD_DIM + 16


def _sigmoid(x):
    return 0.5 * jnp.tanh(0.5 * x) + 0.5


def _silu(x):
    return x * _sigmoid(x)


def _gelu_tanh(x):
    c = math.sqrt(2.0 / math.pi)
    return 0.5 * x * (1.0 + jnp.tanh(c * (x + 0.044715 * (x * x * x))))


def _norm_mod(x, g, scale, shift):
    ms = jnp.mean(x * x, axis=-1, keepdims=True)
    y = x * lax.rsqrt(ms + EPS) * g
    return y * (1.0 + scale) + shift


def _params(n_axes, vmem=VMEM_LIMIT, flags=None):
    return pltpu.CompilerParams(dimension_semantics=("arbitrary",) * n_axes, vmem_limit_bytes=vmem, flags=flags)


def _const_spec(shape):
    nd = len(shape)
    return pl.BlockSpec(shape, lambda *_: (0,) * nd, pipeline_mode=pl.Buffered(1))


def _mod_kernel(c_ref, w_ref, b_ref, o_ref):
    c = c_ref[...]
    cs = _silu(c).astype(BF16)
    o_ref[0] = jnp.dot(cs, w_ref[0].astype(BF16), preferred_element_type=F32) + b_ref[0]


def _modulation(c, mod_w, mod_b):
    depth, d, n = mod_w.shape
    bsz = c.shape[0]
    nt = 1536
    return pl.pallas_call(
        _mod_kernel,
        grid=(depth, n // nt),
        in_specs=[
            pl.BlockSpec((bsz, d), lambda l, j: (0, 0)),
            pl.BlockSpec((1, d, nt), lambda l, j: (l, 0, j)),
            pl.BlockSpec((1, 1, nt), lambda l, j: (l, 0, j)),
        ],
        out_specs=pl.BlockSpec((1, bsz, nt), lambda l, j: (l, 0, j)),
        out_shape=jax.ShapeDtypeStruct((depth, bsz, n), F32),
        compiler_params=_params(2),
        name="modulation",
    )(c, mod_w, mod_b.reshape(depth, 1, n))


def _inproj_kernel(x_ref, g_ref, sc_ref, sh_ref, w_ref, u_ref, q_ref, k_ref, v_ref, u_sc):
    h = _norm_mod(x_ref[...], g_ref[...], sc_ref[0], sh_ref[0]).astype(BF16)
    p = jnp.dot(h, w_ref[...], preferred_element_type=F32)
    tm = p.shape[0]
    lanes = D_SSM // LANES
    for l in range(lanes):
        u_sc[l] = p[:, l * LANES:(l + 1) * LANES]
    for s in range(S5_T):
        u_ref[s] = jnp.concatenate([u_sc[l, pl.ds(s, tm // S5_T, stride=S5_T), :] for l in range(lanes)], axis=-1)
    q_ref[...] = (p[:, D_SSM:D_SSM + D_ATTN] * (HEAD_DIM ** -0.5 * LOG2E)).astype(BF16)
    k_ref[...] = p[:, D_SSM + D_ATTN:D_SSM + 2 * D_ATTN].astype(BF16)
    v_ref[...] = p[:, D_SSM + 2 * D_ATTN:].astype(BF16)


def _in_projection(x2, g, scale, shift, w, seq_len, tm=1024):
    tok, d = x2.shape
    per_seq = seq_len // tm
    n = w.shape[1]
    row = lambda i: (i, 0)
    bat = lambda i: (i // per_seq, 0, 0)
    return pl.pallas_call(
        _inproj_kernel,
        grid=(tok // tm,),
        in_specs=[
            pl.BlockSpec((tm, d), row),
            _const_spec((1, d)),
            pl.BlockSpec((1, 1, d), bat),
            pl.BlockSpec((1, 1, d), bat),
            _const_spec((d, n)),
        ],
        out_specs=[
            pl.BlockSpec((S5_T, tm // S5_T, D_SSM), lambda i: (0, i, 0)),
            pl.BlockSpec((tm, D_ATTN), row),
            pl.BlockSpec((tm, D_ATTN), row),
            pl.BlockSpec((tm, D_ATTN), row),
        ],
        out_shape=[
            jax.ShapeDtypeStruct((S5_T, tok // S5_T, D_SSM), F32),
            jax.ShapeDtypeStruct((tok, D_ATTN), BF16),
            jax.ShapeDtypeStruct((tok, D_ATTN), BF16),
            jax.ShapeDtypeStruct((tok, D_ATTN), BF16),
        ],
        scratch_shapes=[pltpu.VMEM((D_SSM // LANES, tm, LANES), F32)],
        compiler_params=_params(1),
        name="in_projection",
    )(x2, g, scale, shift, w)


def _s5_prep_kernel(lre_r, lim_r, ldt_r, lre_c, lim_c, ldt_c, btr, bti, cre, cim, ctr, cti,
                    kt_ref, sre_ref, sim_ref, ore_ref, oim_ref, at_ref):
    def discretise(lre, lim, ldt):
        dt = jnp.exp(ldt)
        mag = jnp.exp(lre * dt)
        return mag * jnp.cos(lim * dt), mag * jnp.sin(lim * dt)

    lre, lim = lre_r[...], lim_r[...]
    ar, ai = discretise(lre, lim, ldt_r[...])
    den = lre * lre + lim * lim
    nr = ar - 1.0
    coef_re = (nr * lre + ai * lim) / den
    coef_im = (ai * lre - nr * lim) / den
    br, bi = btr[...], bti[...]
    zr = coef_re * br - coef_im * bi
    zi = coef_re * bi + coef_im * br
    c_re, c_im = cre[...], cim[...]
    for k in range(S5_T):
        sre_ref[S5_T - 1 - k] = zr
        sim_ref[S5_T - 1 - k] = zi
        for h in range(SSM_GROUP):
            kt_ref[k, h] = jnp.sum(c_re[:, h:h + 1, :] * zr - c_im[:, h:h + 1, :] * zi, axis=-1)
        zr, zi = ar * zr - ai * zi, ar * zi + ai * zr

    acr, aci = discretise(lre_c[...], lim_c[...], ldt_c[...])
    pr, pi = acr, aci
    ct_re, ct_im = ctr[...], cti[...]
    for t in range(S5_T):
        ore_ref[t] = ct_re * pr - ct_im * pi
        oim_ref[t] = -ct_re * pi - ct_im * pr
        pr, pi = acr * pr - aci * pi, acr * pi + aci * pr

    qr, qi = ar, ai
    for _ in range(S5_T - 1):
        qr, qi = ar * qr - ai * qi, ar * qi + ai * qr
    at_ref[0] = qr
    at_ref[1] = qi


def _s5_prepare(a_re, a_im, log_dt, b_re, b_im, c_re, c_im):
    g, p, h, t = SSM_GROUPS, SSM_STATE, SSM_GROUP, S5_T
    ins = [
        a_re.reshape(g, 1, p), a_im.reshape(g, 1, p), log_dt.reshape(g, 1, 1),
        a_re.reshape(g, p, 1), a_im.reshape(g, p, 1), log_dt.reshape(g, 1, 1),
        b_re.transpose(0, 2, 1), b_im.transpose(0, 2, 1), c_re, c_im,
        c_re.transpose(0, 2, 1), c_im.transpose(0, 2, 1),
    ]
    full = lambda s: pl.BlockSpec(s, lambda: (0,) * len(s))
    out_shapes = [(t, h, g, h), (t, g, h, p), (t, g, h, p), (t, g, p, h), (t, g, p, h), (2, g, 1, p)]
    kt, sre, sim, ore, oim, at = pl.pallas_call(
        _s5_prep_kernel,
        in_specs=[full(a.shape) for a in ins],
        out_specs=[full(s) for s in out_shapes],
        out_shape=[jax.ShapeDtypeStruct(s, F32) for s in out_shapes],
        name="s5_prepare",
    )(*ins)

    q4 = S5_QUAD
    nq = g // q4
    eye = jnp.eye(q4, dtype=F32)
    ktg = kt.transpose(0, 2, 3, 1).reshape(t, nq, q4, h, h)
    steps = jnp.arange(t)
    lag_is = ((steps[None, :] - steps[:, None])[None] == steps[:, None, None]).astype(F32)
    toe = jnp.einsum("kst,kqaxy,ac->qsaxtcy", lag_is, ktg, eye).reshape(nq, t * q4 * h, t * q4 * h)

    def s_quads(s):
        return jnp.einsum("sqaxp,ac->qsaxcp", s.reshape(t, nq, q4, h, p), eye).reshape(nq, t * q4 * h, q4 * p)

    def o_quads(o):
        return jnp.einsum("tqapy,ac->qaptcy", o.reshape(t, nq, q4, p, h), eye).reshape(nq, q4 * p, t * q4 * h)

    smat = jnp.concatenate([s_quads(sre), s_quads(sim)], axis=-1)
    omat = jnp.concatenate([o_quads(ore), o_quads(oim)], axis=-2)
    atr = at[0].reshape(1, g * p)
    ati = at[1].reshape(1, g * p)
    return toe.astype(BF16), smat.astype(BF16), omat.astype(BF16), atr, ati


def _s5_kernel(u_ref, toe_ref, smat_ref, omat_ref, atr_ref, ati_ref, d_ref, gw_ref, gb_ref,
               y_ref, yq_sc, s_sc, xp_sc, cr_sc, ci_sc, y_sc, *, bsz):
    rows = bsz * S5_CB
    nq = SSM_GROUPS // S5_QUAD
    qch = S5_QUAD * SSM_GROUP
    qst = S5_QUAD * SSM_STATE

    @pl.when(pl.program_id(0) == 0)
    def _():
        cr_sc[...] = jnp.zeros_like(cr_sc)
        ci_sc[...] = jnp.zeros_like(ci_sc)

    u = [u_ref[s].reshape(rows, D_SSM) for s in range(S5_T)]
    xq = [jnp.concatenate([u[s][:, q * qch:(q + 1) * qch] for s in range(S5_T)], axis=-1).astype(BF16)
          for q in range(nq)]

    lt = 2 * qst // LANES
    for q in range(nq):
        st = jnp.dot(xq[q], smat_ref[q], preferred_element_type=F32)
        for l in range(lt):
            for b in range(bsz):
                s_sc[q * lt + l, b * S5_ROW_STRIDE:b * S5_ROW_STRIDE + S5_CB, :] = (
                    st[b * S5_CB:(b + 1) * S5_CB, l * LANES:(l + 1) * LANES])

    for q in range(nq):
        yq_sc[q] = jnp.dot(xq[q], toe_ref[q], preferred_element_type=F32)

    half_lt = lt // 2
    for q in range(nq):
        for l in range(half_lt):
            cols = slice(q * qst + l * LANES, q * qst + (l + 1) * LANES)
            a_r = atr_ref[:, cols]
            a_i = ati_ref[:, cols]
            xr = cr_sc[:, cols]
            xi = ci_sc[:, cols]
            t_re = q * lt + l
            t_im = q * lt + half_lt + l
            for c in range(S5_CB):
                idx = pl.ds(c, bsz, stride=S5_ROW_STRIDE)
                xp_sc[t_re, idx, :] = xr
                xp_sc[t_im, idx, :] = xi
                sr = s_sc[t_re, idx, :]
                si = s_sc[t_im, idx, :]
                xr, xi = a_r * xr - a_i * xi + sr, a_r * xi + a_i * xr + si
            cr_sc[:, cols] = xr
            ci_sc[:, cols] = xi

    def xp_tile(i):
        return jnp.concatenate([xp_sc[i, b * S5_ROW_STRIDE:b * S5_ROW_STRIDE + S5_CB, :] for b in range(bsz)], axis=0)

    for q in range(nq):
        xpq = jnp.concatenate([xp_tile(q * lt + l) for l in range(lt)], axis=-1).astype(BF16)
        yq_sc[q] += jnp.dot(xpq, omat_ref[q], preferred_element_type=F32)

    gw = gw_ref[...]
    for t in range(S5_T):
        yt = jnp.concatenate([yq_sc[q, :, t * qch:(t + 1) * qch] for q in range(nq)], axis=-1)
        y = _gelu_tanh(yt + d_ref[...] * u[t])
        z = jnp.dot(y.astype(BF16), gw, preferred_element_type=F32) + gb_ref[...]
        out = y * _sigmoid(z)
        for b in range(bsz):
            for l in range(D_SSM // LANES):
                y_sc[l, pl.ds(b * S5_T * S5_CB + t, S5_CB, stride=S5_T), :] = (
                    out[b * S5_CB:(b + 1) * S5_CB, l * LANES:(l + 1) * LANES])
    span = S5_T * S5_CB
    for b in range(bsz):
        y_ref[b] = jnp.concatenate([y_sc[l, b * span:(b + 1) * span, :] for l in range(D_SSM // LANES)],
                                   axis=-1).astype(BF16)


def _s5_mixer(u, ops, d_skip, glu_w, glu_b):
    _, bsz, nchunk, _ = u.shape
    seq_len = nchunk * S5_T
    toe, smat, omat, atr, ati = ops
    rows = bsz * S5_CB
    state_w = 2 * SSM_GROUPS * SSM_STATE
    assert S5_T * S5_QUAD * SSM_GROUP == MXU_TILE and S5_QUAD * SSM_STATE == MXU_TILE
    return pl.pallas_call(
        functools.partial(_s5_kernel, bsz=bsz),
        grid=(nchunk // S5_CB,),
        in_specs=[
            pl.BlockSpec((S5_T, bsz, S5_CB, D_SSM), lambda i: (0, 0, i, 0)),
            _const_spec(toe.shape), _const_spec(smat.shape), _const_spec(omat.shape),
            _const_spec(atr.shape), _const_spec(ati.shape),
            _const_spec((1, D_SSM)), _const_spec((D_SSM, D_SSM)), _const_spec((1, D_SSM)),
        ],
        out_specs=pl.BlockSpec((bsz, S5_T * S5_CB, D_SSM), lambda i: (0, i, 0)),
        out_shape=jax.ShapeDtypeStruct((bsz, seq_len, D_SSM), BF16),
        scratch_shapes=[
            pltpu.VMEM((SSM_GROUPS // S5_QUAD, rows, MXU_TILE), F32),
            pltpu.VMEM((state_w // LANES, bsz * S5_ROW_STRIDE, LANES), F32),
            pltpu.VMEM((state_w // LANES, bsz * S5_ROW_STRIDE, LANES), F32),
            pltpu.VMEM((bsz, state_w // 2), F32),
            pltpu.VMEM((bsz, state_w // 2), F32),
            pltpu.VMEM((D_SSM // LANES, rows * S5_T, LANES), F32),
        ],
        compiler_params=_params(1),
        name="s5_mixer",
    )(u, toe, smat, omat, atr, ati, d_skip.reshape(1, D_SSM), glu_w.astype(BF16), glu_b.reshape(1, D_SSM))


def _rel_bucket_np(dist):
    n = np.maximum(dist, 0)
    max_exact = REL_BUCKETS // 2
    nf = np.maximum(n, 1).astype(np.float64)
    large = max_exact + (np.log(nf / max_exact) / math.log(REL_MAX_DIST / max_exact)
                         * (REL_BUCKETS - max_exact)).astype(np.int64)
    large = np.minimum(large, REL_BUCKETS - 1)
    return np.where(n < max_exact, n, large).astype(np.int32)


def _bias_bucket_tiles():
    ko = np.arange(MOBA_BLOCK)[:, None]
    qo = np.arange(MOBA_BLOCK)[None, :]
    own = np.where(qo >= ko, _rel_bucket_np(qo - ko), -1)
    prev = _rel_bucket_np(qo - ko + MOBA_BLOCK)
    return np.stack([own, prev]).astype(np.int32)


assert int(_rel_bucket_np(np.arange(MOBA_BLOCK + 1, 8 * MOBA_BLOCK)).min()) == REL_BUCKETS - 1


def _bias_kernel(tab_ref, idx_ref, o_ref):
    h = pl.program_id(0)
    for t in range(2):
        idx = idx_ref[t]
        acc = jnp.full(idx.shape, NEG, F32)
        for b in range(REL_BUCKETS):
            acc = jnp.where(idx == b, tab_ref[h, b] * LOG2E, acc)
        o_ref[0, t] = acc


def _bias_tiles(rel_bias):
    idx = jnp.asarray(_bias_bucket_tiles())
    blk = MOBA_BLOCK
    return pl.pallas_call(
        _bias_kernel,
        grid=(N_HEADS,),
        in_specs=[
            pl.BlockSpec(memory_space=pltpu.SMEM),
            pl.BlockSpec((2, blk, blk), lambda h: (0, 0, 0)),
        ],
        out_specs=pl.BlockSpec((1, 2, blk, blk), lambda h: (h, 0, 0, 0)),
        out_shape=jax.ShapeDtypeStruct((N_HEADS, 2, blk, blk), F32),
        compiler_params=_params(1),
        name="moba_bias_tiles",
    )(rel_bias.T, idx)


def _attn_kernel(far_ref, q_ref, qn_ref, k_ref, v_ref, bias_ref, o_ref,
                 kmean_sc, vt_sc, mfar_sc, msel_sc, rows_sc, s0_sc, s1_sc, gmax0_sc, gmax1_sc, m_sc, acc_sc, *, nb):
    s_bufs = (s0_sc, s1_sc)
    gmax_bufs = (gmax0_sc, gmax1_sc)
    hp = pl.program_id(1)
    qt = pl.program_id(2)
    n_tiles = nb // 2
    blk = MOBA_BLOCK
    qw = 2 * blk
    qlane = lax.broadcasted_iota(jnp.int32, (1, qw), 1)
    lane = lax.broadcasted_iota(jnp.int32, (1, 2 * HEAD_DIM), 1)
    head_mask = [lane < HEAD_DIM, lane >= HEAD_DIM]
    nt = (((1,), (1,)), ((), ()))

    @pl.when(qt == 0)
    def _():
        for j in range(nb):
            kb = k_ref[0, j * blk:(j + 1) * blk, :].astype(F32)
            km = jnp.mean(kb, axis=0, keepdims=True)
            for hh in range(2):
                kmean_sc[hh, j:j + 1, :] = jnp.where(head_mask[hh], km, 0.0)
            vt = v_ref[0, j * blk:(j + 1) * blk, :].astype(F32).T.astype(BF16)
            ones_row = jnp.where(lax.broadcasted_iota(jnp.int32, (PV_ROWS - HEAD_DIM, blk), 0) == 0, 1.0, 0.0)
            for hh in range(2):
                vt_sc[hh, j, 0:HEAD_DIM, :] = vt[hh * HEAD_DIM:(hh + 1) * HEAD_DIM, :]
                vt_sc[hh, j, HEAD_DIM:, :] = ones_row.astype(BF16)

    jidx = lax.broadcasted_iota(jnp.int32, (nb, qw), 0)

    def pv(hh, j, p):
        return jnp.dot(vt_sc[hh, j], p.astype(BF16), preferred_element_type=F32)

    def mask_row(ref, hh, j):
        return jnp.where(j >= 0, ref[hh, pl.ds(jnp.maximum(j, 0), 1), :], NEG)

    class Tile:
        def __init__(self, t, q2):
            self.q2 = q2
            self.top = 2 * t + 1
            self.own = 2 * t + jnp.where(qlane >= blk, 1, 0)
            self.n_groups = self.top // ATTN_GROUP + 1
            self.qm = [jnp.where(head_mask[hh], q2, jnp.zeros_like(q2)) for hh in range(2)]

        def select(self):
            for hh in range(2):
                gate = lax.dot_general(kmean_sc[hh].astype(BF16), self.q2, nt, preferred_element_type=F32)
                rank = jnp.zeros((nb, qw), F32)
                for jp in range(nb):
                    row = gate[jp:jp + 1, :]
                    beats = (row > gate) | ((row == gate) & (jidx > jp))
                    rank = rank + jnp.where(beats & (self.own > jp), 1.0, 0.0)
                sel = (rank < float(MOBA_TOPK)) & (jidx < self.own)
                mfar_sc[hh] = jnp.where(sel, far_ref[2 * hp + hh] * LOG2E, NEG)
                msel_sc[hh] = jnp.where(sel, 0.0, NEG)

        def scores(self, hh, j):
            kb = k_ref[0, pl.ds(pl.multiple_of(j * blk, blk), blk), :]
            return lax.dot_general(kb, self.qm[hh], nt, preferred_element_type=F32)

        def block_ids(self, g):
            js = [self.top - ATTN_GROUP * g - jj for jj in range(ATTN_GROUP)]
            return js, [jnp.maximum(j, 0) for j in js]

        def row_term(self, g, jj, hh, j):
            far = mask_row(mfar_sc, hh, j)
            if jj > 2:
                return far
            second = qlane >= blk
            if jj == 0:
                special = jnp.where(second, 0.0, NEG)
            elif jj == 1:
                special = jnp.where(second, mask_row(msel_sc, hh, j), 0.0)
            else:
                special = jnp.where(second, far, mask_row(msel_sc, hh, j))
            return jnp.where(g == 0, special, far)

    class PhaseA:
        def __init__(self, tile, g, buf, first):
            self.tile, self.g, self.first = tile, g, first
            self.js, self.jc = tile.block_ids(g)
            self.s_buf, self.gmax_buf = s_bufs[buf], gmax_bufs[buf]
            self.gmax = [None, None]

        def block(self, hh, jj):
            s = self.tile.scores(hh, self.jc[jj])
            if self.first and jj < 3:
                tiles = [(None, 0), (0, 1), (1, None)][jj]
                parts = [s[:, h * blk:(h + 1) * blk] if t is None else s[:, h * blk:(h + 1) * blk] + bias_ref[hh, t]
                         for h, t in enumerate(tiles)]
                s = jnp.concatenate(parts, axis=1)
            self.s_buf[hh, jj] = s
            cm = jnp.max(s, axis=0, keepdims=True) + self.tile.row_term(self.g, jj, hh, self.js[jj])
            self.gmax[hh] = cm if self.gmax[hh] is None else jnp.maximum(self.gmax[hh], cm)

        def finish(self):
            for hh in range(2):
                self.gmax_buf[hh] = self.gmax[hh]

    class PhaseB:
        def __init__(self, tile, g, buf, stash_rows=False):
            self.tile, self.g, self.stash_rows = tile, g, stash_rows
            self.js, self.jc = tile.block_ids(g)
            self.s_buf = s_bufs[buf]
            self.m_old = [m_sc[hh] for hh in range(2)]
            self.m_new = [jnp.maximum(self.m_old[hh], gmax_bufs[buf][hh]) for hh in range(2)]
            self.acc = [None, None]
            if stash_rows:
                for hh in range(2):
                    for jj in range(ATTN_GROUP):
                        rows_sc[hh, jj] = tile.row_term(g, jj, hh, self.js[jj])

        def block(self, hh, jj):
            row = rows_sc[hh, jj] if self.stash_rows else self.tile.row_term(self.g, jj, hh, self.js[jj])
            p = jnp.exp2(self.s_buf[hh, jj] - (self.m_new[hh] - row))
            pa = pv(hh, self.jc[jj], p)
            self.acc[hh] = pa if self.acc[hh] is None else self.acc[hh] + pa

        def finish(self):
            for hh in range(2):
                alpha = jnp.exp2(self.m_old[hh] - self.m_new[hh])
                m_sc[hh] = self.m_new[hh]
                acc_sc[hh] = alpha * acc_sc[hh] + self.acc[hh]

    def run(*phases):
        for hh in range(2):
            for jj in range(ATTN_GROUP):
                for ph in phases:
                    ph.block(hh, jj)
        for ph in phases:
            ph.finish()

    cur = Tile(qt, q_ref[0])

    @pl.when(qt == 0)
    def _():
        cur.select()
        run(PhaseA(cur, 0, 0, True))

    for hh in range(2):
        m_sc[hh] = gmax0_sc[hh]
        acc_sc[hh] = jnp.zeros((PV_ROWS, qw), F32)

    def pipelined(g, carry):
        for p in range(2):
            @pl.when(g % 2 == p)
            def _():
                run(PhaseA(cur, g + 1, 1 - p, False), PhaseB(cur, g, p))

        return carry

    lax.fori_loop(0, cur.n_groups - 1, pipelined, 0)

    def write_out():
        out = jnp.concatenate([acc_sc[hh, 0:HEAD_DIM, :] / acc_sc[hh, HEAD_DIM:HEAD_DIM + 1, :] for hh in range(2)],
                              axis=0)
        o_ref[0] = out.T.astype(BF16)

    last = cur.n_groups - 1
    for p in range(2):
        @pl.when((last % 2 == p) & (qt < n_tiles - 1))
        def _():
            nxt = Tile(qt + 1, qn_ref[0])
            tail = PhaseB(cur, last, p, stash_rows=True)
            nxt.select()
            run(tail, PhaseA(nxt, 0, 0, True))
            write_out()

        @pl.when((last % 2 == p) & (qt == n_tiles - 1))
        def _():
            run(PhaseB(cur, last, p))
            write_out()


def _moba_attention(q, k, v, bias_tiles, far_bias):
    bsz, seq_len, _ = q.shape
    blk = MOBA_BLOCK
    nb = seq_len // blk
    pair = 2 * HEAD_DIM
    qw = 2 * blk
    assert nb % 2 == 0
    return pl.pallas_call(
        functools.partial(_attn_kernel, nb=nb),
        grid=(bsz, N_HEADS // 2, nb // 2),
        in_specs=[
            pl.BlockSpec(memory_space=pltpu.SMEM),
            pl.BlockSpec((1, qw, pair), lambda b, h, i: (b, i, h)),
            pl.BlockSpec((1, qw, pair), lambda b, h, i: (b, jnp.minimum(i + 1, nb // 2 - 1), h)),
            pl.BlockSpec((1, seq_len, pair), lambda b, h, i: (b, 0, h)),
            pl.BlockSpec((1, seq_len, pair), lambda b, h, i: (b, 0, h)),
            pl.BlockSpec((2, 2, blk, blk), lambda b, h, i: (h, 0, 0, 0)),
        ],
        out_specs=pl.BlockSpec((1, qw, pair), lambda b, h, i: (b, i, h)),
        out_shape=jax.ShapeDtypeStruct((bsz, seq_len, D_ATTN), BF16),
        scratch_shapes=[
            pltpu.VMEM((2, nb, pair), F32),
            pltpu.VMEM((2, nb, PV_ROWS, blk), BF16),
            pltpu.VMEM((2, nb, qw), F32),
            pltpu.VMEM((2, nb, qw), F32),
            pltpu.VMEM((2, ATTN_GROUP, 1, qw), F32),
            pltpu.VMEM((2, ATTN_GROUP, blk, qw), F32),
            pltpu.VMEM((2, ATTN_GROUP, blk, qw), F32),
            pltpu.VMEM((2, 1, qw), F32),
            pltpu.VMEM((2, 1, qw), F32),
            pltpu.VMEM((2, 1, qw), F32),
            pltpu.VMEM((2, PV_ROWS, qw), F32),
        ],
        compiler_params=_params(3),
        name="moba_attention",
    )(far_bias, q, q, k, v, bias_tiles)


def _ffn_body(x, g_ref, sc_ref, sh_ref, gate_ref, wg_ref, wu_ref, cw_ref, wd_ref, fin_ref, o_ref,
              gext_sc, carry_sc, acc_sc, h_sc, *, tm, per_seq, final_norm):
    @pl.when(pl.program_id(0) % per_seq == 0)
    def _():
        carry_sc[...] = jnp.zeros_like(carry_sc)

    h_sc[...] = _norm_mod(x, g_ref[...], sc_ref[0], sh_ref[0]).astype(BF16)

    def gate_up(j):
        cols = slice(j * FFN_CHUNK, (j + 1) * FFN_CHUNK)
        h = h_sc[...]
        return (jnp.dot(h, wg_ref[:, cols], preferred_element_type=F32),
                jnp.dot(h, wu_ref[:, cols], preferred_element_type=F32))

    nxt = gate_up(0)
    for j in range(N_FFN_CHUNKS):
        cols = slice(j * FFN_CHUNK, (j + 1) * FFN_CHUNK)
        gpre, up = nxt
        if j + 1 < N_FFN_CHUNKS:
            nxt = gate_up(j + 1)
        lane_tiles = FFN_CHUNK // LANES
        for l in range(lane_tiles):
            gext_sc[j, l, 0:FFN_HALO, :] = carry_sc[j, :, l * LANES:(l + 1) * LANES]
            gext_sc[j, l, FFN_HALO:, :] = gpre[:, l * LANES:(l + 1) * LANES]
        carry_sc[j] = gpre[tm - FFN_HALO:, :]

        def back(n):
            return jnp.concatenate([gext_sc[j, l, FFN_HALO - n:FFN_HALO - n + tm, :] for l in range(lane_tiles)],
                                   axis=-1)

        cw = cw_ref[:, cols]
        conv = cw[0:1] * back(2) + cw[1:2] * back(1) + cw[2:3] * gpre + cw[3:4]
        act = (_silu(conv) * up).astype(BF16)
        part = jnp.dot(act, wd_ref[cols, :], preferred_element_type=F32)
        if j == 0:
            acc_sc[...] = part
        else:
            acc_sc[...] += part
    out = x + gate_ref[0] * acc_sc[...]
    if final_norm:
        ms = jnp.mean(out * out, axis=-1, keepdims=True)
        out = out * lax.rsqrt(ms + EPS) * fin_ref[...]
    o_ref[...] = out


def _ffn0_kernel(x_ref, ys_ref, ya_ref, wo_ref, g1_ref, g_ref, sc_ref, sh_ref, gate_ref,
                 wg_ref, wu_ref, cw_ref, wd_ref, fin_ref, o_ref, gext_sc, carry_sc, acc_sc, h_sc, **kw):
    y = (jnp.dot(ys_ref[...], wo_ref[0:D_SSM, :], preferred_element_type=F32)
         + jnp.dot(ya_ref[...], wo_ref[D_SSM:, :], preferred_element_type=F32))
    x = x_ref[...] + g1_ref[0] * y
    _ffn_body(x, g_ref, sc_ref, sh_ref, gate_ref, wg_ref, wu_ref, cw_ref, wd_ref, fin_ref, o_ref,
              gext_sc, carry_sc, acc_sc, h_sc, **kw)


def _ffn1_kernel(x_ref, g_ref, sc_ref, sh_ref, gate_ref, wg_ref, wu_ref, cw_ref, wd_ref, fin_ref,
                 o_ref, gext_sc, carry_sc, acc_sc, h_sc, **kw):
    _ffn_body(x_ref[...], g_ref, sc_ref, sh_ref, gate_ref, wg_ref, wu_ref, cw_ref, wd_ref, fin_ref, o_ref,
              gext_sc, carry_sc, acc_sc, h_sc, **kw)


def _ffn_weights(w_up, w_gate, dw_w, dw_b, w_down, layer):
    cw = jnp.concatenate([dw_w[layer], dw_b[layer][None, :]], axis=0)
    return w_gate.astype(BF16), w_up.astype(BF16), cw, w_down.astype(BF16), layer


def _conv_ffn(x2, mixer, norm_g, scale, shift, gate, weights, final_g, seq_len, final_norm, tm=256):
    tok, d = x2.shape
    per_seq = seq_len // tm
    wg, wu, cw, wd, layer = weights
    row = lambda i: (i, 0)
    bat = lambda i: (i // per_seq, 0, 0)
    vec = pl.BlockSpec((1, 1, d), bat)
    layer_spec = lambda w: pl.BlockSpec((None,) + w.shape[1:], lambda i: (layer, 0, 0), pipeline_mode=pl.Buffered(1))
    common_specs = [_const_spec((1, d)), vec, vec, vec,
                    layer_spec(wg), layer_spec(wu), _const_spec(cw.shape), layer_spec(wd),
                    _const_spec((1, d))]
    common_args = [norm_g, scale, shift, gate, wg, wu, cw, wd, final_g]
    kw = dict(tm=tm, per_seq=per_seq, final_norm=final_norm)
    if mixer is None:
        body = functools.partial(_ffn1_kernel, **kw)
        specs = [pl.BlockSpec((tm, d), row)] + common_specs
        args = [x2] + common_args
    else:
        ys, ya, wo, g1 = mixer
        body = functools.partial(_ffn0_kernel, **kw)
        specs = [pl.BlockSpec((tm, d), row), pl.BlockSpec((tm, D_SSM), row), pl.BlockSpec((tm, D_ATTN), row),
                 _const_spec(wo.shape), vec] + common_specs
        args = [x2, ys, ya, wo, g1] + common_args
    return pl.pallas_call(
        body,
        grid=(tok // tm,),
        in_specs=specs,
        out_specs=pl.BlockSpec((tm, d), row),
        out_shape=jax.ShapeDtypeStruct((tok, d), F32),
        scratch_shapes=[
            pltpu.VMEM((N_FFN_CHUNKS, FFN_CHUNK // LANES, tm + FFN_HALO, LANES), F32),
            pltpu.VMEM((N_FFN_CHUNKS, FFN_HALO, FFN_CHUNK), F32),
            pltpu.VMEM((tm, d), F32),
            pltpu.VMEM((tm, d), BF16),
        ],
        compiler_params=_params(1),
        name="conv_ffn_final" if final_norm else "conv_ffn",
    )(*args)


def _conformer_kernel(x_ref, g_ref, sc_ref, sh_ref, gate_ref, win_ref, bin_ref, dw_ref, dwb_ref,
                      lng_ref, lnb_ref, wout_ref, bout_ref, o_ref, aext_sc, conv_sc, h_sc, *, tm, per_seq):
    d = x_ref.shape[-1]
    lanes = d // LANES
    x = x_ref[...]
    h_sc[...] = _norm_mod(x, g_ref[...], sc_ref[0], sh_ref[0]).astype(BF16)

    @pl.when(pl.program_id(0) % per_seq == 0)
    def _():
        aext_sc[:, 0:CONV_HALO, :] = jnp.zeros((lanes, CONV_HALO, LANES), F32)

    rc = 32
    off = CONV_HALO - (CONV_WIDTH - 1)
    for c0 in range(0, d, CONV_CHUNK):
        cc = slice(c0, c0 + CONV_CHUNK)
        gc = slice(d + c0, d + c0 + CONV_CHUNK)
        h = h_sc[...]
        half_a = jnp.dot(h, win_ref[:, cc], preferred_element_type=F32) + bin_ref[:, cc]
        half_b = jnp.dot(h, win_ref[:, gc], preferred_element_type=F32) + bin_ref[:, gc]
        a = half_a * jnp.tanh(half_b) + half_a
        for l in range(c0 // LANES, (c0 + CONV_CHUNK) // LANES):
            cols = slice(l * LANES, (l + 1) * LANES)
            aext_sc[l, CONV_HALO:, :] = a[:, l * LANES - c0:(l + 1) * LANES - c0]
            for base in range(0, tm, rc):
                acc = [dwb_ref[:, cols]] * (rc // 8)
                for k in range(CONV_WIDTH):
                    w8 = dw_ref[k, :, cols]
                    for j in range(rc // 8):
                        lo = base + 8 * j + off + k
                        acc[j] = acc[j] + w8 * aext_sc[l, lo:lo + 8, :]
                for j in range(rc // 8):
                    conv_sc[base + 8 * j:base + 8 * j + 8, cols] = acc[j]
            aext_sc[l, 0:CONV_HALO, :] = aext_sc[l, tm:tm + CONV_HALO, :]

    c = conv_sc[...]
    mu = jnp.mean(c, axis=-1, keepdims=True)
    xc = c - mu
    y = xc * lax.rsqrt(jnp.mean(xc * xc, axis=-1, keepdims=True) + EPS)
    half_v = y * lng_ref[...] + lnb_ref[...]
    y = (half_v * jnp.tanh(half_v) + half_v).astype(BF16)
    out = jnp.dot(y, wout_ref[...], preferred_element_type=F32) + bout_ref[...]
    o_ref[...] = x + gate_ref[0] * out


def _conformer(x2, norm_g, scale, shift, gate, w_in, b_in, dw_w, dw_b, ln_g, ln_b, w_out, b_out, seq_len, tm=512):
    tok, d = x2.shape
    per_seq = seq_len // tm
    row = lambda i: (i, 0)
    vec = pl.BlockSpec((1, 1, d), lambda i: (i // per_seq, 0, 0))
    return pl.pallas_call(
        functools.partial(_conformer_kernel, tm=tm, per_seq=per_seq),
        grid=(tok // tm,),
        in_specs=[pl.BlockSpec((tm, d), row), _const_spec((1, d)), vec, vec, vec,
                  _const_spec((d, 2 * d)), _const_spec((1, 2 * d)), _const_spec((CONV_WIDTH, 8, d)), _const_spec((8, d)),
                  _const_spec((1, d)), _const_spec((1, d)), _const_spec((d, d)), _const_spec((1, d))],
        out_specs=pl.BlockSpec((tm, d), row),
        out_shape=jax.ShapeDtypeStruct((tok, d), F32),
        scratch_shapes=[pltpu.VMEM((d // LANES, tm + CONV_HALO, LANES), F32), pltpu.VMEM((tm, d), F32),
                        pltpu.VMEM((tm, d), BF16)],
        compiler_params=_params(1),
        name="conformer_conv",
    )(x2, norm_g, scale, shift, gate, (0.5 * w_in).astype(BF16), 0.5 * b_in.reshape(1, -1),
      jnp.broadcast_to(dw_w[:, None, :], (CONV_WIDTH, 8, d)), jnp.broadcast_to(dw_b[None, :], (8, d)),
      0.5 * ln_g.reshape(1, -1), 0.5 * ln_b.reshape(1, -1), w_out.astype(BF16), b_out.reshape(1, -1))


def kernel(x, c, mod_w, mod_b, norm_g, final_g, ab_w_in, ssm_a_re, ssm_a_im, ssm_log_dt, ssm_b_re, ssm_b_im, ssm_c_re, ssm_c_im, ssm_d, ssm_glu_w, ssm_glu_b, ab_w_out, rel_bias, cm_w_in, cm_b_in, cm_dw_w, cm_dw_b, cm_ln_g, cm_ln_b, cm_w_out, cm_b_out, ffn_w_up, ffn_w_gate, ffn_dw_w, ffn_dw_b, ffn_w_down):
    bsz, seq_len, d = x.shape
    tok = bsz * seq_len
    x2 = x.reshape(tok, d)
    mod = _modulation(c, mod_w, mod_b)
    vecs = [[mod[l, :, i * d:(i + 1) * d].reshape(bsz, 1, d) for i in range(6)] for l in range(2)]
    fin = final_g.reshape(1, d)

    sh1, sc1, g1, sh2, sc2, g2 = vecs[0]
    u, q, k, v = _in_projection(x2, norm_g[0, 0].reshape(1, d), sc1, sh1, ab_w_in[0].astype(BF16), seq_len)
    ops = _s5_prepare(ssm_a_re[0], ssm_a_im[0], ssm_log_dt[0], ssm_b_re[0], ssm_b_im[0], ssm_c_re[0], ssm_c_im[0])
    y_ssm = _s5_mixer(u.reshape(S5_T, bsz, seq_len // S5_T, D_SSM), ops, ssm_d[0], ssm_glu_w[0], ssm_glu_b[0])
    att = lambda a: a.reshape(bsz, seq_len, D_ATTN)
    y_att = _moba_attention(att(q), att(k), att(v), _bias_tiles(rel_bias), rel_bias[REL_BUCKETS - 1])
    w0 = _ffn_weights(ffn_w_up, ffn_w_gate, ffn_dw_w, ffn_dw_b, ffn_w_down, 0)
    x2 = _conv_ffn(x2, (y_ssm.reshape(tok, D_SSM), y_att.reshape(tok, D_ATTN), ab_w_out[0].astype(BF16), g1),
                   norm_g[0, 1].reshape(1, d), sc2, sh2, g2, w0, fin, seq_len, final_norm=False)

    sh1, sc1, g1, sh2, sc2, g2 = vecs[1]
    x2 = _conformer(x2, norm_g[1, 0].reshape(1, d), sc1, sh1, g1, cm_w_in[0], cm_b_in[0], cm_dw_w[0], cm_dw_b[0],
                    cm_ln_g[0], cm_ln_b[0], cm_w_out[0], cm_b_out[0], seq_len)
    w1 = _ffn_weights(ffn_w_up, ffn_w_gate, ffn_dw_w, ffn_dw_b, ffn_w_down, 1)
    x2 = _conv_ffn(x2, None, norm_g[1, 1].reshape(1, d), sc2, sh2, g2, w1, fin, seq_len, final_norm=True)
    return x2.reshape(bsz, seq_len, d)
```

```python
import functools
import math

import numpy as np
import jax
import jax.numpy as jnp
from jax import lax
from jax.experimental import pallas as pl
from jax.experimental.pallas import tpu as pltpu

F32 = jnp.float32
BF16 = jnp.bfloat16

D_MODEL = 1024
D_SSM = 512
SSM_GROUP = 16
SSM_GROUPS = 32
SSM_STATE = 64
D_ATTN = 512
HEAD_DIM = 64
N_HEADS = 8
MOBA_BLOCK = 256
MOBA_TOPK = 3
REL_BUCKETS = 32
REL_MAX_DIST = 128
CONV_WIDTH = 31
FFN_HIDDEN = 2816
FFN_CONV_WIDTH = 3
EPS = 1e-6

NEG = -1e30
LOG2E = math.log2(math.e)

V7X_VMEM_BYTES = 64 * 1024 * 1024
VMEM_LIMIT = V7X_VMEM_BYTES - 8 * 1024 * 1024

S5_T = 4
S5_CB = 32
S5_ROW_STRIDE = S5_CB + 8
LANES = 128
MXU_TILE = 256
S5_QUAD = 4
FFN_CHUNK = 256
N_FFN_CHUNKS = FFN_HIDDEN // FFN_CHUNK
CONV_HALO = 32
CONV_CHUNK = 256
FFN_HALO = 8
ATTN_GROUP = 4
PV_ROWS = HEAD_DIM + 16


def _sigmoid(x):
    return 0.5 * jnp.tanh(0.5 * x) + 0.5


def _silu(x):
    return x * _sigmoid(x)


def _gelu_tanh(x):
    c = math.sqrt(2.0 / math.pi)
    return 0.5 * x * (1.0 + jnp.tanh(c * (x + 0.044715 * (x * x * x))))


def _norm_mod(x, g, scale, shift):
    ms = jnp.mean(x * x, axis=-1, keepdims=True)
    y = x * lax.rsqrt(ms + EPS) * g
    return y * (1.0 + scale) + shift


def _params(n_axes, vmem=VMEM_LIMIT, flags=None, fuse_inputs=None):
    return pltpu.CompilerParams(dimension_semantics=("arbitrary",) * n_axes, vmem_limit_bytes=vmem, flags=flags,
                                allow_input_fusion=fuse_inputs)


def _const_spec(shape):
    nd = len(shape)
    return pl.BlockSpec(shape, lambda *_: (0,) * nd, pipeline_mode=pl.Buffered(1))


def _mod_kernel(c_ref, w_ref, b_ref, o_ref):
    c = c_ref[...]
    cs = _silu(c).astype(BF16)
    o_ref[0] = jnp.dot(cs, w_ref[0].astype(BF16), preferred_element_type=F32) + b_ref[0]


def _modulation(c, mod_w, mod_b):
    depth, d, n = mod_w.shape
    bsz = c.shape[0]
    nt = 1536
    return pl.pallas_call(
        _mod_kernel,
        grid=(depth, n // nt),
        in_specs=[
            pl.BlockSpec((bsz, d), lambda l, j: (0, 0)),
            pl.BlockSpec((1, d, nt), lambda l, j: (l, 0, j)),
            pl.BlockSpec((1, 1, nt), lambda l, j: (l, 0, j)),
        ],
        out_specs=pl.BlockSpec((1, bsz, nt), lambda l, j: (l, 0, j)),
        out_shape=jax.ShapeDtypeStruct((depth, bsz, n), F32),
        compiler_params=_params(2),
        name="modulation",
    )(c, mod_w, mod_b.reshape(depth, 1, n))


def _inproj_kernel(x_ref, g_ref, sc_ref, sh_ref, w_ref, u_ref, q_ref, k_ref, v_ref, u_sc):
    h = _norm_mod(x_ref[...], g_ref[...], sc_ref[0], sh_ref[0]).astype(BF16)
    p = jnp.dot(h, w_ref[...], preferred_element_type=F32)
    tm = p.shape[0]
    lanes = D_SSM // LANES
    for l in range(lanes):
        u_sc[l] = p[:, l * LANES:(l + 1) * LANES]
    for s in range(S5_T):
        u_ref[s] = jnp.concatenate([u_sc[l, pl.ds(s, tm // S5_T, stride=S5_T), :] for l in range(lanes)], axis=-1)
    q_ref[...] = (p[:, D_SSM:D_SSM + D_ATTN] * (HEAD_DIM ** -0.5 * LOG2E)).astype(BF16)
    k_ref[...] = p[:, D_SSM + D_ATTN:D_SSM + 2 * D_ATTN].astype(BF16)
    v_ref[...] = p[:, D_SSM + 2 * D_ATTN:].astype(BF16)


def _in_projection(x2, g, scale, shift, w, seq_len, tm=1024):
    tok, d = x2.shape
    per_seq = seq_len // tm
    n = w.shape[1]
    row = lambda i: (i, 0)
    bat = lambda i: (i // per_seq, 0, 0)
    return pl.pallas_call(
        _inproj_kernel,
        grid=(tok // tm,),
        in_specs=[
            pl.BlockSpec((tm, d), row),
            _const_spec((1, d)),
            pl.BlockSpec((1, 1, d), bat),
            pl.BlockSpec((1, 1, d), bat),
            _const_spec((d, n)),
        ],
        out_specs=[
            pl.BlockSpec((S5_T, tm // S5_T, D_SSM), lambda i: (0, i, 0)),
            pl.BlockSpec((tm, D_ATTN), row),
            pl.BlockSpec((tm, D_ATTN), row),
            pl.BlockSpec((tm, D_ATTN), row),
        ],
        out_shape=[
            jax.ShapeDtypeStruct((S5_T, tok // S5_T, D_SSM), F32),
            jax.ShapeDtypeStruct((tok, D_ATTN), BF16),
            jax.ShapeDtypeStruct((tok, D_ATTN), BF16),
            jax.ShapeDtypeStruct((tok, D_ATTN), BF16),
        ],
        scratch_shapes=[pltpu.VMEM((D_SSM // LANES, tm, LANES), F32)],
        compiler_params=_params(1, fuse_inputs=[False, False, False, False, True]),
        name="in_projection",
    )(x2, g, scale, shift, w)


def _s5_prep_kernel(lre_r, lim_r, ldt_r, lre_c, lim_c, ldt_c, btr, bti, cre, cim, ctr, cti,
                    kt_ref, sre_ref, sim_ref, ore_ref, oim_ref, at_ref):
    def discretise(lre, lim, ldt):
        dt = jnp.exp(ldt)
        mag = jnp.exp(lre * dt)
        return mag * jnp.cos(lim * dt), mag * jnp.sin(lim * dt)

    lre, lim = lre_r[...], lim_r[...]
    ar, ai = discretise(lre, lim, ldt_r[...])
    den = lre * lre + lim * lim
    nr = ar - 1.0
    coef_re = (nr * lre + ai * lim) / den
    coef_im = (ai * lre - nr * lim) / den
    br, bi = btr[...], bti[...]
    zr = coef_re * br - coef_im * bi
    zi = coef_re * bi + coef_im * br
    c_re, c_im = cre[...], cim[...]
    for k in range(S5_T):
        sre_ref[S5_T - 1 - k] = zr
        sim_ref[S5_T - 1 - k] = zi
        for h in range(SSM_GROUP):
            kt_ref[k, h] = jnp.sum(c_re[:, h:h + 1, :] * zr - c_im[:, h:h + 1, :] * zi, axis=-1)
        zr, zi = ar * zr - ai * zi, ar * zi + ai * zr

    acr, aci = discretise(lre_c[...], lim_c[...], ldt_c[...])
    pr, pi = acr, aci
    ct_re, ct_im = ctr[...], cti[...]
    for t in range(S5_T):
        ore_ref[t] = ct_re * pr - ct_im * pi
        oim_ref[t] = -ct_re * pi - ct_im * pr
        pr, pi = acr * pr - aci * pi, acr * pi + aci * pr

    qr, qi = ar, ai
    for _ in range(S5_T - 1):
        qr, qi = ar * qr - ai * qi, ar * qi + ai * qr
    at_ref[0] = qr
    at_ref[1] = qi


def _s5_prepare(a_re, a_im, log_dt, b_re, b_im, c_re, c_im):
    g, p, h, t = SSM_GROUPS, SSM_STATE, SSM_GROUP, S5_T
    ins = [
        a_re.reshape(g, 1, p), a_im.reshape(g, 1, p), log_dt.reshape(g, 1, 1),
        a_re.reshape(g, p, 1), a_im.reshape(g, p, 1), log_dt.reshape(g, 1, 1),
        b_re.transpose(0, 2, 1), b_im.transpose(0, 2, 1), c_re, c_im,
        c_re.transpose(0, 2, 1), c_im.transpose(0, 2, 1),
    ]
    full = lambda s: pl.BlockSpec(s, lambda: (0,) * len(s))
    out_shapes = [(t, h, g, h), (t, g, h, p), (t, g, h, p), (t, g, p, h), (t, g, p, h), (2, g, 1, p)]
    kt, sre, sim, ore, oim, at = pl.pallas_call(
        _s5_prep_kernel,
        in_specs=[full(a.shape) for a in ins],
        out_specs=[full(s) for s in out_shapes],
        out_shape=[jax.ShapeDtypeStruct(s, F32) for s in out_shapes],
        name="s5_prepare",
    )(*ins)

    q4 = S5_QUAD
    nq = g // q4
    eye = jnp.eye(q4, dtype=F32)
    ktg = kt.transpose(0, 2, 3, 1).reshape(t, nq, q4, h, h)
    steps = jnp.arange(t)
    lag_is = ((steps[None, :] - steps[:, None])[None] == steps[:, None, None]).astype(F32)
    toe = jnp.einsum("kst,kqaxy,ac->qsaxtcy", lag_is, ktg, eye).reshape(nq, t * q4 * h, t * q4 * h)

    def s_quads(s):
        return jnp.einsum("sqaxp,ac->qsaxcp", s.reshape(t, nq, q4, h, p), eye).reshape(nq, t * q4 * h, q4 * p)

    def o_quads(o):
        return jnp.einsum("tqapy,ac->qaptcy", o.reshape(t, nq, q4, p, h), eye).reshape(nq, q4 * p, t * q4 * h)

    smat = jnp.concatenate([s_quads(sre), s_quads(sim)], axis=-1)
    omat = jnp.concatenate([o_quads(ore), o_quads(oim)], axis=-2)
    atr = at[0].reshape(1, g * p)
    ati = at[1].reshape(1, g * p)
    return toe.astype(BF16), smat.astype(BF16), omat.astype(BF16), atr, ati


def _s5_kernel(u_ref, toe_ref, smat_ref, omat_ref, atr_ref, ati_ref, d_ref, gw_ref, gb_ref,
               y_ref, yq_sc, s_sc, xp_sc, cr_sc, ci_sc, y_sc, *, bsz):
    rows = bsz * S5_CB
    nq = SSM_GROUPS // S5_QUAD
    qch = S5_QUAD * SSM_GROUP
    qst = S5_QUAD * SSM_STATE

    @pl.when(pl.program_id(0) == 0)
    def _():
        cr_sc[...] = jnp.zeros_like(cr_sc)
        ci_sc[...] = jnp.zeros_like(ci_sc)

    u = [u_ref[s].reshape(rows, D_SSM) for s in range(S5_T)]
    xq = [jnp.concatenate([u[s][:, q * qch:(q + 1) * qch] for s in range(S5_T)], axis=-1).astype(BF16)
          for q in range(nq)]

    lt = 2 * qst // LANES
    for q in range(nq):
        st = jnp.dot(xq[q], smat_ref[q], preferred_element_type=F32)
        for l in range(lt):
            for b in range(bsz):
                s_sc[q * lt + l, b * S5_ROW_STRIDE:b * S5_ROW_STRIDE + S5_CB, :] = (
                    st[b * S5_CB:(b + 1) * S5_CB, l * LANES:(l + 1) * LANES])

    for q in range(nq):
        yq_sc[q] = jnp.dot(xq[q], toe_ref[q], preferred_element_type=F32)

    half_lt = lt // 2
    for q in range(nq):
        for l in range(half_lt):
            cols = slice(q * qst + l * LANES, q * qst + (l + 1) * LANES)
            a_r = atr_ref[:, cols]
            a_i = ati_ref[:, cols]
            xr = cr_sc[:, cols]
            xi = ci_sc[:, cols]
            t_re = q * lt + l
            t_im = q * lt + half_lt + l
            for c in range(S5_CB):
                idx = pl.ds(c, bsz, stride=S5_ROW_STRIDE)
                xp_sc[t_re, idx, :] = xr
                xp_sc[t_im, idx, :] = xi
                sr = s_sc[t_re, idx, :]
                si = s_sc[t_im, idx, :]
                xr, xi = a_r * xr - a_i * xi + sr, a_r * xi + a_i * xr + si
            cr_sc[:, cols] = xr
            ci_sc[:, cols] = xi

    def xp_tile(i):
        return jnp.concatenate([xp_sc[i, b * S5_ROW_STRIDE:b * S5_ROW_STRIDE + S5_CB, :] for b in range(bsz)], axis=0)

    for q in range(nq):
        xpq = jnp.concatenate([xp_tile(q * lt + l) for l in range(lt)], axis=-1).astype(BF16)
        yq_sc[q] += jnp.dot(xpq, omat_ref[q], preferred_element_type=F32)

    gw = gw_ref[...]
    for t in range(S5_T):
        yt = jnp.concatenate([yq_sc[q, :, t * qch:(t + 1) * qch] for q in range(nq)], axis=-1)
        y = _gelu_tanh(yt + d_ref[...] * u[t])
        z = jnp.dot(y.astype(BF16), gw, preferred_element_type=F32) + gb_ref[...]
        out = y * _sigmoid(z)
        for b in range(bsz):
            for l in range(D_SSM // LANES):
                y_sc[l, pl.ds(b * S5_T * S5_CB + t, S5_CB, stride=S5_T), :] = (
                    out[b * S5_CB:(b + 1) * S5_CB, l * LANES:(l + 1) * LANES])
    span = S5_T * S5_CB
    for b in range(bsz):
        y_ref[b] = jnp.concatenate([y_sc[l, b * span:(b + 1) * span, :] for l in range(D_SSM // LANES)],
                                   axis=-1).astype(BF16)


def _s5_mixer(u, ops, d_skip, glu_w, glu_b):
    _, bsz, nchunk, _ = u.shape
    seq_len = nchunk * S5_T
    toe, smat, omat, atr, ati = ops
    rows = bsz * S5_CB
    state_w = 2 * SSM_GROUPS * SSM_STATE
    assert S5_T * S5_QUAD * SSM_GROUP == MXU_TILE and S5_QUAD * SSM_STATE == MXU_TILE
    return pl.pallas_call(
        functools.partial(_s5_kernel, bsz=bsz),
        grid=(nchunk // S5_CB,),
        in_specs=[
            pl.BlockSpec((S5_T, bsz, S5_CB, D_SSM), lambda i: (0, 0, i, 0)),
            _const_spec(toe.shape), _const_spec(smat.shape), _const_spec(omat.shape),
            _const_spec(atr.shape), _const_spec(ati.shape),
            _const_spec((1, D_SSM)), _const_spec((D_SSM, D_SSM)), _const_spec((1, D_SSM)),
        ],
        out_specs=pl.BlockSpec((bsz, S5_T * S5_CB, D_SSM), lambda i: (0, i, 0)),
        out_shape=jax.ShapeDtypeStruct((bsz, seq_len, D_SSM), BF16),
        scratch_shapes=[
            pltpu.VMEM((SSM_GROUPS // S5_QUAD, rows, MXU_TILE), F32),
            pltpu.VMEM((state_w // LANES, bsz * S5_ROW_STRIDE, LANES), F32),
            pltpu.VMEM((state_w // LANES, bsz * S5_ROW_STRIDE, LANES), F32),
            pltpu.VMEM((bsz, state_w // 2), F32),
            pltpu.VMEM((bsz, state_w // 2), F32),
            pltpu.VMEM((D_SSM // LANES, rows * S5_T, LANES), F32),
        ],
        compiler_params=_params(1),
        name="s5_mixer",
    )(u, toe, smat, omat, atr, ati, d_skip.reshape(1, D_SSM), glu_w.astype(BF16), glu_b.reshape(1, D_SSM))


def _rel_bucket_np(dist):
    n = np.maximum(dist, 0)
    max_exact = REL_BUCKETS // 2
    nf = np.maximum(n, 1).astype(np.float64)
    large = max_exact + (np.log(nf / max_exact) / math.log(REL_MAX_DIST / max_exact)
                         * (REL_BUCKETS - max_exact)).astype(np.int64)
    large = np.minimum(large, REL_BUCKETS - 1)
    return np.where(n < max_exact, n, large).astype(np.int32)


def _bias_bucket_tiles():
    ko = np.arange(MOBA_BLOCK)[:, None]
    qo = np.arange(MOBA_BLOCK)[None, :]
    own = np.where(qo >= ko, _rel_bucket_np(qo - ko), -1)
    prev = _rel_bucket_np(qo - ko + MOBA_BLOCK)
    return np.stack([own, prev]).astype(np.int32)


assert int(_rel_bucket_np(np.arange(MOBA_BLOCK + 1, 8 * MOBA_BLOCK)).min()) == REL_BUCKETS - 1


def _bias_kernel(tab_ref, idx_ref, o_ref):
    h = pl.program_id(0)
    for t in range(2):
        idx = idx_ref[t]
        acc = jnp.full(idx.shape, NEG, F32)
        for b in range(REL_BUCKETS):
            acc = jnp.where(idx == b, tab_ref[h, b] * LOG2E, acc)
        o_ref[0, t] = acc


def _bias_tiles(rel_bias):
    idx = jnp.asarray(_bias_bucket_tiles())
    blk = MOBA_BLOCK
    return pl.pallas_call(
        _bias_kernel,
        grid=(N_HEADS,),
        in_specs=[
            pl.BlockSpec(memory_space=pltpu.SMEM),
            pl.BlockSpec((2, blk, blk), lambda h: (0, 0, 0)),
        ],
        out_specs=pl.BlockSpec((1, 2, blk, blk), lambda h: (h, 0, 0, 0)),
        out_shape=jax.ShapeDtypeStruct((N_HEADS, 2, blk, blk), F32),
        compiler_params=_params(1),
        name="moba_bias_tiles",
    )(rel_bias.T, idx)


def _attn_kernel(far_ref, q_ref, qn_ref, k_ref, v_ref, bias_ref, o_ref,
                 kmean_sc, vt_sc, mfar_sc, msel_sc, rows_sc, s0_sc, s1_sc, gmax0_sc, gmax1_sc, m_sc, acc_sc, *, nb):
    s_bufs = (s0_sc, s1_sc)
    gmax_bufs = (gmax0_sc, gmax1_sc)
    hp = pl.program_id(1)
    qt = pl.program_id(2)
    n_tiles = nb // 2
    blk = MOBA_BLOCK
    qw = 2 * blk
    qlane = lax.broadcasted_iota(jnp.int32, (1, qw), 1)
    lane = lax.broadcasted_iota(jnp.int32, (1, 2 * HEAD_DIM), 1)
    head_mask = [lane < HEAD_DIM, lane >= HEAD_DIM]
    nt = (((1,), (1,)), ((), ()))

    @pl.when(qt == 0)
    def _():
        for j in range(nb):
            kb = k_ref[0, j * blk:(j + 1) * blk, :].astype(F32)
            km = jnp.mean(kb, axis=0, keepdims=True)
            for hh in range(2):
                kmean_sc[hh, j:j + 1, :] = jnp.where(head_mask[hh], km, 0.0)
            vt = v_ref[0, j * blk:(j + 1) * blk, :].astype(F32).T.astype(BF16)
            ones_row = jnp.where(lax.broadcasted_iota(jnp.int32, (PV_ROWS - HEAD_DIM, blk), 0) == 0, 1.0, 0.0)
            for hh in range(2):
                vt_sc[hh, j, 0:HEAD_DIM, :] = vt[hh * HEAD_DIM:(hh + 1) * HEAD_DIM, :]
                vt_sc[hh, j, HEAD_DIM:, :] = ones_row.astype(BF16)

    jidx = lax.broadcasted_iota(jnp.int32, (nb, qw), 0)

    def pv(hh, j, p):
        return jnp.dot(vt_sc[hh, j], p.astype(BF16), preferred_element_type=F32)

    def mask_row(ref, hh, j):
        return jnp.where(j >= 0, ref[hh, pl.ds(jnp.maximum(j, 0), 1), :], NEG)

    class Tile:
        def __init__(self, t, q2):
            self.q2 = q2
            self.top = 2 * t + 1
            self.own = 2 * t + jnp.where(qlane >= blk, 1, 0)
            self.n_groups = self.top // ATTN_GROUP + 1
            self.qm = [jnp.where(head_mask[hh], q2, jnp.zeros_like(q2)) for hh in range(2)]

        def select(self):
            for hh in range(2):
                gate = lax.dot_general(kmean_sc[hh].astype(BF16), self.q2, nt, preferred_element_type=F32)
                rank = jnp.zeros((nb, qw), F32)
                for jp in range(nb):
                    row = gate[jp:jp + 1, :]
                    beats = (row > gate) | ((row == gate) & (jidx > jp))
                    rank = rank + jnp.where(beats & (self.own > jp), 1.0, 0.0)
                sel = (rank < float(MOBA_TOPK)) & (jidx < self.own)
                mfar_sc[hh] = jnp.where(sel, far_ref[2 * hp + hh] * LOG2E, NEG)
                msel_sc[hh] = jnp.where(sel, 0.0, NEG)

        def scores(self, hh, j):
            kb = k_ref[0, pl.ds(pl.multiple_of(j * blk, blk), blk), :]
            return lax.dot_general(kb, self.qm[hh], nt, preferred_element_type=F32)

        def block_ids(self, g):
            js = [self.top - ATTN_GROUP * g - jj for jj in range(ATTN_GROUP)]
            return js, [jnp.maximum(j, 0) for j in js]

        def row_term(self, g, jj, hh, j):
            far = mask_row(mfar_sc, hh, j)
            if jj > 2:
                return far
            second = qlane >= blk
            if jj == 0:
                special = jnp.where(second, 0.0, NEG)
            elif jj == 1:
                special = jnp.where(second, mask_row(msel_sc, hh, j), 0.0)
            else:
                special = jnp.where(second, far, mask_row(msel_sc, hh, j))
            return jnp.where(g == 0, special, far)

    class PhaseA:
        def __init__(self, tile, g, buf, first):
            self.tile, self.g, self.first = tile, g, first
            self.js, self.jc = tile.block_ids(g)
            self.s_buf, self.gmax_buf = s_bufs[buf], gmax_bufs[buf]
            self.gmax = [None, None]

        def block(self, hh, jj):
            s = self.tile.scores(hh, self.jc[jj])
            if self.first and jj < 3:
                tiles = [(None, 0), (0, 1), (1, None)][jj]
                parts = [s[:, h * blk:(h + 1) * blk] if t is None else s[:, h * blk:(h + 1) * blk] + bias_ref[hh, t]
                         for h, t in enumerate(tiles)]
                s = jnp.concatenate(parts, axis=1)
            self.s_buf[hh, jj] = s
            cm = jnp.max(s, axis=0, keepdims=True) + self.tile.row_term(self.g, jj, hh, self.js[jj])
            self.gmax[hh] = cm if self.gmax[hh] is None else jnp.maximum(self.gmax[hh], cm)

        def finish(self):
            for hh in range(2):
                self.gmax_buf[hh] = self.gmax[hh]

    class PhaseB:
        def __init__(self, tile, g, buf, stash_rows=False):
            self.tile, self.g, self.stash_rows = tile, g, stash_rows
            self.js, self.jc = tile.block_ids(g)
            self.s_buf = s_bufs[buf]
            self.m_old = [m_sc[hh] for hh in range(2)]
            self.m_new = [jnp.maximum(self.m_old[hh], gmax_bufs[buf][hh]) for hh in range(2)]
            self.acc = [None, None]
            if stash_rows:
                for hh in range(2):
                    for jj in range(ATTN_GROUP):
                        rows_sc[hh, jj] = tile.row_term(g, jj, hh, self.js[jj])

        def block(self, hh, jj):
            row = rows_sc[hh, jj] if self.stash_rows else self.tile.row_term(self.g, jj, hh, self.js[jj])
            p = jnp.exp2(self.s_buf[hh, jj] - (self.m_new[hh] - row))
            pa = pv(hh, self.jc[jj], p)
            self.acc[hh] = pa if self.acc[hh] is None else self.acc[hh] + pa

        def finish(self):
            for hh in range(2):
                alpha = jnp.exp2(self.m_old[hh] - self.m_new[hh])
                m_sc[hh] = self.m_new[hh]
                acc_sc[hh] = alpha * acc_sc[hh] + self.acc[hh]

    def run(*phases):
        for hh in range(2):
            for jj in range(ATTN_GROUP):
                for ph in phases:
                    ph.block(hh, jj)
        for ph in phases:
            ph.finish()

    cur = Tile(qt, q_ref[0])

    @pl.when(qt == 0)
    def _():
        cur.select()
        run(PhaseA(cur, 0, 0, True))

    for hh in range(2):
        m_sc[hh] = gmax0_sc[hh]
        acc_sc[hh] = jnp.zeros((PV_ROWS, qw), F32)

    def pipelined(g, carry):
        for p in range(2):
            @pl.when(g % 2 == p)
            def _():
                run(PhaseA(cur, g + 1, 1 - p, False), PhaseB(cur, g, p))

        return carry

    lax.fori_loop(0, cur.n_groups - 1, pipelined, 0)

    def write_out():
        out = jnp.concatenate([acc_sc[hh, 0:HEAD_DIM, :] / acc_sc[hh, HEAD_DIM:HEAD_DIM + 1, :] for hh in range(2)],
                              axis=0)
        o_ref[0] = out.T.astype(BF16)

    last = cur.n_groups - 1
    for p in range(2):
        @pl.when((last % 2 == p) & (qt < n_tiles - 1))
        def _():
            nxt = Tile(qt + 1, qn_ref[0])
            tail = PhaseB(cur, last, p, stash_rows=True)
            nxt.select()
            run(tail, PhaseA(nxt, 0, 0, True))
            write_out()

        @pl.when((last % 2 == p) & (qt == n_tiles - 1))
        def _():
            run(PhaseB(cur, last, p))
            write_out()


def _moba_attention(q, k, v, bias_tiles, far_bias):
    bsz, seq_len, _ = q.shape
    blk = MOBA_BLOCK
    nb = seq_len // blk
    pair = 2 * HEAD_DIM
    qw = 2 * blk
    assert nb % 2 == 0
    return pl.pallas_call(
        functools.partial(_attn_kernel, nb=nb),
        grid=(bsz, N_HEADS // 2, nb // 2),
        in_specs=[
            pl.BlockSpec(memory_space=pltpu.SMEM),
            pl.BlockSpec((1, qw, pair), lambda b, h, i: (b, i, h)),
            pl.BlockSpec((1, qw, pair), lambda b, h, i: (b, jnp.minimum(i + 1, nb // 2 - 1), h)),
            pl.BlockSpec((1, seq_len, pair), lambda b, h, i: (b, 0, h)),
            pl.BlockSpec((1, seq_len, pair), lambda b, h, i: (b, 0, h)),
            pl.BlockSpec((2, 2, blk, blk), lambda b, h, i: (h, 0, 0, 0)),
        ],
        out_specs=pl.BlockSpec((1, qw, pair), lambda b, h, i: (b, i, h)),
        out_shape=jax.ShapeDtypeStruct((bsz, seq_len, D_ATTN), BF16),
        scratch_shapes=[
            pltpu.VMEM((2, nb, pair), F32),
            pltpu.VMEM((2, nb, PV_ROWS, blk), BF16),
            pltpu.VMEM((2, nb, qw), F32),
            pltpu.VMEM((2, nb, qw), F32),
            pltpu.VMEM((2, ATTN_GROUP, 1, qw), F32),
            pltpu.VMEM((2, ATTN_GROUP, blk, qw), F32),
            pltpu.VMEM((2, ATTN_GROUP, blk, qw), F32),
            pltpu.VMEM((2, 1, qw), F32),
            pltpu.VMEM((2, 1, qw), F32),
            pltpu.VMEM((2, 1, qw), F32),
            pltpu.VMEM((2, PV_ROWS, qw), F32),
        ],
        compiler_params=_params(3),
        name="moba_attention",
    )(far_bias, q, q, k, v, bias_tiles)


def _ffn_body(x, g_ref, sc_ref, sh_ref, gate_ref, wg_ref, wu_ref, cw_ref, wd_ref, fin_ref, o_ref,
              gext_sc, carry_sc, acc_sc, h_sc, *, tm, per_seq, final_norm):
    @pl.when(pl.program_id(0) % per_seq == 0)
    def _():
        carry_sc[...] = jnp.zeros_like(carry_sc)

    h_sc[...] = _norm_mod(x, g_ref[...], sc_ref[0], sh_ref[0]).astype(BF16)

    def gate_up(j):
        cols = slice(j * FFN_CHUNK, (j + 1) * FFN_CHUNK)
        h = h_sc[...]
        return (jnp.dot(h, wg_ref[:, cols], preferred_element_type=F32),
                jnp.dot(h, wu_ref[:, cols], preferred_element_type=F32))

    nxt = gate_up(0)
    for j in range(N_FFN_CHUNKS):
        cols = slice(j * FFN_CHUNK, (j + 1) * FFN_CHUNK)
        gpre, up = nxt
        if j + 1 < N_FFN_CHUNKS:
            nxt = gate_up(j + 1)
        lane_tiles = FFN_CHUNK // LANES
        for l in range(lane_tiles):
            gext_sc[j, l, 0:FFN_HALO, :] = carry_sc[j, :, l * LANES:(l + 1) * LANES]
            gext_sc[j, l, FFN_HALO:, :] = gpre[:, l * LANES:(l + 1) * LANES]
        carry_sc[j] = gpre[tm - FFN_HALO:, :]

        def back(n):
            return jnp.concatenate([gext_sc[j, l, FFN_HALO - n:FFN_HALO - n + tm, :] for l in range(lane_tiles)],
                                   axis=-1)

        cw = cw_ref[:, cols]
        conv = cw[0:1] * back(2) + cw[1:2] * back(1) + cw[2:3] * gpre + cw[3:4]
        act = (_silu(conv) * up).astype(BF16)
        part = jnp.dot(act, wd_ref[cols, :], preferred_element_type=F32)
        if j == 0:
            acc_sc[...] = part
        else:
            acc_sc[...] += part
    out = x + gate_ref[0] * acc_sc[...]
    if final_norm:
        ms = jnp.mean(out * out, axis=-1, keepdims=True)
        out = out * lax.rsqrt(ms + EPS) * fin_ref[...]
    o_ref[...] = out


def _ffn0_kernel(x_ref, ys_ref, ya_ref, wo_ref, g1_ref, g_ref, sc_ref, sh_ref, gate_ref,
                 wg_ref, wu_ref, cw_ref, wd_ref, fin_ref, o_ref, gext_sc, carry_sc, acc_sc, h_sc, **kw):
    y = (jnp.dot(ys_ref[...], wo_ref[0:D_SSM, :], preferred_element_type=F32)
         + jnp.dot(ya_ref[...], wo_ref[D_SSM:, :], preferred_element_type=F32))
    x = x_ref[...] + g1_ref[0] * y
    _ffn_body(x, g_ref, sc_ref, sh_ref, gate_ref, wg_ref, wu_ref, cw_ref, wd_ref, fin_ref, o_ref,
              gext_sc, carry_sc, acc_sc, h_sc, **kw)


def _ffn1_kernel(x_ref, g_ref, sc_ref, sh_ref, gate_ref, wg_ref, wu_ref, cw_ref, wd_ref, fin_ref,
                 o_ref, gext_sc, carry_sc, acc_sc, h_sc, **kw):
    _ffn_body(x_ref[...], g_ref, sc_ref, sh_ref, gate_ref, wg_ref, wu_ref, cw_ref, wd_ref, fin_ref, o_ref,
              gext_sc, carry_sc, acc_sc, h_sc, **kw)


def _ffn_weights(w_up, w_gate, dw_w, dw_b, w_down, layer):
    cw = jnp.concatenate([dw_w[layer], dw_b[layer][None, :]], axis=0)
    return w_gate.astype(BF16), w_up.astype(BF16), cw, w_down.astype(BF16), layer


def _conv_ffn(x2, mixer, norm_g, scale, shift, gate, weights, final_g, seq_len, final_norm, tm=256):
    tok, d = x2.shape
    per_seq = seq_len // tm
    wg, wu, cw, wd, layer = weights
    row = lambda i: (i, 0)
    bat = lambda i: (i // per_seq, 0, 0)
    vec = pl.BlockSpec((1, 1, d), bat)
    layer_spec = lambda w: pl.BlockSpec((None,) + w.shape[1:], lambda i: (layer, 0, 0), pipeline_mode=pl.Buffered(1))
    common_specs = [_const_spec((1, d)), vec, vec, vec,
                    layer_spec(wg), layer_spec(wu), _const_spec(cw.shape), layer_spec(wd),
                    _const_spec((1, d))]
    common_args = [norm_g, scale, shift, gate, wg, wu, cw, wd, final_g]
    kw = dict(tm=tm, per_seq=per_seq, final_norm=final_norm)
    if mixer is None:
        body = functools.partial(_ffn1_kernel, **kw)
        specs = [pl.BlockSpec((tm, d), row)] + common_specs
        args = [x2] + common_args
    else:
        ys, ya, wo, g1 = mixer
        body = functools.partial(_ffn0_kernel, **kw)
        specs = [pl.BlockSpec((tm, d), row), pl.BlockSpec((tm, D_SSM), row), pl.BlockSpec((tm, D_ATTN), row),
                 _const_spec(wo.shape), vec] + common_specs
        args = [x2, ys, ya, wo, g1] + common_args
    return pl.pallas_call(
        body,
        grid=(tok // tm,),
        in_specs=specs,
        out_specs=pl.BlockSpec((tm, d), row),
        out_shape=jax.ShapeDtypeStruct((tok, d), F32),
        scratch_shapes=[
            pltpu.VMEM((N_FFN_CHUNKS, FFN_CHUNK // LANES, tm + FFN_HALO, LANES), F32),
            pltpu.VMEM((N_FFN_CHUNKS, FFN_HALO, FFN_CHUNK), F32),
            pltpu.VMEM((tm, d), F32),
            pltpu.VMEM((tm, d), BF16),
        ],
        compiler_params=_params(1),
        name="conv_ffn_final" if final_norm else "conv_ffn",
    )(*args)


def _conformer_kernel(x_ref, g_ref, sc_ref, sh_ref, gate_ref, win_ref, bin_ref, dw_ref, dwb_ref,
                      lng_ref, lnb_ref, wout_ref, bout_ref, o_ref, aext_sc, conv_sc, h_sc, *, tm, per_seq):
    d = x_ref.shape[-1]
    lanes = d // LANES
    x = x_ref[...]
    h_sc[...] = _norm_mod(x, g_ref[...], sc_ref[0], sh_ref[0]).astype(BF16)

    @pl.when(pl.program_id(0) % per_seq == 0)
    def _():
        aext_sc[:, 0:CONV_HALO, :] = jnp.zeros((lanes, CONV_HALO, LANES), F32)

    rc = 32
    off = CONV_HALO - (CONV_WIDTH - 1)
    for c0 in range(0, d, CONV_CHUNK):
        cc = slice(c0, c0 + CONV_CHUNK)
        gc = slice(d + c0, d + c0 + CONV_CHUNK)
        h = h_sc[...]
        half_a = jnp.dot(h, win_ref[:, cc], preferred_element_type=F32) + bin_ref[:, cc]
        half_b = jnp.dot(h, win_ref[:, gc], preferred_element_type=F32) + bin_ref[:, gc]
        a = half_a * jnp.tanh(half_b) + half_a
        for l in range(c0 // LANES, (c0 + CONV_CHUNK) // LANES):
            cols = slice(l * LANES, (l + 1) * LANES)
            aext_sc[l, CONV_HALO:, :] = a[:, l * LANES - c0:(l + 1) * LANES - c0]
            for base in range(0, tm, rc):
                acc = [dwb_ref[:, cols]] * (rc // 8)
                for k in range(CONV_WIDTH):
                    w8 = dw_ref[k, :, cols]
                    for j in range(rc // 8):
                        lo = base + 8 * j + off + k
                        acc[j] = acc[j] + w8 * aext_sc[l, lo:lo + 8, :]
                for j in range(rc // 8):
                    conv_sc[base + 8 * j:base + 8 * j + 8, cols] = acc[j]
            aext_sc[l, 0:CONV_HALO, :] = aext_sc[l, tm:tm + CONV_HALO, :]

    c = conv_sc[...]
    mu = jnp.mean(c, axis=-1, keepdims=True)
    xc = c - mu
    y = xc * lax.rsqrt(jnp.mean(xc * xc, axis=-1, keepdims=True) + EPS)
    half_v = y * lng_ref[...] + lnb_ref[...]
    y = (half_v * jnp.tanh(half_v) + half_v).astype(BF16)
    out = jnp.dot(y, wout_ref[...], preferred_element_type=F32) + bout_ref[...]
    o_ref[...] = x + gate_ref[0] * out


def _conformer(x2, norm_g, scale, shift, gate, w_in, b_in, dw_w, dw_b, ln_g, ln_b, w_out, b_out, seq_len, tm=512):
    tok, d = x2.shape
    per_seq = seq_len // tm
    row = lambda i: (i, 0)
    vec = pl.BlockSpec((1, 1, d), lambda i: (i // per_seq, 0, 0))
    return pl.pallas_call(
        functools.partial(_conformer_kernel, tm=tm, per_seq=per_seq),
        grid=(tok // tm,),
        in_specs=[pl.BlockSpec((tm, d), row), _const_spec((1, d)), vec, vec, vec,
                  _const_spec((d, 2 * d)), _const_spec((1, 2 * d)), _const_spec((CONV_WIDTH, 8, d)), _const_spec((8, d)),
                  _const_spec((1, d)), _const_spec((1, d)), _const_spec((d, d)), _const_spec((1, d))],
        out_specs=pl.BlockSpec((tm, d), row),
        out_shape=jax.ShapeDtypeStruct((tok, d), F32),
        scratch_shapes=[pltpu.VMEM((d // LANES, tm + CONV_HALO, LANES), F32), pltpu.VMEM((tm, d), F32),
                        pltpu.VMEM((tm, d), BF16)],
        compiler_params=_params(1, fuse_inputs=[False] * 5 + [True] + [False] * 5 + [True, False]),
        name="conformer_conv",
    )(x2, norm_g, scale, shift, gate, (0.5 * w_in).astype(BF16), 0.5 * b_in.reshape(1, -1),
      jnp.broadcast_to(dw_w[:, None, :], (CONV_WIDTH, 8, d)), jnp.broadcast_to(dw_b[None, :], (8, d)),
      0.5 * ln_g.reshape(1, -1), 0.5 * ln_b.reshape(1, -1), w_out.astype(BF16), b_out.reshape(1, -1))


def kernel(x, c, mod_w, mod_b, norm_g, final_g, ab_w_in, ssm_a_re, ssm_a_im, ssm_log_dt, ssm_b_re, ssm_b_im, ssm_c_re, ssm_c_im, ssm_d, ssm_glu_w, ssm_glu_b, ab_w_out, rel_bias, cm_w_in, cm_b_in, cm_dw_w, cm_dw_b, cm_ln_g, cm_ln_b, cm_w_out, cm_b_out, ffn_w_up, ffn_w_gate, ffn_dw_w, ffn_dw_b, ffn_w_down):
    bsz, seq_len, d = x.shape
    tok = bsz * seq_len
    x2 = x.reshape(tok, d)
    mod = _modulation(c, mod_w, mod_b)
    vecs = [[mod[l, :, i * d:(i + 1) * d].reshape(bsz, 1, d) for i in range(6)] for l in range(2)]
    fin = final_g.reshape(1, d)

    sh1, sc1, g1, sh2, sc2, g2 = vecs[0]
    u, q, k, v = _in_projection(x2, norm_g[0, 0].reshape(1, d), sc1, sh1, ab_w_in[0].astype(BF16), seq_len)
    ops = _s5_prepare(ssm_a_re[0], ssm_a_im[0], ssm_log_dt[0], ssm_b_re[0], ssm_b_im[0], ssm_c_re[0], ssm_c_im[0])
    y_ssm = _s5_mixer(u.reshape(S5_T, bsz, seq_len // S5_T, D_SSM), ops, ssm_d[0], ssm_glu_w[0], ssm_glu_b[0])
    att = lambda a: a.reshape(bsz, seq_len, D_ATTN)
    y_att = _moba_attention(att(q), att(k), att(v), _bias_tiles(rel_bias), rel_bias[REL_BUCKETS - 1])
    w0 = _ffn_weights(ffn_w_up, ffn_w_gate, ffn_dw_w, ffn_dw_b, ffn_w_down, 0)
    x2 = _conv_ffn(x2, (y_ssm.reshape(tok, D_SSM), y_att.reshape(tok, D_ATTN), ab_w_out[0].astype(BF16), g1),
                   norm_g[0, 1].reshape(1, d), sc2, sh2, g2, w0, fin, seq_len, final_norm=False)

    sh1, sc1, g1, sh2, sc2, g2 = vecs[1]
    x2 = _conformer(x2, norm_g[1, 0].reshape(1, d), sc1, sh1, g1, cm_w_in[0], cm_b_in[0], cm_dw_w[0], cm_dw_b[0],
                    cm_ln_g[0], cm_ln_b[0], cm_w_out[0], cm_b_out[0], seq_len)
    w1 = _ffn_weights(ffn_w_up, ffn_w_gate, ffn_dw_w, ffn_dw_b, ffn_w_down, 1)
    x2 = _conv_ffn(x2, None, norm_g[1, 1].reshape(1, d), sc2, sh2, g2, w1, fin, seq_len, final_norm=True)
    return x2.reshape(bsz, seq_len, d)
```
